```python
import math
import jax, jax.numpy as jnp
from jax import lax
import numpy as np

D_MODEL = 1024
BATCH = 8
SEQ = 2048
DEPTH = 1
DEC_BATCH = 128
DEC_SEQ = 4
PAST_LEN = 8192
PAGE_SIZE = 128

M_HEADS = 4
M_HEAD_DIM = 256
M_WIDTH = M_HEADS * M_HEAD_DIM
M_CHUNK = 64
A_GROUPS = ((128, 1), (512, 4), (2048, 16))
A_N_GROUPS = len(A_GROUPS)
A_HEADS_PER_GROUP = 4
A_HEAD_DIM = 128
A_HEADS = A_N_GROUPS * A_HEADS_PER_GROUP
A_WIDTH = A_HEADS * A_HEAD_DIM
A_OUT = A_HEADS_PER_GROUP * A_HEAD_DIM
ROPE_THETA = 500000.0
ROPE_DIM = A_HEAD_DIM // 4
N_EXPERTS = 32
TOP_K = 4
D_FF = D_MODEL
SWIGLU_ALPHA = 1.702
SWIGLU_LIMIT = 7.0
MOE_BLOCK = 128
PLE_DIM = 256
EPS = 1e-6
PROJ_SIZES = (M_WIDTH,) * 4 + (M_HEADS,) * 2 + (A_WIDTH,) * 3 + (D_MODEL,) * 2
IN_COLS = sum(PROJ_SIZES)

kernel_name = 'hybrid_mlstm_dilated_attn_moe_step'


def rmsnorm(x, w):
    xf = x.astype(jnp.float32)
    y = xf * lax.rsqrt(jnp.mean(xf * xf, axis=-1, keepdims=True) + EPS)
    return (y * w.astype(jnp.float32)).astype(x.dtype)


def rope(x, pos):
    inv = 1.0 / (ROPE_THETA ** (jnp.arange(0, ROPE_DIM, 2, dtype=jnp.float32) / ROPE_DIM))
    ang = pos[:, None] * inv[None, :]
    cos = jnp.cos(ang)[None, :, None, :]
    sin = jnp.sin(ang)[None, :, None, :]
    half = ROPE_DIM // 2
    xr = x[..., :ROPE_DIM].astype(jnp.float32)
    x1, x2 = xr[..., :half], xr[..., half:]
    rot = jnp.concatenate([x1 * cos - x2 * sin, x2 * cos + x1 * sin], axis=-1)
    return jnp.concatenate([rot.astype(x.dtype), x[..., ROPE_DIM:]], axis=-1)


def split_proj(xn, w_in):
    z = xn @ w_in
    cuts = [int(c) for c in np.cumsum(PROJ_SIZES)[:-1]]
    return jnp.split(z, cuts, axis=-1)


def mlstm_chunkwise(q, k, v, ig, lf, C0, n0, m0):
    B, T, H, E = q.shape
    L = math.gcd(T, M_CHUNK)
    nc = T // L

    def to_chunks(a):
        a = a.reshape((B, nc, L) + a.shape[2:])
        return jnp.swapaxes(jnp.moveaxis(a, 1, 0), 2, 3)

    causal = jnp.tril(jnp.ones((L, L), dtype=bool))

    def step(carry, inp):
        C, n, m = carry
        qc, kc, vc, ic, fc = inp
        b = jnp.cumsum(fc, axis=-1)
        dmat = jnp.where(causal, b[..., :, None] - b[..., None, :] + ic[..., None, :], -jnp.inf)
        inter = b + m[..., None]
        mj = jnp.maximum(inter, jnp.max(dmat, axis=-1))
        sc = jnp.einsum('bhje,bhse->bhjs', qc, kc) * jnp.exp(dmat - mj[..., None])
        a_int = jnp.exp(inter - mj)
        num = jnp.einsum('bhjs,bhsd->bhjd', sc, vc) + a_int[..., None] * jnp.einsum('bhde,bhje->bhjd', C, qc)
        den = jnp.sum(sc, axis=-1) + a_int * jnp.einsum('bhe,bhje->bhj', n, qc)
        h = num / jnp.maximum(jnp.abs(den), jnp.exp(-mj))[..., None]
        bl = b[..., -1]
        g = bl[..., None] - b + ic
        m_new = jnp.maximum(bl + m, jnp.max(g, axis=-1))
        ws = jnp.exp(g - m_new[..., None])
        a_c = jnp.exp(bl + m - m_new)
        C_new = a_c[..., None, None] * C + jnp.einsum('bhs,bhsd,bhse->bhde', ws, vc, kc)
        n_new = a_c[..., None] * n + jnp.einsum('bhs,bhse->bhe', ws, kc)
        return (C_new, n_new, m_new), h

    xs = (to_chunks(q), to_chunks(k), to_chunks(v), to_chunks(ig), to_chunks(lf))
    (C, n, m), hs = lax.scan(step, (C0, n0, m0), xs)
    h = jnp.transpose(hs, (1, 0, 3, 2, 4)).reshape(B, T, H, E)
    return h, C, n, m


def mlstm_branch(mq, mk, mv, mo, mi, mf, norm_w, C0, n0, m0):
    B, T, _ = mq.shape
    f32 = jnp.float32
    shp = (B, T, M_HEADS, M_HEAD_DIM)
    q = mq.reshape(shp).astype(f32)
    k = mk.reshape(shp).astype(f32) * (M_HEAD_DIM ** -0.5)
    v = mv.reshape(shp).astype(f32)
    ig = mi.astype(f32)
    lf = jax.nn.log_sigmoid(mf.astype(f32))
    h, C, n, m = mlstm_chunkwise(q, k, v, ig, lf, C0, n0, m0)
    h = h * jax.nn.sigmoid(mo.astype(f32)).reshape(shp)
    h = h * lax.rsqrt(jnp.mean(h * h, axis=-1, keepdims=True) + EPS)
    h = h.reshape(B, T, M_WIDTH) * norm_w.astype(f32)
    return h.astype(mq.dtype), C, n, m


def attn_qkv(aq, ak, av, pos):
    B, T, _ = aq.shape
    shp = (B, T, A_HEADS, A_HEAD_DIM)
    return rope(aq.reshape(shp), pos), rope(ak.reshape(shp), pos), av.reshape(shp)


def masked_softmax_stats(s, valid):
    s = jnp.where(valid, s, -jnp.inf)
    mx = jnp.max(s, axis=-1, keepdims=True)
    p = jnp.exp(s - mx)
    l = jnp.sum(p, axis=-1)
    return p, l, mx[..., 0] + jnp.log(l)


def dilated_group_prompt(q, k, v, dil, window):
    B, S, H, E = q.shape
    nw = window // dil
    blk = nw
    L = S // dil
    nb = -(-L // blk)
    Lp = nb * blk

    def strided(a):
        return a.reshape(B, L, dil, H, E).transpose(0, 2, 3, 1, 4)

    qs = jnp.pad(strided(q), ((0, 0), (0, 0), (0, 0), (0, Lp - L), (0, 0))).reshape(B, dil, H, nb, blk, E)

    def band(a):
        a = jnp.pad(strided(a), ((0, 0), (0, 0), (0, 0), (blk, Lp - L), (0, 0))).reshape(B, dil, H, nb + 1, blk, E)
        return jnp.concatenate([a[:, :, :, :-1], a[:, :, :, 1:]], axis=-2)

    kb, vb = band(k), band(v)
    s = jnp.einsum('bdhnqe,bdhnke->bdhnqk', qs, kb, preferred_element_type=jnp.float32) * (E ** -0.5)
    qi = jnp.arange(blk)[:, None]
    ki = jnp.arange(2 * blk)[None, :]
    dist = qi + blk - ki
    kpos = jnp.arange(nb)[:, None, None] * blk + ki[None] - blk
    valid = (dist >= 0)[None] & (dist <= nw)[None] & (kpos >= 0)
    p, l, lse = masked_softmax_stats(s, valid)
    o = jnp.einsum('bdhnqk,bdhnke->bdhnqe', p, vb.astype(jnp.float32)) / l[..., None]
    o = o.reshape(B, dil, H, Lp, E)[:, :, :, :L].transpose(0, 3, 1, 2, 4).reshape(B, S, H, E)
    lse = lse.reshape(B, dil, H, Lp)[..., :L].transpose(0, 3, 1, 2).reshape(B, S, H)
    return o, lse


def dilated_group_sample(q, k, v, kbuf, vbuf, dil, window):
    B, T, H, E = q.shape
    Wb = kbuf.shape[1]
    nw = window // dil
    kf = jnp.concatenate([kbuf, k.astype(kbuf.dtype)], axis=1)
    vf = jnp.concatenate([vbuf, v.astype(vbuf.dtype)], axis=1)
    idx = Wb + jnp.arange(T)[:, None] - dil * jnp.arange(nw + 1)[None, :]
    valid = idx >= 0
    idx = jnp.maximum(idx, 0)
    kg = kf[:, idx]
    vg = vf[:, idx]
    s = jnp.einsum('bthe,btjhe->bthj', q, kg, preferred_element_type=jnp.float32) * (E ** -0.5)
    p, l, lse = masked_softmax_stats(s, valid[None, :, None, :])
    o = jnp.einsum('bthj,btjhe->bthe', p, vg.astype(jnp.float32)) / l[..., None]
    return o, lse, kf[:, -Wb:], vf[:, -Wb:]


def combine_groups(outs, lses, dtype):
    o = jnp.stack(outs)
    alpha = jax.nn.softmax(jnp.stack(lses), axis=0)
    y = jnp.einsum('gbth,gbthe->bthe', alpha, o)
    B, T = y.shape[:2]
    return y.reshape(B, T, A_OUT).astype(dtype)


def clamped_swiglu(h):
    x_glu, x_lin = jnp.split(h, 2, axis=-1)
    x_glu = jnp.minimum(x_glu, SWIGLU_LIMIT)
    x_lin = jnp.clip(x_lin, -SWIGLU_LIMIT, SWIGLU_LIMIT)
    return x_glu * jax.nn.sigmoid(SWIGLU_ALPHA * x_glu) * (x_lin + 1.0)


def moe(xn, w_router, b_router, w1, b1, w2, b2):
    N, D = xn.shape
    logits = (xn @ w_router).astype(jnp.float32) + b_router.astype(jnp.float32)
    top_v, top_e = lax.top_k(logits, TOP_K)
    gate = jax.nn.softmax(top_v, axis=-1)
    NA = N * TOP_K
    e_flat = top_e.reshape(-1).astype(jnp.int32)
    tok_flat = jnp.repeat(jnp.arange(N, dtype=jnp.int32), TOP_K)
    order = jnp.argsort(e_flat)
    e_sorted = e_flat[order]
    tok_sorted = tok_flat[order]
    gate_sorted = gate.reshape(-1)[order]
    counts = jnp.bincount(e_flat, length=N_EXPERTS).astype(jnp.int32)
    padded = ((counts + MOE_BLOCK - 1) // MOE_BLOCK) * MOE_BLOCK
    start = jnp.cumsum(counts) - counts
    pend = jnp.cumsum(padded)
    pstart = pend - padded
    dest = pstart[e_sorted] + jnp.arange(NA, dtype=jnp.int32) - start[e_sorted]
    n_blocks = -(-NA // MOE_BLOCK) + N_EXPERTS
    P = n_blocks * MOE_BLOCK
    slot_tok = jnp.full((P,), N, dtype=jnp.int32).at[dest].set(tok_sorted)
    blk_start = jnp.arange(n_blocks, dtype=jnp.int32) * MOE_BLOCK
    blk_exp = jnp.minimum(jnp.searchsorted(pend, blk_start, side='right'), N_EXPERTS - 1)
    x_pad = jnp.concatenate([xn, jnp.zeros((1, D), xn.dtype)], axis=0)

    def run_block(args):
        toks, e = args
        h = x_pad[toks] @ w1[e] + b1[e]
        return clamped_swiglu(h) @ w2[e] + b2[e]

    ys = lax.map(run_block, (slot_tok.reshape(n_blocks, MOE_BLOCK), blk_exp)).reshape(P, D)
    y_assign = ys[dest].astype(jnp.float32) * gate_sorted[:, None]
    return jax.ops.segment_sum(y_assign, tok_sorted, num_segments=N).astype(xn.dtype)


def finish(h, hA, hB, ga, gb, p, w_pa, w_pb, w_o, norm2, w_router, b_router, w1, b1, w2, b2, norm3, w_ple, w_ple_gate):
    u = jax.nn.sigmoid(ga) * (hA @ w_pa) + jax.nn.sigmoid(gb) * (hB @ w_pb)
    h = h + u @ w_o
    B, T, D = h.shape
    h = h + moe(rmsnorm(h, norm2).reshape(B * T, D), w_router, b_router, w1, b1, w2, b2).reshape(B, T, D)
    h = h + (p.astype(h.dtype) @ w_ple) * jax.nn.sigmoid(rmsnorm(h, norm3) @ w_ple_gate)
    return h


def setup_inputs(seed: int = 0) -> dict:
    key = jax.random.key(seed)
    ks = list(jax.random.split(key, 48))
    f32 = jnp.float32
    D = D_MODEL

    def nrm(shape, scale=1.0):
        return jax.random.normal(ks.pop(), shape, f32) * scale

    def gain(shape):
        return 1.0 + nrm(shape, 0.05)

    inp = {}
    inp['x_prompt'] = nrm((BATCH, SEQ, D))
    inp['x_sample'] = nrm((DEC_BATCH, DEC_SEQ, D))
    inp['state_C'] = nrm((DEPTH, DEC_BATCH, M_HEADS, M_HEAD_DIM, M_HEAD_DIM), 0.1)
    inp['state_n'] = nrm((DEPTH, DEC_BATCH, M_HEADS, M_HEAD_DIM), 0.1)
    inp['state_m'] = nrm((DEPTH, DEC_BATCH, M_HEADS))
    for g, (win, dil) in enumerate(A_GROUPS):
        wb = min(win, PAST_LEN)
        inp['cache_k' + str(g)] = nrm((DEPTH, DEC_BATCH, wb, A_HEADS_PER_GROUP, A_HEAD_DIM))
        inp['cache_v' + str(g)] = nrm((DEPTH, DEC_BATCH, wb, A_HEADS_PER_GROUP, A_HEAD_DIM))
    inp['p_prompt'] = nrm((DEPTH, BATCH, SEQ, PLE_DIM))
    inp['p_sample'] = nrm((DEPTH, DEC_BATCH, DEC_SEQ, PLE_DIM))
    inp['norm1'] = gain((DEPTH, D))
    inp['w_in'] = nrm((DEPTH, D, IN_COLS), D ** -0.5)
    inp['b_igate'] = nrm((DEPTH, M_HEADS), 0.1)
    inp['b_fgate'] = jnp.linspace(3.0, 6.0, M_HEADS, dtype=f32)[None, :] + nrm((DEPTH, M_HEADS), 0.1)
    inp['m_norm'] = gain((DEPTH, M_WIDTH))
    inp['w_pa'] = nrm((DEPTH, M_WIDTH, D), M_WIDTH ** -0.5)
    inp['w_pb'] = nrm((DEPTH, A_OUT, D), A_OUT ** -0.5)
    inp['w_o'] = nrm((DEPTH, D, D), D ** -0.5)
    inp['norm2'] = gain((DEPTH, D))
    inp['w_router'] = nrm((DEPTH, D, N_EXPERTS), D ** -0.5)
    inp['b_router'] = nrm((DEPTH, N_EXPERTS), 0.01)
    inp['w1'] = nrm((DEPTH, N_EXPERTS, D, 2 * D_FF), D ** -0.5)
    inp['b1'] = nrm((DEPTH, N_EXPERTS, 2 * D_FF), 0.01)
    inp['w2'] = nrm((DEPTH, N_EXPERTS, D_FF, D), D_FF ** -0.5)
    inp['b2'] = nrm((DEPTH, N_EXPERTS, D), 0.01)
    inp['norm3'] = gain((DEPTH, D))
    inp['w_ple'] = nrm((DEPTH, PLE_DIM, D), PLE_DIM ** -0.5)
    inp['w_ple_gate'] = nrm((DEPTH, D, D), D ** -0.5)
    inp['norm_f'] = gain((D,))
    return inp


def reference(x_prompt, x_sample, state_C, state_n, state_m, cache_k0, cache_v0, cache_k1, cache_v1,
              cache_k2, cache_v2, p_prompt, p_sample, norm1, w_in, b_igate, b_fgate, m_norm, w_pa, w_pb,
              w_o, norm2, w_router, b_router, w1, b1, w2, b2, norm3, w_ple, w_ple_gate, norm_f):
    f32 = jnp.float32
    Bp, S, _ = x_prompt.shape
    T = x_sample.shape[1]
    pos_p = jnp.arange(S, dtype=f32)
    pos_s = jnp.arange(T, dtype=f32) + float(PAST_LEN)
    ck_in = (cache_k0, cache_k1, cache_k2)
    cv_in = (cache_v0, cache_v1, cache_v2)
    pC, pn, pm, sC, sn, sm = [], [], [], [], [], []
    pk = [[] for _ in A_GROUPS]
    pv = [[] for _ in A_GROUPS]
    sk = [[] for _ in A_GROUPS]
    sv = [[] for _ in A_GROUPS]
    hp, hs = x_prompt, x_sample
    for i in range(DEPTH):
        post = (w_pa[i], w_pb[i], w_o[i], norm2[i], w_router[i], b_router[i], w1[i], b1[i], w2[i], b2[i],
                norm3[i], w_ple[i], w_ple_gate[i])
        xn = rmsnorm(hp, norm1[i])
        mq, mk, mv, mo, mi, mf, aq, ak, av, ga, gb = split_proj(xn, w_in[i])
        hA, C, n, m = mlstm_branch(mq, mk, mv, mo, mi + b_igate[i], mf + b_fgate[i], m_norm[i],
                                   jnp.zeros((Bp, M_HEADS, M_HEAD_DIM, M_HEAD_DIM), f32),
                                   jnp.zeros((Bp, M_HEADS, M_HEAD_DIM), f32),
                                   jnp.zeros((Bp, M_HEADS), f32))
        pC.append(C)
        pn.append(n)
        pm.append(m)
        q, k, v = attn_qkv(aq, ak, av, pos_p)
        outs, lses = [], []
        for g, (win, dil) in enumerate(A_GROUPS):
            sl = slice(g * A_HEADS_PER_GROUP, (g + 1) * A_HEADS_PER_GROUP)
            o, l = dilated_group_prompt(q[:, :, sl], k[:, :, sl], v[:, :, sl], dil, win)
            outs.append(o)
            lses.append(l)
            keep = min(win, S)
            pk[g].append(k[:, S - keep:, sl])
            pv[g].append(v[:, S - keep:, sl])
        hB = combine_groups(outs, lses, hp.dtype)
        hp = finish(hp, hA, hB, ga, gb, p_prompt[i], *post)
        xn = rmsnorm(hs, norm1[i])
        mq, mk, mv, mo, mi, mf, aq, ak, av, ga, gb = split_proj(xn, w_in[i])
        hA, C, n, m = mlstm_branch(mq, mk, mv, mo, mi + b_igate[i], mf + b_fgate[i], m_norm[i],
                                   state_C[i].astype(f32), state_n[i].astype(f32), state_m[i].astype(f32))
        sC.append(C)
        sn.append(n)
        sm.append(m)
        q, k, v = attn_qkv(aq, ak, av, pos_s)
        outs, lses = [], []
        for g, (win, dil) in enumerate(A_GROUPS):
            sl = slice(g * A_HEADS_PER_GROUP, (g + 1) * A_HEADS_PER_GROUP)
            o, l, kb, vb = dilated_group_sample(q[:, :, sl], k[:, :, sl], v[:, :, sl], ck_in[g][i], cv_in[g][i], dil, win)
            outs.append(o)
            lses.append(l)
            sk[g].append(kb)
            sv[g].append(vb)
        hB = combine_groups(outs, lses, hs.dtype)
        hs = finish(hs, hA, hB, ga, gb, p_sample[i], *post)
    y_prompt = rmsnorm(hp, norm_f)
    y_sample = rmsnorm(hs, norm_f)
    return (y_prompt, y_sample,
            jnp.stack(pC), jnp.stack(pn), jnp.stack(pm),
            jnp.stack(pk[0]), jnp.stack(pv[0]), jnp.stack(pk[1]), jnp.stack(pv[1]), jnp.stack(pk[2]), jnp.stack(pv[2]),
            jnp.stack(sC), jnp.stack(sn), jnp.stack(sm),
            jnp.stack(sk[0]), jnp.stack(sv[0]), jnp.stack(sk[1]), jnp.stack(sv[1]), jnp.stack(sk[2]), jnp.stack(sv[2]))
```

```python
import functools
import math

import jax
import jax.numpy as jnp
import numpy as np
from jax import lax
from jax.experimental import pallas as pl
from jax.experimental.pallas import tpu as pltpu

F32 = jnp.float32
BF16 = jnp.bfloat16

D_MODEL = 1024
PAST_LEN = 8192
M_HEADS = 4
M_HEAD_DIM = 256
M_WIDTH = M_HEADS * M_HEAD_DIM
A_GROUPS = ((128, 1), (512, 4), (2048, 16))
A_HEADS_PER_GROUP = 4
A_HEAD_DIM = 128
A_GROUP_WIDTH = A_HEADS_PER_GROUP * A_HEAD_DIM
A_WIDTH = len(A_GROUPS) * A_GROUP_WIDTH
ROPE_THETA = 500000.0
ROPE_DIM = A_HEAD_DIM // 4
N_EXPERTS = 32
TOP_K = 4
D_FF = D_MODEL
SWIGLU_ALPHA = 1.702
SWIGLU_LIMIT = 7.0
PLE_DIM = 256
EPS = 1e-6

LANES = 128
VMEM_LIMIT = 56 * 1024 * 1024
NEG_INF = float("-inf")


def _cparams(sem):
    return pltpu.CompilerParams(dimension_semantics=sem, vmem_limit_bytes=VMEM_LIMIT)


def _rms(x, gain):
    return x * lax.rsqrt(jnp.mean(x * x, axis=-1, keepdims=True) + EPS) * gain


def _sigmoid(x):
    return 1.0 / (1.0 + jnp.exp(-x))


def _log_sigmoid(x):
    return jnp.minimum(x, 0.0) - jnp.log(1.0 + jnp.exp(-jnp.abs(x)))


def _norm_proj_kernel(x_ref, g_ref, w_ref, *rest, n_rope_blocks, heads_per_block):
    if n_rope_blocks:
        cos_ref, sinm_ref, sinp_ref, o_ref, xn_ref = rest
    else:
        o_ref, xn_ref = rest
    j = pl.program_id(1)

    @pl.when(j == 0)
    def _():
        xn_ref[...] = _rms(x_ref[...], g_ref[...]).astype(BF16)

    z = jnp.dot(xn_ref[...], w_ref[...], preferred_element_type=F32)
    if n_rope_blocks:
        @pl.when(j < n_rope_blocks)
        def _():
            cosf, sinm, sinp = cos_ref[...], sinm_ref[...], sinp_ref[...]
            for h in range(heads_per_block):
                zh = z[:, h * LANES:(h + 1) * LANES]
                rot = (zh * cosf + pltpu.roll(zh, LANES - ROPE_DIM // 2, 1) * sinm
                       + pltpu.roll(zh, ROPE_DIM // 2, 1) * sinp)
                o_ref[:, h * LANES:(h + 1) * LANES] = rot.astype(o_ref.dtype)

        @pl.when(j >= n_rope_blocks)
        def _():
            o_ref[...] = z.astype(o_ref.dtype)
    else:
        o_ref[...] = z.astype(o_ref.dtype)


def norm_proj(x, gain, w, out_dtype, tm, tn, rope=None, n_rope_blocks=0):
    n, d = x.shape
    ncol = w.shape[1]
    in_specs = [pl.BlockSpec((tm, d), lambda i, j: (i, 0)),
                pl.BlockSpec((1, d), lambda i, j: (0, 0)),
                pl.BlockSpec((d, tn), lambda i, j: (0, j))]
    args = [x, gain.reshape(1, d), w]
    if n_rope_blocks:
        in_specs += [pl.BlockSpec((tm, LANES), lambda i, j: (i, 0))] * 3
        args += list(rope)
    return pl.pallas_call(
        functools.partial(_norm_proj_kernel, n_rope_blocks=n_rope_blocks, heads_per_block=tn // LANES),
        grid=(n // tm, ncol // tn),
        in_specs=in_specs,
        out_specs=pl.BlockSpec((tm, tn), lambda i, j: (i, j)),
        out_shape=jax.ShapeDtypeStruct((n, ncol), out_dtype),
        scratch_shapes=[pltpu.VMEM((tm, d), BF16)],
        compiler_params=_cparams(("parallel", "arbitrary")),
        name="norm_proj",
    )(*args)


def rope_tables(pos):
    half = ROPE_DIM // 2
    inv = 1.0 / (ROPE_THETA ** (jnp.arange(0, ROPE_DIM, 2, dtype=F32) / ROPE_DIM))
    ang = pos[:, None] * inv[None, :]
    cos, sin = jnp.cos(ang), jnp.sin(ang)
    n = pos.shape[0]
    ones = jnp.ones((n, LANES - ROPE_DIM), F32)
    zeros = jnp.zeros((n, LANES - ROPE_DIM), F32)
    zh = jnp.zeros((n, half), F32)
    cosf = jnp.concatenate([cos, cos, ones], axis=1)
    sinm = jnp.concatenate([-sin, zh, zeros], axis=1)
    sinp = jnp.concatenate([zh, sin, zeros], axis=1)
    return cosf, sinm, sinp


def _mlstm_kernel(bias_ref, q_ref, k_ref, v_ref, o_ref, gi_ref, gfr_ref, gfc_ref, mn_ref, *rest,
                  chunk, valid_len, has_state):
    if has_state:
        c0_ref, n0_ref, m0_ref, h_ref, c_out, n_out, m_out, c_s, n_s, m_s = rest
    else:
        h_ref, c_out, n_out, m_out, c_s, n_s, m_s = rest
    hd = pl.program_id(1)
    c = pl.program_id(2)
    L = chunk

    @pl.when(c == 0)
    def _():
        if has_state:
            c_s[...] = c0_ref[0, 0]
            n_s[...] = n0_ref[0, 0]
            m_s[...] = m0_ref[0, 0]
        else:
            c_s[...] = jnp.zeros_like(c_s)
            n_s[...] = jnp.zeros_like(n_s)
            m_s[...] = jnp.zeros_like(m_s)

    q = q_ref[...]
    k = k_ref[...] * (M_HEAD_DIM ** -0.5)
    v = v_ref[...]
    b_i = bias_ref[0, hd]
    b_f = bias_ref[1, hd]
    i_row = gi_ref[0, 0] + b_i
    lf_row = _log_sigmoid(gfr_ref[0, 0] + b_f)
    lf_col = _log_sigmoid(gfc_ref[0, 0] + b_f)
    row_id = lax.broadcasted_iota(jnp.int32, (L, L), 0)
    col_id = lax.broadcasted_iota(jnp.int32, (L, L), 1)
    if valid_len < L:
        lane = lax.broadcasted_iota(jnp.int32, (1, L), 1)
        sub = lax.broadcasted_iota(jnp.int32, (L, 1), 0)
        i_row = jnp.where(lane < valid_len, i_row, NEG_INF)
        lf_row = jnp.where(lane < valid_len, lf_row, 0.0)
        lf_col = jnp.where(sub < valid_len, lf_col, 0.0)
    causal = col_id <= row_id
    b_col = jnp.sum(jnp.where(causal, lf_row, 0.0), axis=1, keepdims=True)
    b_row = jnp.sum(jnp.where(row_id <= col_id, lf_col, 0.0), axis=0, keepdims=True)
    m_prev = m_s[...]
    dmat = jnp.where(causal, b_col - b_row + i_row, NEG_INF)
    inter = b_col + m_prev
    mj = jnp.maximum(inter, jnp.max(dmat, axis=1, keepdims=True))
    s = lax.dot_general(q, k, (((1,), (1,)), ((), ())), preferred_element_type=F32)
    sc = s * jnp.exp(dmat - mj)
    a_int = jnp.exp(inter - mj)
    c_prev = c_s[...]
    n_prev = n_s[...]
    qc = lax.dot_general(q, c_prev.astype(BF16), (((1,), (1,)), ((), ())), preferred_element_type=F32)
    num = jnp.dot(sc.astype(BF16), v, preferred_element_type=F32) + a_int * qc
    qn = jnp.sum(q.astype(F32) * n_prev, axis=1, keepdims=True)
    den = jnp.sum(sc, axis=1, keepdims=True) + a_int * qn
    h = num / jnp.maximum(jnp.abs(den), jnp.exp(-mj))
    h = h * _sigmoid(o_ref[...].astype(F32))
    h = h * lax.rsqrt(jnp.mean(h * h, axis=-1, keepdims=True) + EPS) * mn_ref[...]
    h_ref[...] = h.astype(h_ref.dtype)

    bl = jnp.sum(lf_row, axis=1, keepdims=True)
    g_row = bl - b_row + i_row
    m_new = jnp.maximum(bl + m_prev, jnp.max(g_row, axis=1, keepdims=True))
    ws_row = jnp.exp(g_row - m_new)
    a_c = jnp.exp(bl + m_prev - m_new)
    vt = (v.astype(F32).T * ws_row).astype(BF16)
    c_new = a_c * c_prev + jnp.dot(vt, k, preferred_element_type=F32)
    ws8 = jnp.broadcast_to(ws_row, (8, L)).astype(BF16)
    n_new = a_c * n_prev + jnp.dot(ws8, k, preferred_element_type=F32)[0:1]
    c_s[...] = c_new
    n_s[...] = n_new
    m_s[...] = m_new

    @pl.when(c == pl.num_programs(2) - 1)
    def _():
        c_out[0, 0] = c_new
        n_out[0, 0] = n_new
        m_out[0, 0] = m_new


def mlstm(zm, gates_row, gates_col, bias, m_norm, batch, seq, chunk, valid_len, row0, state=None):
    E = M_HEAD_DIM
    nc = seq // chunk
    blk0 = row0 // chunk

    def zspec(col):
        return pl.BlockSpec((chunk, E), lambda b, h, c: (blk0 + b * nc + c, col * M_HEADS + h))

    in_specs = [pl.BlockSpec(memory_space=pltpu.SMEM),
                zspec(0), zspec(1), zspec(2), zspec(3),
                pl.BlockSpec((1, 1, 1, chunk), lambda b, h, c: (b * M_HEADS + h, c, 0, 0)),
                pl.BlockSpec((1, 1, 1, chunk), lambda b, h, c: (b * M_HEADS + h, c, 0, 0)),
                pl.BlockSpec((1, 1, chunk, 1), lambda b, h, c: (b * M_HEADS + h, c, 0, 0)),
                pl.BlockSpec((1, E), lambda b, h, c: (0, h))]
    gi_row, gf_row = gates_row
    args = [bias, zm, zm, zm, zm, gi_row, gf_row, gates_col, m_norm.reshape(1, M_WIDTH)]
    if state is not None:
        c0, n0, m0 = state
        in_specs += [pl.BlockSpec((1, 1, E, E), lambda b, h, c: (b, h, 0, 0)),
                     pl.BlockSpec((1, 1, 1, E), lambda b, h, c: (b, h, 0, 0)),
                     pl.BlockSpec((1, 1, 1, 1), lambda b, h, c: (b, h, 0, 0))]
        args += [c0, n0.reshape(batch, M_HEADS, 1, E), m0.reshape(batch, M_HEADS, 1, 1)]
    out_specs = [pl.BlockSpec((chunk, E), lambda b, h, c: (b * nc + c, h)),
                 pl.BlockSpec((1, 1, E, E), lambda b, h, c: (b, h, 0, 0)),
                 pl.BlockSpec((1, 1, 1, E), lambda b, h, c: (b, h, 0, 0)),
                 pl.BlockSpec((1, 1, 1, 1), lambda b, h, c: (b, h, 0, 0))]
    out_shape = [jax.ShapeDtypeStruct((batch * seq, M_WIDTH), BF16),
                 jax.ShapeDtypeStruct((batch, M_HEADS, E, E), F32),
                 jax.ShapeDtypeStruct((batch, M_HEADS, 1, E), F32),
                 jax.ShapeDtypeStruct((batch, M_HEADS, 1, 1), F32)]
    h, c_f, n_f, m_f = pl.pallas_call(
        functools.partial(_mlstm_kernel, chunk=chunk, valid_len=valid_len, has_state=state is not None),
        grid=(batch, M_HEADS, nc),
        in_specs=in_specs,
        out_specs=out_specs,
        out_shape=out_shape,
        scratch_shapes=[pltpu.VMEM((E, E), F32), pltpu.VMEM((1, E), F32), pltpu.VMEM((1, 1), F32)],
        compiler_params=_cparams(("parallel", "parallel", "arbitrary")),
        name="mlstm",
    )(*args)
    return h, c_f, n_f.reshape(batch, M_HEADS, E), m_f.reshape(batch, M_HEADS)


ATT_BLK = 128


def _band_block(qb, kcat, vcat, mask):
    s = lax.dot_general(qb, kcat, (((1,), (1,)), ((), ())), preferred_element_type=F32)
    s = jnp.where(mask, s, NEG_INF)
    mx = jnp.max(s, axis=1, keepdims=True)
    p = jnp.exp(s - mx)
    l = jnp.sum(p, axis=1, keepdims=True)
    o = jnp.dot(p.astype(BF16), vcat, preferred_element_type=F32) / l
    return o, mx + jnp.log(l)


def _attn_prompt_kernel(*refs, seq):
    qkv = refs[:9]
    y_ref = refs[9]
    o_scr = refs[10:13]
    lse_scr = refs[13:16]
    scale = A_HEAD_DIM ** -0.5
    qi = lax.broadcasted_iota(jnp.int32, (ATT_BLK, ATT_BLK), 0)
    ki = lax.broadcasted_iota(jnp.int32, (ATT_BLK, ATT_BLK), 1)
    cur_mask = ki <= qi
    prev_mask = ki >= qi
    band_mask = jnp.concatenate([prev_mask, cur_mask], axis=1)
    for g, (_, dil) in enumerate(A_GROUPS):
        q_ref, k_ref, v_ref = qkv[3 * g:3 * g + 3]
        L = seq // dil
        nb = L // ATT_BLK

        def residue(r, carry, q_ref=q_ref, k_ref=k_ref, v_ref=v_ref, dil=dil, L=L, nb=nb, g=g):
            def rows(first_blk, n_blk):
                if dil == 1:
                    return pl.ds(first_blk * ATT_BLK, n_blk * ATT_BLK)
                return pl.ds(r + first_blk * ATT_BLK * dil, n_blk * ATT_BLK, stride=dil)

            for n in range(nb):
                qb = (q_ref[rows(n, 1), :] * scale).astype(BF16)
                if n == 0:
                    kk, vv, mask = k_ref[rows(0, 1), :], v_ref[rows(0, 1), :], cur_mask
                else:
                    kk, vv, mask = k_ref[rows(n - 1, 2), :], v_ref[rows(n - 1, 2), :], band_mask
                o, lse = _band_block(qb, kk.astype(BF16), vv.astype(BF16), mask)
                o_scr[g][rows(n, 1), :] = o
                lse_scr[g][rows(n, 1), :] = lse
            return carry

        if dil == 1:
            residue(0, 0)
        else:
            lax.fori_loop(0, dil, residue, 0)
    l0, l1, l2 = lse_scr[0][...], lse_scr[1][...], lse_scr[2][...]
    mx = jnp.maximum(jnp.maximum(l0, l1), l2)
    w0, w1, w2 = jnp.exp(l0 - mx), jnp.exp(l1 - mx), jnp.exp(l2 - mx)
    y = (w0 * o_scr[0][...] + w1 * o_scr[1][...] + w2 * o_scr[2][...]) / (w0 + w1 + w2)
    y_ref[...] = y.astype(y_ref.dtype)


def attn_prompt(zatt, batch, seq):
    nh = A_WIDTH // LANES

    def spec(col0):
        return pl.BlockSpec((seq, LANES), lambda b, j: (b, col0 + j))

    in_specs, args = [], []
    for g in range(len(A_GROUPS)):
        for part in range(3):
            in_specs.append(spec(part * nh + g * A_HEADS_PER_GROUP))
            args.append(zatt)
    return pl.pallas_call(
        functools.partial(_attn_prompt_kernel, seq=seq),
        grid=(batch, A_HEADS_PER_GROUP),
        in_specs=in_specs,
        out_specs=pl.BlockSpec((seq, LANES), lambda b, j: (b, j)),
        out_shape=jax.ShapeDtypeStruct((batch * seq, A_GROUP_WIDTH), BF16),
        scratch_shapes=[pltpu.VMEM((seq, LANES), F32)] * 3 + [pltpu.VMEM((seq, 1), F32)] * 3,
        compiler_params=_cparams(("parallel", "parallel")),
        name="attn_prompt",
    )(*args)


def _attn_sample_kernel(new_ref, *refs, t_dec):
    caches = refs[:6]
    y_ref = refs[6]
    nh = A_WIDTH // LANES
    scale = A_HEAD_DIM ** -0.5
    jj = lax.broadcasted_iota(jnp.int32, (ATT_BLK, 1, 1), 0)
    for t in range(t_dec):
        outs, lses = [], []
        for g, (_, dil) in enumerate(A_GROUPS):
            kc_ref, vc_ref = caches[2 * g], caches[2 * g + 1]
            h0 = g * A_HEADS_PER_GROUP
            q = new_ref[0, t, h0:h0 + A_HEADS_PER_GROUP, :] * scale
            kc = kc_ref[0, :, t % dil]
            vc = vc_ref[0, :, t % dil]
            s_c = jnp.sum(kc * q[None], axis=-1, keepdims=True)
            if t // dil > 0:
                s_c = jnp.where(jj >= t // dil, s_c, NEG_INF)
            mx = jnp.max(s_c, axis=0)
            new_u = [u for u in range(t + 1) if (t - u) % dil == 0]
            s_new = []
            for u in new_u:
                k_u = new_ref[0, u, nh + h0:nh + h0 + A_HEADS_PER_GROUP, :]
                s_u = jnp.sum(k_u * q, axis=-1, keepdims=True)
                s_new.append(s_u)
                mx = jnp.maximum(mx, s_u)
            p_c = jnp.exp(s_c - mx[None])
            l = jnp.sum(p_c, axis=0)
            acc = jnp.sum(p_c * vc, axis=0)
            for u, s_u in zip(new_u, s_new):
                p_u = jnp.exp(s_u - mx)
                v_u = new_ref[0, u, 2 * nh + h0:2 * nh + h0 + A_HEADS_PER_GROUP, :]
                l = l + p_u
                acc = acc + p_u * v_u
            outs.append(acc / l)
            lses.append(mx + jnp.log(l))
        mxg = jnp.maximum(jnp.maximum(lses[0], lses[1]), lses[2])
        ws = [jnp.exp(l_g - mxg) for l_g in lses]
        y = (ws[0] * outs[0] + ws[1] * outs[1] + ws[2] * outs[2]) / (ws[0] + ws[1] + ws[2])
        y_ref[0, t] = y.astype(y_ref.dtype)


def attn_sample(new_qkv, caches, batch, t_dec):
    in_specs = [pl.BlockSpec((1, t_dec, new_qkv.shape[2], LANES), lambda b: (b, 0, 0, 0))]
    args = [new_qkv]
    for (win, dil), kv in zip(A_GROUPS, caches):
        assert win == ATT_BLK * dil and kv[0].shape[1] == win
        n_res = min(dil, t_dec)
        for buf in kv:
            in_specs.append(pl.BlockSpec((1, ATT_BLK, n_res, A_HEADS_PER_GROUP, LANES),
                                         lambda b: (b, 0, 0, 0, 0)))
            args.append(buf.reshape(batch, ATT_BLK, dil, A_HEADS_PER_GROUP, LANES))
    return pl.pallas_call(
        functools.partial(_attn_sample_kernel, t_dec=t_dec),
        grid=(batch,),
        in_specs=in_specs,
        out_specs=pl.BlockSpec((1, t_dec, A_HEADS_PER_GROUP, LANES), lambda b: (b, 0, 0, 0)),
        out_shape=jax.ShapeDtypeStruct((batch, t_dec, A_HEADS_PER_GROUP, LANES), BF16),
        compiler_params=_cparams(("parallel",)),
        name="attn_sample",
    )(*args)


ROUTE_EXPERT, ROUTE_RANK, ROUTE_GATE = 0, TOP_K, 2 * TOP_K


def _merge_route_kernel(h_ref, ha_ref, hb_ref, ga_ref, gb_ref, wpa_ref, wpb_ref, wo_ref, g2_ref,
                        wrh_ref, wrl_ref, br_ref, h1_ref, xn_ref, route_ref, cnt_ref, carry_ref):
    i = pl.program_id(0)
    tm = h_ref.shape[0]

    @pl.when(i == 0)
    def _():
        carry_ref[...] = jnp.zeros_like(carry_ref)

    a = jnp.dot(ha_ref[...], wpa_ref[...], preferred_element_type=F32)
    b = jnp.dot(hb_ref[...], wpb_ref[...], preferred_element_type=F32)
    u = _sigmoid(ga_ref[...].astype(F32)) * a + _sigmoid(gb_ref[...].astype(F32)) * b
    h1 = h_ref[...] + jnp.dot(u.astype(BF16), wo_ref[...], preferred_element_type=F32)
    h1_ref[...] = h1
    xn = _rms(h1, g2_ref[...])
    xn_ref[...] = xn
    x_hi = xn.astype(BF16)
    x_lo = (xn - x_hi.astype(F32)).astype(BF16)
    logits = (jnp.dot(x_hi, wrh_ref[...], preferred_element_type=F32)
              + jnp.dot(x_lo, wrh_ref[...], preferred_element_type=F32)
              + jnp.dot(x_hi, wrl_ref[...], preferred_element_type=F32)) + br_ref[...]
    lane = lax.broadcasted_iota(jnp.int32, (tm, LANES), 1)
    logits = jnp.where(lane < N_EXPERTS, logits, NEG_INF)
    vals, hots = [], []
    work = logits
    for _ in range(TOP_K):
        mx = jnp.max(work, axis=1, keepdims=True)
        idx = jnp.min(jnp.where(work == mx, lane, LANES), axis=1, keepdims=True)
        hot = lane == idx
        work = jnp.where(hot, NEG_INF, work)
        vals.append(mx)
        hots.append(hot)
    exps = [jnp.exp(v - vals[0]) for v in vals]
    denom = exps[0] + exps[1] + exps[2] + exps[3]
    hot_all = hots[0] | hots[1] | hots[2] | hots[3]
    hot_f = jnp.where(hot_all, 1.0, 0.0)
    r_id = lax.broadcasted_iota(jnp.int32, (tm, tm), 0)
    c_id = lax.broadcasted_iota(jnp.int32, (tm, tm), 1)
    earlier = jnp.where(c_id < r_id, 1.0, 0.0).astype(BF16)
    base = carry_ref[...] + jnp.dot(earlier, hot_f.astype(BF16), preferred_element_type=F32)
    carry_new = carry_ref[...] + jnp.sum(hot_f, axis=0, keepdims=True)
    carry_ref[...] = carry_new
    lane_f = lane.astype(F32)
    route = jnp.zeros((tm, LANES), F32)
    for k in range(TOP_K):
        e_k = jnp.sum(jnp.where(hots[k], lane_f, 0.0), axis=1, keepdims=True)
        r_k = jnp.sum(jnp.where(hots[k], base, 0.0), axis=1, keepdims=True)
        route = jnp.where(lane == ROUTE_EXPERT + k, e_k, route)
        route = jnp.where(lane == ROUTE_RANK + k, r_k, route)
        route = jnp.where(lane == ROUTE_GATE + k, exps[k] / denom, route)
    route_ref[...] = route

    @pl.when(i == pl.num_programs(0) - 1)
    def _():
        cnt_ref[...] = carry_new


def merge_route(h, ha, hb, zmg, w_pa, w_pb, w_o, norm2, w_r_hi, w_r_lo, b_r, tm):
    n, d = h.shape

    def full(arr):
        return pl.BlockSpec(arr.shape, lambda i: (0,) * arr.ndim)

    weights = [w_pa, w_pb, w_o, norm2.reshape(1, d), w_r_hi, w_r_lo, b_r]
    return pl.pallas_call(
        _merge_route_kernel,
        grid=(n // tm,),
        in_specs=[pl.BlockSpec((tm, d), lambda i: (i, 0)),
                  pl.BlockSpec((tm, M_WIDTH), lambda i: (i, 0)),
                  pl.BlockSpec((tm, A_GROUP_WIDTH), lambda i: (i, 0)),
                  pl.BlockSpec((tm, d), lambda i: (i, 0)),
                  pl.BlockSpec((tm, d), lambda i: (i, 1))] + [full(wt) for wt in weights],
        out_specs=[pl.BlockSpec((tm, d), lambda i: (i, 0)),
                   pl.BlockSpec((tm, d), lambda i: (i, 0)),
                   pl.BlockSpec((tm, LANES), lambda i: (i, 0)),
                   pl.BlockSpec((1, LANES), lambda i: (0, 0))],
        out_shape=[jax.ShapeDtypeStruct((n, d), F32),
                   jax.ShapeDtypeStruct((n, d), F32),
                   jax.ShapeDtypeStruct((n, LANES), F32),
                   jax.ShapeDtypeStruct((1, LANES), F32)],
        scratch_shapes=[pltpu.VMEM((1, LANES), F32)],
        compiler_params=_cparams(("arbitrary",)),
        name="merge_route",
    )(h, ha, hb, zmg, zmg, *weights)


def _swiglu(h):
    x_glu = jnp.minimum(h[:, :D_FF], SWIGLU_LIMIT)
    x_lin = jnp.clip(h[:, D_FF:], -SWIGLU_LIMIT, SWIGLU_LIMIT)
    return x_glu * _sigmoid(SWIGLU_ALPHA * x_glu) * (x_lin + 1.0)


def _row_gather(src_hbm, idx_ref, idx0, dst_ref, n_rows, sem):
    def start(r, carry):
        tok = idx_ref[0, 0, idx0 + r]
        pltpu.make_async_copy(src_hbm.at[pl.ds(tok, 1)], dst_ref.at[pl.ds(r, 1)], sem).start()
        return carry

    lax.fori_loop(0, n_rows, start, 0)
    pltpu.make_async_copy(src_hbm.at[pl.ds(0, n_rows)], dst_ref, sem).wait()


def _moe_kernel(blk_exp_ref, n_used_ref, tok_ref, x_hbm, w1_ref, b1_ref, w2_ref, b2_ref, y_ref,
                xbuf, w1b, w2b, sem):
    i = pl.program_id(0)
    bm = xbuf.shape[0]

    @pl.when(i < n_used_ref[0])
    def _():
        changed = jnp.logical_or(i == 0, blk_exp_ref[i] != blk_exp_ref[jnp.maximum(i - 1, 0)])

        @pl.when(changed)
        def _():
            w1b[...] = w1_ref[0].astype(BF16)
            w2b[...] = w2_ref[0].astype(BF16)

        _row_gather(x_hbm, tok_ref, 0, xbuf, bm, sem)
        h = jnp.dot(xbuf[...].astype(BF16), w1b[...], preferred_element_type=F32) + b1_ref[0]
        act = _swiglu(h)
        y_ref[...] = jnp.dot(act.astype(BF16), w2b[...], preferred_element_type=F32) + b2_ref[0]

    @pl.when(i >= n_used_ref[0])
    def _():
        y_ref[...] = jnp.zeros_like(y_ref)


def moe_experts(xn, slot_tok, blk_exp, n_used, w1, b1, w2, b2, bm):
    n, d = xn.shape
    nblk = blk_exp.shape[0]
    ne = w1.shape[0]
    grid_spec = pltpu.PrefetchScalarGridSpec(
        num_scalar_prefetch=2,
        grid=(nblk,),
        in_specs=[pl.BlockSpec((1, 1, bm), lambda i, be, nu: (i, 0, 0), memory_space=pltpu.SMEM),
                  pl.BlockSpec(memory_space=pl.ANY),
                  pl.BlockSpec((1, d, 2 * D_FF), lambda i, be, nu: (be[i], 0, 0)),
                  pl.BlockSpec((1, 1, 2 * D_FF), lambda i, be, nu: (be[i], 0, 0)),
                  pl.BlockSpec((1, D_FF, d), lambda i, be, nu: (be[i], 0, 0)),
                  pl.BlockSpec((1, 1, d), lambda i, be, nu: (be[i], 0, 0))],
        out_specs=pl.BlockSpec((bm, d), lambda i, be, nu: (i, 0)),
        scratch_shapes=[pltpu.VMEM((bm, d), F32), pltpu.VMEM((d, 2 * D_FF), BF16),
                        pltpu.VMEM((D_FF, d), BF16), pltpu.SemaphoreType.DMA(())],
    )
    return pl.pallas_call(
        _moe_kernel,
        grid_spec=grid_spec,
        out_shape=jax.ShapeDtypeStruct((nblk * bm, d), F32),
        compiler_params=_cparams(("arbitrary",)),
        name="moe_experts",
    )(blk_exp, n_used, slot_tok.reshape(nblk, 1, bm), xn, w1, b1.reshape(ne, 1, -1), w2, b2.reshape(ne, 1, -1))


def _combine_kernel(dest_ref, ys_hbm, h1_ref, route_ref, p_ref, wple_ref, wpg_ref, g3_ref, gf_ref, y_ref,
                    ybuf, sem):
    tm = h1_ref.shape[0]
    _row_gather(ys_hbm, dest_ref, 0, ybuf, TOP_K * tm, sem)
    route = route_ref[...]
    h2 = h1_ref[...]
    for k in range(TOP_K):
        h2 = h2 + route[:, ROUTE_GATE + k:ROUTE_GATE + k + 1] * ybuf[k * tm:(k + 1) * tm, :]
    ple = jnp.dot(p_ref[...].astype(BF16), wple_ref[...], preferred_element_type=F32)
    gate = _sigmoid(jnp.dot(_rms(h2, g3_ref[...]).astype(BF16), wpg_ref[...], preferred_element_type=F32))
    h3 = h2 + ple * gate
    y_ref[...] = _rms(h3, gf_ref[...])


def moe_combine(ys, dest, h1, route, p, w_ple, w_pg, norm3, norm_f, tm):
    n, d = h1.shape

    def full(arr):
        return pl.BlockSpec(arr.shape, lambda i: (0,) * arr.ndim)

    weights = [w_ple, w_pg, norm3.reshape(1, d), norm_f.reshape(1, d)]
    return pl.pallas_call(
        _combine_kernel,
        grid=(n // tm,),
        in_specs=[pl.BlockSpec((1, 1, TOP_K * tm), lambda i: (i, 0, 0), memory_space=pltpu.SMEM),
                  pl.BlockSpec(memory_space=pl.ANY),
                  pl.BlockSpec((tm, d), lambda i: (i, 0)),
                  pl.BlockSpec((tm, LANES), lambda i: (i, 0)),
                  pl.BlockSpec((tm, PLE_DIM), lambda i: (i, 0))] + [full(wt) for wt in weights],
        out_specs=pl.BlockSpec((tm, d), lambda i: (i, 0)),
        out_shape=jax.ShapeDtypeStruct((n, d), F32),
        scratch_shapes=[pltpu.VMEM((TOP_K * tm, d), F32), pltpu.SemaphoreType.DMA(())],
        compiler_params=_cparams(("arbitrary",)),
        name="moe_combine",
    )(dest, ys, h1, route, p, *weights)


def kernel(x_prompt, x_sample, state_C, state_n, state_m, cache_k0, cache_v0, cache_k1, cache_v1, cache_k2, cache_v2, p_prompt, p_sample, norm1, w_in, b_igate, b_fgate, m_norm, w_pa, w_pb, w_o, norm2, w_router, b_router, w1, b1, w2, b2, norm3, w_ple, w_ple_gate, norm_f):
    bp, seq, d = x_prompt.shape
    bs, t_dec, _ = x_sample.shape
    n_p, n_s = bp * seq, bs * t_dec
    n_all = n_p + n_s
    x_all = jnp.concatenate([x_prompt.reshape(n_p, d), x_sample.reshape(n_s, d)], axis=0)
    w = w_in[0]
    c_gate = 4 * M_WIDTH
    c_att = c_gate + 2 * M_HEADS
    c_mg = c_att + 3 * A_WIDTH
    w_m = w[:, :c_gate].astype(BF16)
    w_gate = jnp.pad(w[:, c_gate:c_att], ((0, 0), (0, LANES - 2 * M_HEADS))).astype(BF16)
    w_att = w[:, c_att:c_mg].astype(BF16)
    w_mg = w[:, c_mg:].astype(BF16)
    tm = 512
    pos = jnp.concatenate([jnp.tile(jnp.arange(seq, dtype=F32), bp),
                           jnp.tile(jnp.arange(t_dec, dtype=F32) + float(PAST_LEN), bs)])
    rope = rope_tables(pos)
    zm = norm_proj(x_all, norm1[0], w_m, BF16, tm, 1024)
    zgate = norm_proj(x_all, norm1[0], w_gate, F32, tm, LANES)
    zatt = norm_proj(x_all, norm1[0], w_att, F32, tm, A_GROUP_WIDTH, rope=rope,
                     n_rope_blocks=2 * len(A_GROUPS))
    zmg = norm_proj(x_all, norm1[0], w_mg, BF16, tm, 1024)

    bias = jnp.stack([b_igate[0], b_fgate[0]])
    chunk = 128
    grow_p, gcol_p = _gate_layouts(zgate[:n_p], bp, seq, chunk)
    ha_p, c_p, nn_p, m_p = mlstm(zm, grow_p, gcol_p, bias, m_norm[0], bp, seq, chunk, chunk, 0)
    t_pad = 8
    zm_s = jnp.pad(zm[n_p:].reshape(bs, t_dec, -1), ((0, 0), (0, t_pad - t_dec), (0, 0))).reshape(bs * t_pad, -1)
    zg_s = jnp.pad(zgate[n_p:].reshape(bs, t_dec, -1), ((0, 0), (0, t_pad - t_dec), (0, 0))).reshape(bs * t_pad, -1)
    grow_s, gcol_s = _gate_layouts(zg_s, bs, t_pad, t_pad)
    ha_s, c_s, nn_s, m_s = mlstm(zm_s, grow_s, gcol_s, bias, m_norm[0], bs, t_pad, t_pad, t_dec, 0,
                                 state=(state_C[0], state_n[0], state_m[0]))
    hb_p = attn_prompt(zatt, bp, seq)
    caches = ((cache_k0[0], cache_v0[0]), (cache_k1[0], cache_v1[0]), (cache_k2[0], cache_v2[0]))
    z_new = zatt[n_p:]
    hb_s = attn_sample(z_new.reshape(bs, t_dec, 3 * A_WIDTH // LANES, LANES), caches, bs, t_dec)
    ha_all = jnp.concatenate([ha_p, ha_s.reshape(bs, t_pad, M_WIDTH)[:, :t_dec].reshape(n_s, M_WIDTH)], axis=0)
    hb_all = jnp.concatenate([hb_p, hb_s.reshape(n_s, A_GROUP_WIDTH)], axis=0)

    w_r = jnp.pad(w_router[0], ((0, 0), (0, LANES - N_EXPERTS)))
    w_r_hi = w_r.astype(BF16)
    w_r_lo = (w_r - w_r_hi.astype(F32)).astype(BF16)
    b_r = jnp.pad(b_router[0], (0, LANES - N_EXPERTS)).reshape(1, LANES)
    tm2 = 256
    h1, xn2, route, cnt = merge_route(x_all, ha_all, hb_all, zmg, w_pa[0].astype(BF16), w_pb[0].astype(BF16),
                                      w_o[0].astype(BF16), norm2[0], w_r_hi, w_r_lo, b_r, tm2)
    bm = 256
    n_assign = n_all * TOP_K
    nblk = n_assign // bm + N_EXPERTS
    counts = cnt[0, :N_EXPERTS].astype(jnp.int32)
    padded = ((counts + bm - 1) // bm) * bm
    pend = jnp.cumsum(padded)
    pstart = pend - padded
    top_e = route[:, ROUTE_EXPERT:ROUTE_EXPERT + TOP_K].astype(jnp.int32)
    rank = route[:, ROUTE_RANK:ROUTE_RANK + TOP_K].astype(jnp.int32)
    dest = pstart[top_e] + rank
    tok = jnp.repeat(jnp.arange(n_all, dtype=jnp.int32), TOP_K)
    slot_tok = jnp.zeros((nblk * bm,), jnp.int32).at[dest.reshape(-1)].set(tok)
    blk_start = jnp.arange(nblk, dtype=jnp.int32) * bm
    blk_exp = jnp.minimum(jnp.searchsorted(pend, blk_start, side='right'), N_EXPERTS - 1).astype(jnp.int32)
    n_used = (pend[-1:] // bm).astype(jnp.int32)
    ys = moe_experts(xn2, slot_tok, blk_exp, n_used, w1[0], b1[0], w2[0], b2[0], bm)
    dest_blk = dest.reshape(n_all // tm2, tm2, TOP_K).transpose(0, 2, 1).reshape(n_all // tm2, 1, TOP_K * tm2)
    p_all = jnp.concatenate([p_prompt[0].reshape(n_p, PLE_DIM), p_sample[0].reshape(n_s, PLE_DIM)], axis=0)
    y_all = moe_combine(ys, dest_blk, h1, route, p_all, w_ple[0].astype(BF16), w_ple_gate[0].astype(BF16),
                        norm3[0], norm_f, tm2)

    y_prompt = y_all[:n_p].reshape(bp, seq, d)
    y_sample = y_all[n_p:].reshape(bs, t_dec, d)
    outs_p = [c_p[None], nn_p[None], m_p[None]]
    outs_s = [c_s[None], nn_s[None], m_s[None]]
    for g, (win, _) in enumerate(A_GROUPS):
        keep = min(win, seq)
        for part, buf in ((1, caches[g][0]), (2, caches[g][1])):
            col = part * A_WIDTH + g * A_GROUP_WIDTH
            full = zatt[:n_p, col:col + A_GROUP_WIDTH].reshape(bp, seq, A_HEADS_PER_GROUP, A_HEAD_DIM)
            outs_p.append(full[:, seq - keep:][None])
            new = z_new[:, col:col + A_GROUP_WIDTH].reshape(bs, t_dec, A_HEADS_PER_GROUP, A_HEAD_DIM)
            outs_s.append(jnp.concatenate([buf[:, t_dec:], new], axis=1)[None])
    return (y_prompt, y_sample, *outs_p, *outs_s)


def _gate_layouts(zg, batch, seq, chunk):
    nc = seq // chunk
    g = zg[:, :2 * M_HEADS].reshape(batch, nc, chunk, 2, M_HEADS)
    g = jnp.transpose(g, (3, 0, 4, 1, 2)).reshape(2, batch * M_HEADS, nc, chunk)
    return (g[0][:, :, None, :], g[1][:, :, None, :]), g[1][:, :, :, None]
```

```python
import functools

import jax
import jax.numpy as jnp
import numpy as np
from jax import lax
from jax.experimental import pallas as pl
from jax.experimental.pallas import tpu as pltpu

F32 = jnp.float32
BF16 = jnp.bfloat16

D_MODEL = 1024
PAST_LEN = 8192
M_HEADS = 4
M_HEAD_DIM = 256
M_WIDTH = M_HEADS * M_HEAD_DIM
A_GROUPS = ((128, 1), (512, 4), (2048, 16))
A_HEADS_PER_GROUP = 4
A_HEAD_DIM = 128
A_GROUP_WIDTH = A_HEADS_PER_GROUP * A_HEAD_DIM
A_WIDTH = len(A_GROUPS) * A_GROUP_WIDTH
ROPE_THETA = 500000.0
ROPE_DIM = A_HEAD_DIM // 4
N_EXPERTS = 32
TOP_K = 4
D_FF = D_MODEL
SWIGLU_ALPHA = 1.702
SWIGLU_LIMIT = 7.0
PLE_DIM = 256
EPS = 1e-6

LANES = 128
VMEM_LIMIT = 56 * 1024 * 1024
NEG_INF = float("-inf")
PROJ_TM = 512


def _cparams(sem):
    return pltpu.CompilerParams(dimension_semantics=sem, vmem_limit_bytes=VMEM_LIMIT)


def _rms(x, gain):
    return x * lax.rsqrt(jnp.mean(x * x, axis=-1, keepdims=True) + EPS) * gain


def _sigmoid(x):
    return 1.0 / (1.0 + jnp.exp(-x))


def _log_sigmoid(x):
    return jnp.minimum(x, 0.0) - jnp.log(1.0 + jnp.exp(-jnp.abs(x)))


SHIFT_SPLIT = 8


def _shift_descriptors(cache_ref, new_ref, out_ref, sem):
    _, batch, wb = cache_ref.shape[:3]
    t_dec = new_ref.shape[1]
    step = batch // SHIFT_SPLIT
    descs = []
    for s in range(SHIFT_SPLIT):
        rows = pl.ds(s * step, step)
        descs.append(pltpu.make_async_copy(cache_ref.at[0, rows, pl.ds(t_dec, wb - t_dec)],
                                           out_ref.at[0, rows, pl.ds(0, wb - t_dec)], sem))
    descs.append(pltpu.make_async_copy(new_ref, out_ref.at[0, :, pl.ds(wb - t_dec, t_dec)], sem))
    return descs


def _host_shift(body, grid_rank, n_prefetch, n_in, n_out, n_shift):
    if n_shift == 0:
        return body

    def wrapped(*refs):
        a = n_prefetch + n_in
        b = a + 2 * n_shift
        c = b + n_out
        d = c + n_shift
        main = refs[:a] + refs[b:c] + refs[d:-1]
        sems = refs[-1]
        first = pl.program_id(0) == 0
        last = pl.program_id(0) == pl.num_programs(0) - 1
        for ax in range(1, grid_rank):
            first = jnp.logical_and(first, pl.program_id(ax) == 0)
            last = jnp.logical_and(last, pl.program_id(ax) == pl.num_programs(ax) - 1)
        descs = []
        for s in range(n_shift):
            descs += _shift_descriptors(refs[a + 2 * s], refs[a + 2 * s + 1], refs[c + s], sems.at[s])

        @pl.when(first)
        def _():
            for dsc in descs:
                dsc.start()

        body(*main)

        @pl.when(last)
        def _():
            for dsc in descs:
                dsc.wait()

    return wrapped


def _shift_io(shifts):
    any_spec = pl.BlockSpec(memory_space=pl.ANY)
    args, out_shape = [], []
    for cache, new in shifts:
        args += [cache, new]
        out_shape.append(jax.ShapeDtypeStruct(cache.shape, cache.dtype))
    n = len(shifts)
    scratch = [pltpu.SemaphoreType.DMA((n,))] if n else []
    return args, [any_spec] * (2 * n), [any_spec] * n, out_shape, scratch


def _norm_proj_kernel(x_ref, g_ref, w_ref, *rest, n_rope_blocks, heads_per_block, has_extra, col_axis):
    rest = list(rest)
    if n_rope_blocks:
        cos_ref, sinm_ref, sinp_ref = rest[:3]
        rest = rest[3:]
    if has_extra:
        we_ref, o_ref, oe_ref = rest
    else:
        o_ref, = rest
    j = pl.program_id(col_axis)
    xn = _rms(x_ref[...], g_ref[...]).astype(BF16)
    z = jnp.dot(xn, w_ref[...], preferred_element_type=F32)
    if has_extra:
        oe_ref[...] = jnp.dot(xn, we_ref[...], preferred_element_type=F32)
    if n_rope_blocks:
        @pl.when(j < n_rope_blocks)
        def _():
            cosf, sinm, sinp = cos_ref[...], sinm_ref[...], sinp_ref[...]
            for h in range(heads_per_block):
                zh = z[:, h * LANES:(h + 1) * LANES]
                rot = (zh * cosf + pltpu.roll(zh, LANES - ROPE_DIM // 2, 1) * sinm
                       + pltpu.roll(zh, ROPE_DIM // 2, 1) * sinp)
                o_ref[:, h * LANES:(h + 1) * LANES] = rot.astype(o_ref.dtype)

        @pl.when(j >= n_rope_blocks)
        def _():
            o_ref[...] = z.astype(o_ref.dtype)
    else:
        o_ref[...] = z.astype(o_ref.dtype)


def norm_proj(x, gain, w, out_dtype, tn, rope=None, n_rope_blocks=0, rope_block=None, w_extra=None):
    n, d = x.shape
    tm = PROJ_TM
    ncol = w.shape[1]
    rows_outer = w_extra is not None
    grid = (n // tm, ncol // tn) if rows_outer else (ncol // tn, n // tm)

    def ix(fn):
        return (lambda i, j: fn(i, j)) if rows_outer else (lambda j, i: fn(i, j))

    in_specs = [pl.BlockSpec((tm, d), ix(lambda i, j: (i, 0))),
                pl.BlockSpec((1, d), ix(lambda i, j: (0, 0))),
                pl.BlockSpec((d, tn), ix(lambda i, j: (0, j)))]
    args = [x, gain.reshape(1, d), w]
    if n_rope_blocks:
        in_specs += [pl.BlockSpec((tm, LANES), ix(lambda i, j: (rope_block(i), 0)))] * 3
        args += list(rope)
    out_specs = pl.BlockSpec((tm, tn), ix(lambda i, j: (i, j)))
    out_shape = jax.ShapeDtypeStruct((n, ncol), out_dtype)
    if w_extra is not None:
        in_specs.append(pl.BlockSpec((d, LANES), ix(lambda i, j: (0, 0))))
        args.append(w_extra)
        out_specs = [out_specs, pl.BlockSpec((tm, LANES), ix(lambda i, j: (i, 0)))]
        out_shape = [out_shape, jax.ShapeDtypeStruct((n, LANES), F32)]
    return pl.pallas_call(
        functools.partial(_norm_proj_kernel, n_rope_blocks=n_rope_blocks, heads_per_block=tn // LANES,
                          has_extra=w_extra is not None, col_axis=1 if rows_outer else 0),
        grid=grid,
        in_specs=in_specs,
        out_specs=out_specs,
        out_shape=out_shape,
        compiler_params=_cparams(("arbitrary", "arbitrary")),
        name="norm_proj",
    )(*args)


def rope_tables(pos):
    half = ROPE_DIM // 2
    inv = (1.0 / (np.float32(ROPE_THETA) ** (np.arange(0, ROPE_DIM, 2, dtype=np.float32) / ROPE_DIM))).astype(np.float32)
    ang = (pos.astype(np.float32)[:, None] * inv[None, :]).astype(np.float64)
    cos, sin = np.cos(ang), np.sin(ang)
    n = pos.shape[0]
    ones = np.ones((n, LANES - ROPE_DIM))
    zeros = np.zeros((n, LANES - ROPE_DIM))
    zh = np.zeros((n, half))
    cosf = np.concatenate([cos, cos, ones], axis=1).astype(np.float32)
    sinm = np.concatenate([-sin, zh, zeros], axis=1).astype(np.float32)
    sinp = np.concatenate([zh, sin, zeros], axis=1).astype(np.float32)
    return jnp.asarray(cosf), jnp.asarray(sinm), jnp.asarray(sinp)


KV_ROWS = 512


def _kv_out_kernel(zk_ref, zv_ref, *outs, seq):
    s = pl.program_id(1)
    last = s == pl.num_programs(1) - 1
    for g, (win, _) in enumerate(A_GROUPS):
        keep = min(win, seq)
        for z_ref, o_ref in ((zk_ref, outs[2 * g]), (zv_ref, outs[2 * g + 1])):
            def write(z_ref=z_ref, o_ref=o_ref, rows=min(keep, KV_ROWS), g=g):
                for h in range(A_HEADS_PER_GROUP):
                    col = (g * A_HEADS_PER_GROUP + h) * LANES
                    o_ref[0, 0, :, h, :] = z_ref[KV_ROWS - rows:, col:col + LANES]

            if keep >= seq:
                write()
            else:
                assert keep <= KV_ROWS
                pl.when(last)(write)


def kv_out(zatt, batch, seq):
    steps = seq // KV_ROWS
    in_specs = [pl.BlockSpec((KV_ROWS, A_WIDTH), lambda b, s: (b * steps + s, 1)),
                pl.BlockSpec((KV_ROWS, A_WIDTH), lambda b, s: (b * steps + s, 2))]
    out_specs, out_shape = [], []
    for win, _ in A_GROUPS:
        keep = min(win, seq)
        rows = min(keep, KV_ROWS)
        idx = (lambda b, s: (0, b, s, 0, 0)) if keep >= seq else (lambda b, s: (0, b, 0, 0, 0))
        for _ in range(2):
            out_specs.append(pl.BlockSpec((1, 1, rows, A_HEADS_PER_GROUP, A_HEAD_DIM), idx))
            out_shape.append(jax.ShapeDtypeStruct((1, batch, keep, A_HEADS_PER_GROUP, A_HEAD_DIM), F32))
    return pl.pallas_call(
        functools.partial(_kv_out_kernel, seq=seq),
        grid=(batch, steps),
        in_specs=in_specs,
        out_specs=out_specs,
        out_shape=out_shape,
        compiler_params=_cparams(("arbitrary", "arbitrary")),
        name="kv_out",
    )(zatt, zatt)


def _mlstm_kernel(bias_ref, q_ref, k_ref, v_ref, o_ref, gi_ref, gfr_ref, gfc_ref, mn_ref, *rest,
                  chunk, valid_len, has_state):
    if has_state:
        c0_ref, n0_ref, m0_ref, h_ref, c_out, n_out, m_out, c_s, n_s, m_s = rest
    else:
        h_ref, c_out, n_out, m_out, c_s, n_s, m_s = rest
    hd = pl.program_id(1)
    c = pl.program_id(2)
    L = chunk

    @pl.when(c == 0)
    def _():
        if has_state:
            c_s[...] = c0_ref[0, 0]
            n_s[...] = n0_ref[0, 0]
            m_s[...] = m0_ref[0, 0]
        else:
            c_s[...] = jnp.zeros_like(c_s)
            n_s[...] = jnp.zeros_like(n_s)
            m_s[...] = jnp.zeros_like(m_s)

    q = q_ref[...]
    k = k_ref[...] * (M_HEAD_DIM ** -0.5)
    v = v_ref[...]
    b_i = bias_ref[0, hd]
    b_f = bias_ref[1, hd]
    i_row = gi_ref[0, 0] + b_i
    lf_row = _log_sigmoid(gfr_ref[0, 0] + b_f)
    lf_col = _log_sigmoid(gfc_ref[0, 0] + b_f)
    row_id = lax.broadcasted_iota(jnp.int32, (L, L), 0)
    col_id = lax.broadcasted_iota(jnp.int32, (L, L), 1)
    if valid_len < L:
        lane = lax.broadcasted_iota(jnp.int32, (1, L), 1)
        sub = lax.broadcasted_iota(jnp.int32, (L, 1), 0)
        i_row = jnp.where(lane < valid_len, i_row, NEG_INF)
        lf_row = jnp.where(lane < valid_len, lf_row, 0.0)
        lf_col = jnp.where(sub < valid_len, lf_col, 0.0)
    causal = col_id <= row_id
    b_col = jnp.sum(jnp.where(causal, lf_row, 0.0), axis=1, keepdims=True)
    b_row = jnp.sum(jnp.where(row_id <= col_id, lf_col, 0.0), axis=0, keepdims=True)
    m_prev = m_s[...]
    dmat = jnp.where(causal, b_col - b_row + i_row, NEG_INF)
    inter = b_col + m_prev
    mj = jnp.maximum(inter, jnp.max(dmat, axis=1, keepdims=True))
    s = lax.dot_general(q, k, (((1,), (1,)), ((), ())), preferred_element_type=F32)
    sc = s * jnp.exp(dmat - mj)
    a_int = jnp.exp(inter - mj)
    c_prev = c_s[...]
    n_prev = n_s[...]
    qc = lax.dot_general(q, c_prev.astype(BF16), (((1,), (1,)), ((), ())), preferred_element_type=F32)
    num = jnp.dot(sc.astype(BF16), v, preferred_element_type=F32) + a_int * qc
    qn = jnp.sum(q.astype(F32) * n_prev, axis=1, keepdims=True)
    den = jnp.sum(sc, axis=1, keepdims=True) + a_int * qn
    h = num / jnp.maximum(jnp.abs(den), jnp.exp(-mj))
    h = h * _sigmoid(o_ref[...].astype(F32))
    h = h * lax.rsqrt(jnp.mean(h * h, axis=-1, keepdims=True) + EPS) * mn_ref[...]
    h_ref[...] = h.astype(h_ref.dtype)

    bl = jnp.sum(lf_row, axis=1, keepdims=True)
    g_row = bl - b_row + i_row
    m_new = jnp.maximum(bl + m_prev, jnp.max(g_row, axis=1, keepdims=True))
    ws_row = jnp.exp(g_row - m_new)
    a_c = jnp.exp(bl + m_prev - m_new)
    vt = (v.astype(F32).T * ws_row).astype(BF16)
    c_new = a_c * c_prev + jnp.dot(vt, k, preferred_element_type=F32)
    ws8 = jnp.broadcast_to(ws_row, (8, L)).astype(BF16)
    n_new = a_c * n_prev + jnp.dot(ws8, k, preferred_element_type=F32)[0:1]
    c_s[...] = c_new
    n_s[...] = n_new
    m_s[...] = m_new

    @pl.when(c == pl.num_programs(2) - 1)
    def _():
        c_out[0, 0] = c_new
        n_out[0, 0] = n_new
        m_out[0, 0] = m_new


def mlstm(zm, gates_row, gates_col, bias, m_norm, batch, seq, chunk, valid_len, row0, state=None, shifts=()):
    E = M_HEAD_DIM
    nc = seq // chunk
    blk0 = row0 // chunk

    def zspec(col):
        return pl.BlockSpec((chunk, E), lambda b, h, c: (blk0 + b * nc + c, col * M_HEADS + h))

    in_specs = [pl.BlockSpec(memory_space=pltpu.SMEM),
                zspec(0), zspec(1), zspec(2), zspec(3),
                pl.BlockSpec((1, 1, 1, chunk), lambda b, h, c: (b * M_HEADS + h, c, 0, 0)),
                pl.BlockSpec((1, 1, 1, chunk), lambda b, h, c: (b * M_HEADS + h, c, 0, 0)),
                pl.BlockSpec((1, 1, chunk, 1), lambda b, h, c: (b * M_HEADS + h, c, 0, 0)),
                pl.BlockSpec((1, E), lambda b, h, c: (0, h))]
    gi_row, gf_row = gates_row
    args = [bias, zm, zm, zm, zm, gi_row, gf_row, gates_col, m_norm.reshape(1, M_WIDTH)]
    if state is not None:
        c0, n0, m0 = state
        in_specs += [pl.BlockSpec((1, 1, E, E), lambda b, h, c: (b, h, 0, 0)),
                     pl.BlockSpec((1, 1, 1, E), lambda b, h, c: (b, h, 0, 0)),
                     pl.BlockSpec((1, 1, 1, 1), lambda b, h, c: (b, h, 0, 0))]
        args += [c0, n0.reshape(batch, M_HEADS, 1, E), m0.reshape(batch, M_HEADS, 1, 1)]
    out_specs = [pl.BlockSpec((chunk, E), lambda b, h, c: (b * nc + c, h)),
                 pl.BlockSpec((1, 1, E, E), lambda b, h, c: (b, h, 0, 0)),
                 pl.BlockSpec((1, 1, 1, E), lambda b, h, c: (b, h, 0, 0)),
                 pl.BlockSpec((1, 1, 1, 1), lambda b, h, c: (b, h, 0, 0))]
    out_shape = [jax.ShapeDtypeStruct((batch * seq, M_WIDTH), BF16),
                 jax.ShapeDtypeStruct((batch, M_HEADS, E, E), F32),
                 jax.ShapeDtypeStruct((batch, M_HEADS, 1, E), F32),
                 jax.ShapeDtypeStruct((batch, M_HEADS, 1, 1), F32)]
    body = functools.partial(_mlstm_kernel, chunk=chunk, valid_len=valid_len, has_state=state is not None)
    s_args, s_in, s_out, s_shape, s_scratch = _shift_io(shifts)
    res = pl.pallas_call(
        _host_shift(body, 3, 0, len(in_specs), len(out_specs), len(shifts)),
        grid=(batch, M_HEADS, nc),
        in_specs=in_specs + s_in,
        out_specs=out_specs + s_out,
        out_shape=out_shape + s_shape,
        scratch_shapes=[pltpu.VMEM((E, E), F32), pltpu.VMEM((1, E), F32), pltpu.VMEM((1, 1), F32)] + s_scratch,
        compiler_params=_cparams(("arbitrary", "arbitrary", "arbitrary")),
        name="mlstm",
    )(*args, *s_args)
    h, c_f, n_f, m_f = res[:4]
    return h, c_f, n_f.reshape(batch, M_HEADS, E), m_f.reshape(batch, M_HEADS), list(res[4:])


ATT_BLK = 128


def _band_block(qb, kcat, vcat, mask):
    s = lax.dot_general(qb, kcat, (((1,), (1,)), ((), ())), preferred_element_type=F32)
    s = jnp.where(mask, s, NEG_INF)
    mx = jnp.max(s, axis=1, keepdims=True)
    p = jnp.exp(s - mx)
    l = jnp.sum(p, axis=1, keepdims=True)
    o = jnp.dot(p.astype(BF16), vcat, preferred_element_type=F32) / l
    return o, mx + jnp.log(l)


def _attn_prompt_kernel(*refs, seq):
    qkv = refs[:9]
    y_ref = refs[9]
    o_scr = refs[10:13]
    lse_scr = refs[13:16]
    scale = A_HEAD_DIM ** -0.5
    qi = lax.broadcasted_iota(jnp.int32, (ATT_BLK, ATT_BLK), 0)
    ki = lax.broadcasted_iota(jnp.int32, (ATT_BLK, ATT_BLK), 1)
    cur_mask = ki <= qi
    prev_mask = ki >= qi
    band_mask = jnp.concatenate([prev_mask, cur_mask], axis=1)
    for g, (_, dil) in enumerate(A_GROUPS):
        q_ref, k_ref, v_ref = qkv[3 * g:3 * g + 3]
        L = seq // dil
        nb = L // ATT_BLK

        def residue(r, carry, q_ref=q_ref, k_ref=k_ref, v_ref=v_ref, dil=dil, L=L, nb=nb, g=g):
            def rows(first_blk, n_blk):
                if dil == 1:
                    return pl.ds(first_blk * ATT_BLK, n_blk * ATT_BLK)
                return pl.ds(r + first_blk * ATT_BLK * dil, n_blk * ATT_BLK, stride=dil)

            for n in range(nb):
                qb = (q_ref[rows(n, 1), :] * scale).astype(BF16)
                if n == 0:
                    kk, vv, mask = k_ref[rows(0, 1), :], v_ref[rows(0, 1), :], cur_mask
                else:
                    kk, vv, mask = k_ref[rows(n - 1, 2), :], v_ref[rows(n - 1, 2), :], band_mask
                o, lse = _band_block(qb, kk.astype(BF16), vv.astype(BF16), mask)
                o_scr[g][rows(n, 1), :] = o
                lse_scr[g][rows(n, 1), :] = lse
            return carry

        if dil == 1:
            residue(0, 0)
        else:
            lax.fori_loop(0, dil, residue, 0)
    l0, l1, l2 = lse_scr[0][...], lse_scr[1][...], lse_scr[2][...]
    mx = jnp.maximum(jnp.maximum(l0, l1), l2)
    w0, w1, w2 = jnp.exp(l0 - mx), jnp.exp(l1 - mx), jnp.exp(l2 - mx)
    y = (w0 * o_scr[0][...] + w1 * o_scr[1][...] + w2 * o_scr[2][...]) / (w0 + w1 + w2)
    y_ref[...] = y.astype(y_ref.dtype)


def attn_prompt(zatt, batch, seq, shifts=()):
    nh = A_WIDTH // LANES

    def spec(col0):
        return pl.BlockSpec((seq, LANES), lambda b, j: (b, col0 + j))

    in_specs, args = [], []
    for g in range(len(A_GROUPS)):
        for part in range(3):
            in_specs.append(spec(part * nh + g * A_HEADS_PER_GROUP))
            args.append(zatt)
    s_args, s_in, s_out, s_shape, s_scratch = _shift_io(shifts)
    res = pl.pallas_call(
        _host_shift(functools.partial(_attn_prompt_kernel, seq=seq), 2, 0, len(in_specs), 1, len(shifts)),
        grid=(batch, A_HEADS_PER_GROUP),
        in_specs=in_specs + s_in,
        out_specs=[pl.BlockSpec((seq, LANES), lambda b, j: (b, j))] + s_out,
        out_shape=[jax.ShapeDtypeStruct((batch * seq, A_GROUP_WIDTH), BF16)] + s_shape,
        scratch_shapes=[pltpu.VMEM((seq, LANES), F32)] * 3 + [pltpu.VMEM((seq, 1), F32)] * 3 + s_scratch,
        compiler_params=_cparams(("arbitrary", "arbitrary")),
        name="attn_prompt",
    )(*args, *s_args)
    return res[0], list(res[1:])


def _attn_sample_kernel(new_ref, *refs, t_dec):
    caches = refs[:6]
    y_ref = refs[6]
    nh = A_WIDTH // LANES
    scale = A_HEAD_DIM ** -0.5
    jj = lax.broadcasted_iota(jnp.int32, (ATT_BLK, 1, 1), 0)
    for t in range(t_dec):
        outs, lses = [], []
        for g, (_, dil) in enumerate(A_GROUPS):
            kc_ref, vc_ref = caches[2 * g], caches[2 * g + 1]
            h0 = g * A_HEADS_PER_GROUP
            q = new_ref[0, t, h0:h0 + A_HEADS_PER_GROUP, :] * scale
            kc = kc_ref[0, :, t % dil]
            vc = vc_ref[0, :, t % dil]
            s_c = jnp.sum(kc * q[None], axis=-1, keepdims=True)
            if t // dil > 0:
                s_c = jnp.where(jj >= t // dil, s_c, NEG_INF)
            mx = jnp.max(s_c, axis=0)
            new_u = [u for u in range(t + 1) if (t - u) % dil == 0]
            s_new = []
            for u in new_u:
                k_u = new_ref[0, u, nh + h0:nh + h0 + A_HEADS_PER_GROUP, :]
                s_u = jnp.sum(k_u * q, axis=-1, keepdims=True)
                s_new.append(s_u)
                mx = jnp.maximum(mx, s_u)
            p_c = jnp.exp(s_c - mx[None])
            l = jnp.sum(p_c, axis=0)
            acc = jnp.sum(p_c * vc, axis=0)
            for u, s_u in zip(new_u, s_new):
                p_u = jnp.exp(s_u - mx)
                v_u = new_ref[0, u, 2 * nh + h0:2 * nh + h0 + A_HEADS_PER_GROUP, :]
                l = l + p_u
                acc = acc + p_u * v_u
            outs.append(acc / l)
            lses.append(mx + jnp.log(l))
        mxg = jnp.maximum(jnp.maximum(lses[0], lses[1]), lses[2])
        ws = [jnp.exp(l_g - mxg) for l_g in lses]
        y = (ws[0] * outs[0] + ws[1] * outs[1] + ws[2] * outs[2]) / (ws[0] + ws[1] + ws[2])
        y_ref[0, t] = y.astype(y_ref.dtype)


def attn_sample(new_qkv, caches, batch, t_dec):
    in_specs = [pl.BlockSpec((1, t_dec, new_qkv.shape[2], LANES), lambda b: (b, 0, 0, 0))]
    args = [new_qkv]
    for (win, dil), kv in zip(A_GROUPS, caches):
        assert win == ATT_BLK * dil and kv[0].shape[1] == win
        n_res = min(dil, t_dec)
        for buf in kv:
            in_specs.append(pl.BlockSpec((1, ATT_BLK, n_res, A_HEADS_PER_GROUP, LANES),
                                         lambda b: (b, 0, 0, 0, 0)))
            args.append(buf.reshape(batch, ATT_BLK, dil, A_HEADS_PER_GROUP, LANES))
    return pl.pallas_call(
        functools.partial(_attn_sample_kernel, t_dec=t_dec),
        grid=(batch,),
        in_specs=in_specs,
        out_specs=pl.BlockSpec((1, t_dec, A_HEADS_PER_GROUP, LANES), lambda b: (b, 0, 0, 0)),
        out_shape=jax.ShapeDtypeStruct((batch, t_dec, A_HEADS_PER_GROUP, LANES), BF16),
        compiler_params=_cparams(("parallel",)),
        name="attn_sample",
    )(*args)


MOE_TM = 256
MOE_CHUNK = 8
MOE_BM = 256
MOE_TILE_ROWS = TOP_K * MOE_TM + N_EXPERTS * MOE_CHUNK
ROUTE_EXPERT, ROUTE_POS, ROUTE_GATE = 0, TOP_K, 2 * TOP_K


def _merge_route_kernel(h_ref, ha_ref, hb_ref, ga_ref, gb_ref, wpa_ref, wpb_ref, wo_ref, g2_ref,
                        wrh_ref, wrl_ref, br_ref, h1_ref, xt_ref, route_ref, cnt_ref):
    tm = h_ref.shape[0]
    rows = xt_ref.shape[0]
    a = jnp.dot(ha_ref[...], wpa_ref[...], preferred_element_type=F32)
    b = jnp.dot(hb_ref[...], wpb_ref[...], preferred_element_type=F32)
    u = _sigmoid(ga_ref[...].astype(F32)) * a + _sigmoid(gb_ref[...].astype(F32)) * b
    h1 = h_ref[...] + jnp.dot(u.astype(BF16), wo_ref[...], preferred_element_type=F32)
    h1_ref[...] = h1
    xn = _rms(h1, g2_ref[...])
    x_hi = xn.astype(BF16)
    x_lo = (xn - x_hi.astype(F32)).astype(BF16)
    logits = (jnp.dot(x_hi, wrh_ref[...], preferred_element_type=F32)
              + jnp.dot(x_lo, wrh_ref[...], preferred_element_type=F32)
              + jnp.dot(x_hi, wrl_ref[...], preferred_element_type=F32)) + br_ref[...]
    lane = lax.broadcasted_iota(jnp.int32, (tm, LANES), 1)
    logits = jnp.where(lane < N_EXPERTS, logits, NEG_INF)
    vals, hots = [], []
    work = logits
    for _ in range(TOP_K):
        mx = jnp.max(work, axis=1, keepdims=True)
        idx = jnp.min(jnp.where(work == mx, lane, LANES), axis=1, keepdims=True)
        hot = lane == idx
        work = jnp.where(hot, NEG_INF, work)
        vals.append(mx)
        hots.append(hot)
    exps = [jnp.exp(v - vals[0]) for v in vals]
    denom = exps[0] + exps[1] + exps[2] + exps[3]
    hot_f = jnp.zeros((tm, LANES), F32)
    for hot in hots:
        hot_f = hot_f + jnp.where(hot, 1.0, 0.0)
    r_id = lax.broadcasted_iota(jnp.int32, (tm, tm), 0)
    c_id = lax.broadcasted_iota(jnp.int32, (tm, tm), 1)
    earlier = jnp.where(c_id < r_id, 1.0, 0.0).astype(BF16)
    rank = jnp.dot(earlier, hot_f.astype(BF16), preferred_element_type=F32)
    cnt = jnp.sum(hot_f, axis=0, keepdims=True)
    padded = jnp.floor((cnt + (MOE_CHUNK - 1)) * (1.0 / MOE_CHUNK)) * MOE_CHUNK
    la = lax.broadcasted_iota(jnp.int32, (LANES, LANES), 0)
    lb = lax.broadcasted_iota(jnp.int32, (LANES, LANES), 1)
    before = jnp.where(la < lb, 1.0, 0.0).astype(BF16)
    run_start = jnp.dot(jnp.broadcast_to(padded, (8, LANES)).astype(BF16), before,
                        preferred_element_type=F32)[0:1]
    pos_all = run_start + rank
    lane_f = lane.astype(F32)
    row_id = lax.broadcasted_iota(jnp.int32, (tm, rows), 1)
    place = jnp.zeros((tm, rows), F32)
    route = jnp.zeros((tm, LANES), F32)
    for k in range(TOP_K):
        e_k = jnp.sum(jnp.where(hots[k], lane_f, 0.0), axis=1, keepdims=True)
        p_k = jnp.sum(jnp.where(hots[k], pos_all, 0.0), axis=1, keepdims=True)
        place = place + jnp.where(row_id == p_k.astype(jnp.int32), 1.0, 0.0)
        route = jnp.where(lane == ROUTE_EXPERT + k, e_k, route)
        route = jnp.where(lane == ROUTE_POS + k, p_k, route)
        route = jnp.where(lane == ROUTE_GATE + k, exps[k] / denom, route)
    route_ref[...] = route
    cnt_ref[0] = cnt
    xt_ref[...] = lax.dot_general(place.astype(BF16), x_hi, (((0,), (0,)), ((), ())),
                                  preferred_element_type=F32)


def merge_route(h, ha, hb, zmg, w_pa, w_pb, w_o, norm2, w_r_hi, w_r_lo, b_r):
    n, d = h.shape
    tm = MOE_TM
    nt = n // tm

    def full(arr):
        return pl.BlockSpec(arr.shape, lambda i: (0,) * arr.ndim)

    weights = [w_pa, w_pb, w_o, norm2.reshape(1, d), w_r_hi, w_r_lo, b_r]
    return pl.pallas_call(
        _merge_route_kernel,
        grid=(nt,),
        in_specs=[pl.BlockSpec((tm, d), lambda i: (i, 0)),
                  pl.BlockSpec((tm, M_WIDTH), lambda i: (i, 0)),
                  pl.BlockSpec((tm, A_GROUP_WIDTH), lambda i: (i, 0)),
                  pl.BlockSpec((tm, d), lambda i: (i, 0)),
                  pl.BlockSpec((tm, d), lambda i: (i, 1))] + [full(wt) for wt in weights],
        out_specs=[pl.BlockSpec((tm, d), lambda i: (i, 0)),
                   pl.BlockSpec((MOE_TILE_ROWS, d), lambda i: (i, 0)),
                   pl.BlockSpec((tm, LANES), lambda i: (i, 0)),
                   pl.BlockSpec((1, 1, LANES), lambda i: (i, 0, 0))],
        out_shape=[jax.ShapeDtypeStruct((n, d), F32),
                   jax.ShapeDtypeStruct((nt * MOE_TILE_ROWS, d), F32),
                   jax.ShapeDtypeStruct((n, LANES), F32),
                   jax.ShapeDtypeStruct((nt, 1, LANES), F32)],
        compiler_params=_cparams(("parallel",)),
        name="merge_route",
    )(h, ha, hb, zmg, zmg, *weights)


def moe_plan(cnt, n_blocks):
    nt = cnt.shape[0]
    cpb = MOE_BM // MOE_CHUNK
    cpt = MOE_TILE_ROWS // MOE_CHUNK
    i32 = jnp.int32
    cnt = cnt[:, 0, :N_EXPERTS].astype(i32)
    nch = (cnt + MOE_CHUNK - 1) // MOE_CHUNK
    off_incl = jnp.cumsum(nch, axis=1)
    off_ch = off_incl - nch
    seq_incl = jnp.cumsum(nch, axis=0)
    base_ch = seq_incl - nch
    tot = seq_incl[-1]
    nb = (tot + cpb - 1) // cpb
    bend = jnp.cumsum(nb)
    bstart = bend - nb
    n_used = bend[-1:]
    blk = jnp.arange(n_blocks, dtype=i32)
    blk_exp = jnp.minimum(jnp.sum(blk[:, None] >= bend[None, :], axis=1), N_EXPERTS - 1).astype(i32)
    q = (blk - bstart[blk_exp])[:, None] * cpb + jnp.arange(cpb, dtype=i32)[None, :]
    seq_e = seq_incl.T[blk_exp]
    tile = jnp.minimum(jnp.sum(q[:, :, None] >= seq_e[:, None, :], axis=2), nt - 1).astype(i32)
    e_b = jnp.broadcast_to(blk_exp[:, None], tile.shape)
    src = (tile * cpt + off_ch[tile, e_b] + q - base_ch[tile, e_b]) * MOE_CHUNK
    src = jnp.where(q < tot[blk_exp][:, None], src, 0).astype(i32)
    slot = jnp.arange(cpt, dtype=i32)
    e_s = jnp.sum(slot[None, :, None] >= off_incl[:, None, :], axis=2).astype(i32)
    e_c = jnp.minimum(e_s, N_EXPERTS - 1)
    c = slot[None, :] - jnp.take_along_axis(off_ch, e_c, axis=1)
    back = (bstart[e_c] * cpb + jnp.take_along_axis(base_ch, e_c, axis=1) + c) * MOE_CHUNK
    back = jnp.where(e_s < N_EXPERTS, back, 0).astype(i32)
    return blk_exp, n_used.astype(i32), src.reshape(-1), back.reshape(-1)


def _swiglu(h):
    x_glu = jnp.minimum(h[:, :D_FF], SWIGLU_LIMIT)
    x_lin = jnp.clip(h[:, D_FF:], -SWIGLU_LIMIT, SWIGLU_LIMIT)
    return x_glu * _sigmoid(SWIGLU_ALPHA * x_glu) * (x_lin + 1.0)


def _chunk_gather_start(src_hbm, rows_ref, first, dst_ref, sem):
    def start(c, carry):
        r = pl.multiple_of(rows_ref[first + c], MOE_CHUNK)
        d0 = pl.multiple_of(c * MOE_CHUNK, MOE_CHUNK)
        pltpu.make_async_copy(src_hbm.at[pl.ds(r, MOE_CHUNK)], dst_ref.at[pl.ds(d0, MOE_CHUNK)], sem).start()
        return carry

    lax.fori_loop(0, dst_ref.shape[0] // MOE_CHUNK, start, 0)


def _chunk_gather_wait(src_hbm, dst_ref, sem):
    pltpu.make_async_copy(src_hbm.at[pl.ds(0, dst_ref.shape[0])], dst_ref, sem).wait()


def _moe_kernel(blk_exp_ref, n_used_ref, rows_ref, xt_hbm, w1_ref, b1_ref, w2_ref, b2_ref, y_ref,
                xbuf, w1b, w2b, sems):
    i = pl.program_id(0)
    n_used = n_used_ref[0]
    cpb = xbuf.shape[1] // MOE_CHUNK

    def fetch(blk, slot):
        _chunk_gather_start(xt_hbm, rows_ref, blk * cpb, xbuf.at[slot], sems.at[slot])

    @pl.when(jnp.logical_and(i == 0, n_used > 0))
    def _():
        fetch(0, 0)

    @pl.when(i + 1 < n_used)
    def _():
        fetch(i + 1, (i + 1) % 2)

    @pl.when(i < n_used)
    def _():
        slot = i % 2
        _chunk_gather_wait(xt_hbm, xbuf.at[slot], sems.at[slot])
        changed = jnp.logical_or(i == 0, blk_exp_ref[i] != blk_exp_ref[jnp.maximum(i - 1, 0)])

        @pl.when(changed)
        def _():
            w1b[...] = w1_ref[0].astype(BF16)
            w2b[...] = w2_ref[0].astype(BF16)

        h = jnp.dot(xbuf[slot].astype(BF16), w1b[...], preferred_element_type=F32) + b1_ref[0]
        act = _swiglu(h)
        y_ref[...] = jnp.dot(act.astype(BF16), w2b[...], preferred_element_type=F32) + b2_ref[0]

    @pl.when(i >= n_used)
    def _():
        y_ref[...] = jnp.zeros_like(y_ref)


def moe_experts(xt, blk_exp, n_used, chunk_rows, w1, b1, w2, b2, shifts=()):
    d = xt.shape[1]
    nblk = blk_exp.shape[0]
    ne = w1.shape[0]
    bm = MOE_BM
    s_args, s_in, s_out, s_shape, s_scratch = _shift_io(shifts)
    in_specs = [pl.BlockSpec(memory_space=pl.ANY),
                pl.BlockSpec((1, d, 2 * D_FF), lambda i, be, nu, cr: (be[i], 0, 0)),
                pl.BlockSpec((1, 1, 2 * D_FF), lambda i, be, nu, cr: (be[i], 0, 0)),
                pl.BlockSpec((1, D_FF, d), lambda i, be, nu, cr: (be[i], 0, 0)),
                pl.BlockSpec((1, 1, d), lambda i, be, nu, cr: (be[i], 0, 0))]
    grid_spec = pltpu.PrefetchScalarGridSpec(
        num_scalar_prefetch=3,
        grid=(nblk,),
        in_specs=in_specs + s_in,
        out_specs=[pl.BlockSpec((bm, d), lambda i, be, nu, cr: (i, 0))] + s_out,
        scratch_shapes=[pltpu.VMEM((2, bm, d), F32), pltpu.VMEM((d, 2 * D_FF), BF16),
                        pltpu.VMEM((D_FF, d), BF16), pltpu.SemaphoreType.DMA((2,))] + s_scratch,
    )
    res = pl.pallas_call(
        _host_shift(_moe_kernel, 1, 3, len(in_specs), 1, len(shifts)),
        grid_spec=grid_spec,
        out_shape=[jax.ShapeDtypeStruct((nblk * bm, d), F32)] + s_shape,
        compiler_params=_cparams(("arbitrary",)),
        name="moe_experts",
    )(blk_exp, n_used, chunk_rows, xt, w1, b1.reshape(ne, 1, -1), w2, b2.reshape(ne, 1, -1), *s_args)
    return res[0], list(res[1:])


def _combine_kernel(rows_ref, ys_hbm, h1_ref, route_ref, p_ref, wple_ref, wpg_ref, g3_ref, gf_ref, y_ref,
                    ybuf, sems, *, tile0):
    i = pl.program_id(0)
    n = pl.num_programs(0)
    tm = h1_ref.shape[0]
    rows = ybuf.shape[1]
    cpt = rows // MOE_CHUNK

    def fetch(step, slot):
        _chunk_gather_start(ys_hbm, rows_ref, (tile0 + step) * cpt, ybuf.at[slot], sems.at[slot])

    @pl.when(i == 0)
    def _():
        fetch(0, 0)

    @pl.when(i + 1 < n)
    def _():
        fetch(i + 1, (i + 1) % 2)

    slot = i % 2
    _chunk_gather_wait(ys_hbm, ybuf.at[slot], sems.at[slot])
    route = route_ref[...]
    row_id = lax.broadcasted_iota(jnp.int32, (tm, rows), 1)
    weight = jnp.zeros((tm, rows), F32)
    for k in range(TOP_K):
        p_k = route[:, ROUTE_POS + k:ROUTE_POS + k + 1].astype(jnp.int32)
        weight = weight + jnp.where(row_id == p_k, route[:, ROUTE_GATE + k:ROUTE_GATE + k + 1], 0.0)
    h2 = h1_ref[...] + jnp.dot(weight.astype(BF16), ybuf[slot].astype(BF16), preferred_element_type=F32)
    ple = jnp.dot(p_ref[...].astype(BF16), wple_ref[...], preferred_element_type=F32)
    gate = _sigmoid(jnp.dot(_rms(h2, g3_ref[...]).astype(BF16), wpg_ref[...], preferred_element_type=F32))
    h3 = h2 + ple * gate
    y_ref[...] = _rms(h3, gf_ref[...])


def moe_combine(ys, back_rows, h1, route, p, w_ple, w_pg, norm3, norm_f, tile0, n_tiles):
    d = h1.shape[1]
    tm = MOE_TM

    def full(arr):
        return pl.BlockSpec(arr.shape, lambda i, br: (0,) * arr.ndim)

    weights = [w_ple, w_pg, norm3.reshape(1, d), norm_f.reshape(1, d)]
    grid_spec = pltpu.PrefetchScalarGridSpec(
        num_scalar_prefetch=1,
        grid=(n_tiles,),
        in_specs=[pl.BlockSpec(memory_space=pl.ANY),
                  pl.BlockSpec((tm, d), lambda i, br: (tile0 + i, 0)),
                  pl.BlockSpec((tm, LANES), lambda i, br: (tile0 + i, 0)),
                  pl.BlockSpec((tm, PLE_DIM), lambda i, br: (i, 0))] + [full(wt) for wt in weights],
        out_specs=pl.BlockSpec((tm, d), lambda i, br: (i, 0)),
        scratch_shapes=[pltpu.VMEM((2, MOE_TILE_ROWS, d), F32), pltpu.SemaphoreType.DMA((2,))],
    )
    return pl.pallas_call(
        functools.partial(_combine_kernel, tile0=tile0),
        grid_spec=grid_spec,
        out_shape=jax.ShapeDtypeStruct((n_tiles * tm, d), F32),
        compiler_params=_cparams(("arbitrary",)),
        name="moe_combine",
    )(back_rows, ys, h1, route, p, *weights)


def _gate_layouts(zg, batch, seq, chunk):
    nc = seq // chunk
    g = zg[:, :2 * M_HEADS].reshape(batch, nc, chunk, 2, M_HEADS)
    g = jnp.transpose(g, (3, 0, 4, 1, 2)).reshape(2, batch * M_HEADS, nc, chunk)
    return (g[0][:, :, None, :], g[1][:, :, None, :]), g[1][:, :, :, None]


def kernel(x_prompt, x_sample, state_C, state_n, state_m, cache_k0, cache_v0, cache_k1, cache_v1, cache_k2, cache_v2, p_prompt, p_sample, norm1, w_in, b_igate, b_fgate, m_norm, w_pa, w_pb, w_o, norm2, w_router, b_router, w1, b1, w2, b2, norm3, w_ple, w_ple_gate, norm_f):
    bp, seq, d = x_prompt.shape
    bs, t_dec, _ = x_sample.shape
    n_p, n_s = bp * seq, bs * t_dec
    n_all = n_p + n_s
    x_all = jnp.concatenate([x_prompt.reshape(n_p, d), x_sample.reshape(n_s, d)], axis=0)

    w = w_in[0]
    c_gate = 4 * M_WIDTH
    c_att = c_gate + 2 * M_HEADS
    c_mg = c_att + 3 * A_WIDTH
    w_m = w[:, :c_gate].astype(BF16)
    w_gate = jnp.pad(w[:, c_gate:c_att], ((0, 0), (0, LANES - 2 * M_HEADS))).astype(BF16)
    w_att = w[:, c_att:c_mg].astype(BF16)
    w_mg = w[:, c_mg:].astype(BF16)
    rope = rope_tables(np.concatenate([np.arange(seq), np.tile(np.arange(t_dec) + PAST_LEN, bs)]))
    tiles_p, tiles_seq = n_p // PROJ_TM, seq // PROJ_TM

    def rope_block(i):
        return jnp.where(i < tiles_p, i % tiles_seq, tiles_seq + i - tiles_p)

    zm, zgate = norm_proj(x_all, norm1[0], w_m, BF16, 2048, w_extra=w_gate)
    zatt = norm_proj(x_all, norm1[0], w_att, F32, A_WIDTH, rope=rope, n_rope_blocks=2, rope_block=rope_block)
    zmg = norm_proj(x_all, norm1[0], w_mg, BF16, 2048)

    kv_p = kv_out(zatt, bp, seq)
    z_new = zatt[n_p:]
    caches = ((cache_k0, cache_v0), (cache_k1, cache_v1), (cache_k2, cache_v2))

    def shift(g, part):
        col = (1 + part) * A_WIDTH + g * A_GROUP_WIDTH
        new = z_new[:, col:col + A_GROUP_WIDTH].reshape(bs, t_dec, A_HEADS_PER_GROUP, A_HEAD_DIM)
        return caches[g][part], new

    bias = jnp.stack([b_igate[0], b_fgate[0]])
    chunk = 128
    grow_p, gcol_p = _gate_layouts(zgate[:n_p], bp, seq, chunk)
    ha_p, c_p, nn_p, m_p, (k2_s,) = mlstm(zm, grow_p, gcol_p, bias, m_norm[0], bp, seq, chunk, chunk, 0,
                                          shifts=[shift(2, 0)])
    t_pad = 8
    zm_s = jnp.pad(zm[n_p:].reshape(bs, t_dec, -1), ((0, 0), (0, t_pad - t_dec), (0, 0))).reshape(bs * t_pad, -1)
    zg_s = jnp.pad(zgate[n_p:].reshape(bs, t_dec, -1), ((0, 0), (0, t_pad - t_dec), (0, 0))).reshape(bs * t_pad, -1)
    grow_s, gcol_s = _gate_layouts(zg_s, bs, t_pad, t_pad)
    ha_s, c_s, nn_s, m_s, _ = mlstm(zm_s, grow_s, gcol_s, bias, m_norm[0], bs, t_pad, t_pad, t_dec, 0,
                                    state=(state_C[0], state_n[0], state_m[0]))
    hb_p, (v2_s,) = attn_prompt(zatt, bp, seq, shifts=[shift(2, 1)])
    hb_s = attn_sample(z_new.reshape(bs, t_dec, 3 * A_WIDTH // LANES, LANES),
                       tuple((ck[0], cv[0]) for ck, cv in caches), bs, t_dec)
    ha_all = jnp.concatenate([ha_p, ha_s.reshape(bs, t_pad, M_WIDTH)[:, :t_dec].reshape(n_s, M_WIDTH)], axis=0)
    hb_all = jnp.concatenate([hb_p, hb_s.reshape(n_s, A_GROUP_WIDTH)], axis=0)

    w_r = jnp.pad(w_router[0], ((0, 0), (0, LANES - N_EXPERTS)))
    w_r_hi = w_r.astype(BF16)
    w_r_lo = (w_r - w_r_hi.astype(F32)).astype(BF16)
    b_r = jnp.pad(b_router[0], (0, LANES - N_EXPERTS)).reshape(1, LANES)
    h1, xt, route, cnt = merge_route(x_all, ha_all, hb_all, zmg, w_pa[0].astype(BF16), w_pb[0].astype(BF16),
                                     w_o[0].astype(BF16), norm2[0], w_r_hi, w_r_lo, b_r)
    nt = n_all // MOE_TM
    n_blocks = nt * MOE_TILE_ROWS // MOE_BM + N_EXPERTS + 1
    blk_exp, n_used, src_rows, back_rows = moe_plan(cnt, n_blocks)
    ys, (k0_s, v0_s, k1_s, v1_s) = moe_experts(xt, blk_exp, n_used, src_rows, w1[0], b1[0], w2[0], b2[0],
                                               shifts=[shift(0, 0), shift(0, 1), shift(1, 0), shift(1, 1)])
    tail_w = (w_ple[0].astype(BF16), w_ple_gate[0].astype(BF16), norm3[0], norm_f)
    nt_p = n_p // MOE_TM
    y_p = moe_combine(ys, back_rows, h1, route, p_prompt[0].reshape(n_p, PLE_DIM), *tail_w, 0, nt_p)
    y_s = moe_combine(ys, back_rows, h1, route, p_sample[0].reshape(n_s, PLE_DIM), *tail_w, nt_p, nt - nt_p)

    return (y_p.reshape(bp, seq, d), y_s.reshape(bs, t_dec, d),
            c_p[None], nn_p[None], m_p[None], *kv_p,
            c_s[None], nn_s[None], m_s[None], k0_s, v0_s, k1_s, v1_s, k2_s, v2_s)
```

```python
import functools

import jax
import jax.numpy as jnp
import numpy as np
from jax import lax
from jax.experimental import pallas as pl
from jax.experimental.pallas import tpu as pltpu

F32 = jnp.float32
BF16 = jnp.bfloat16

D_MODEL = 1024
PAST_LEN = 8192
M_HEADS = 4
M_HEAD_DIM = 256
M_WIDTH = M_HEADS * M_HEAD_DIM
A_GROUPS = ((128, 1), (512, 4), (2048, 16))
A_HEADS_PER_GROUP = 4
A_HEAD_DIM = 128
A_GROUP_WIDTH = A_HEADS_PER_GROUP * A_HEAD_DIM
A_WIDTH = len(A_GROUPS) * A_GROUP_WIDTH
ROPE_THETA = 500000.0
ROPE_DIM = A_HEAD_DIM // 4
N_EXPERTS = 32
TOP_K = 4
D_FF = D_MODEL
SWIGLU_ALPHA = 1.702
SWIGLU_LIMIT = 7.0
PLE_DIM = 256
EPS = 1e-6

LANES = 128
VMEM_LIMIT = 56 * 1024 * 1024
NEG_INF = float("-inf")
PROJ_TM = 512


def _cparams(sem):
    return pltpu.CompilerParams(dimension_semantics=sem, vmem_limit_bytes=VMEM_LIMIT)


def _rms(x, gain):
    return x * lax.rsqrt(jnp.mean(x * x, axis=-1, keepdims=True) + EPS) * gain


def _sigmoid(x):
    return 1.0 / (1.0 + jnp.exp(-x))


def _log_sigmoid(x):
    return jnp.minimum(x, 0.0) - jnp.log(1.0 + jnp.exp(-jnp.abs(x)))


def _norm_proj_kernel(x_ref, g_ref, w_ref, *rest, n_rope_blocks, heads_per_block, has_extra, col_axis):
    rest = list(rest)
    if n_rope_blocks:
        cos_ref, sinm_ref, sinp_ref = rest[:3]
        rest = rest[3:]
    if has_extra:
        we_ref, o_ref, oe_ref = rest
    else:
        o_ref, = rest
    j = pl.program_id(col_axis)
    xn = _rms(x_ref[...], g_ref[...]).astype(BF16)
    z = jnp.dot(xn, w_ref[...], preferred_element_type=F32)
    if has_extra:
        oe_ref[...] = jnp.dot(xn, we_ref[...], preferred_element_type=F32)
    if n_rope_blocks:
        @pl.when(j < n_rope_blocks)
        def _():
            cosf, sinm, sinp = cos_ref[...], sinm_ref[...], sinp_ref[...]
            for h in range(heads_per_block):
                zh = z[:, h * LANES:(h + 1) * LANES]
                rot = (zh * cosf + pltpu.roll(zh, LANES - ROPE_DIM // 2, 1) * sinm
                       + pltpu.roll(zh, ROPE_DIM // 2, 1) * sinp)
                o_ref[:, h * LANES:(h + 1) * LANES] = rot.astype(o_ref.dtype)

        @pl.when(j >= n_rope_blocks)
        def _():
            o_ref[...] = z.astype(o_ref.dtype)
    else:
        o_ref[...] = z.astype(o_ref.dtype)


def norm_proj(x, gain, w, out_dtype, tn, rope=None, n_rope_blocks=0, rope_block=None, w_extra=None):
    n, d = x.shape
    tm = PROJ_TM
    ncol = w.shape[1]
    rows_outer = w_extra is not None
    grid = (n // tm, ncol // tn) if rows_outer else (ncol // tn, n // tm)

    def ix(fn):
        return (lambda i, j: fn(i, j)) if rows_outer else (lambda j, i: fn(i, j))

    in_specs = [pl.BlockSpec((tm, d), ix(lambda i, j: (i, 0))),
                pl.BlockSpec((1, d), ix(lambda i, j: (0, 0))),
                pl.BlockSpec((d, tn), ix(lambda i, j: (0, j)))]
    args = [x, gain.reshape(1, d), w]
    if n_rope_blocks:
        in_specs += [pl.BlockSpec((tm, LANES), ix(lambda i, j: (rope_block(i), 0)))] * 3
        args += list(rope)
    out_specs = pl.BlockSpec((tm, tn), ix(lambda i, j: (i, j)))
    out_shape = jax.ShapeDtypeStruct((n, ncol), out_dtype)
    if w_extra is not None:
        in_specs.append(pl.BlockSpec((d, LANES), ix(lambda i, j: (0, 0))))
        args.append(w_extra)
        out_specs = [out_specs, pl.BlockSpec((tm, LANES), ix(lambda i, j: (i, 0)))]
        out_shape = [out_shape, jax.ShapeDtypeStruct((n, LANES), F32)]
    return pl.pallas_call(
        functools.partial(_norm_proj_kernel, n_rope_blocks=n_rope_blocks, heads_per_block=tn // LANES,
                          has_extra=w_extra is not None, col_axis=1 if rows_outer else 0),
        grid=grid,
        in_specs=in_specs,
        out_specs=out_specs,
        out_shape=out_shape,
        compiler_params=_cparams(("arbitrary", "arbitrary")),
        name="norm_proj",
    )(*args)


def rope_tables(pos):
    half = ROPE_DIM // 2
    inv = (1.0 / (np.float32(ROPE_THETA) ** (np.arange(0, ROPE_DIM, 2, dtype=np.float32) / ROPE_DIM))).astype(np.float32)
    ang = (pos.astype(np.float32)[:, None] * inv[None, :]).astype(np.float64)
    cos, sin = np.cos(ang), np.sin(ang)
    n = pos.shape[0]
    ones = np.ones((n, LANES - ROPE_DIM))
    zeros = np.zeros((n, LANES - ROPE_DIM))
    zh = np.zeros((n, half))
    cosf = np.concatenate([cos, cos, ones], axis=1).astype(np.float32)
    sinm = np.concatenate([-sin, zh, zeros], axis=1).astype(np.float32)
    sinp = np.concatenate([zh, sin, zeros], axis=1).astype(np.float32)
    return jnp.asarray(cosf), jnp.asarray(sinm), jnp.asarray(sinp)


KV_ROWS = 512


def _kv_out_kernel(zk_ref, zv_ref, *outs, seq):
    s = pl.program_id(1)
    last = s == pl.num_programs(1) - 1
    for g, (win, _) in enumerate(A_GROUPS):
        keep = min(win, seq)
        for z_ref, o_ref in ((zk_ref, outs[2 * g]), (zv_ref, outs[2 * g + 1])):
            def write(z_ref=z_ref, o_ref=o_ref, rows=min(keep, KV_ROWS), g=g):
                for h in range(A_HEADS_PER_GROUP):
                    col = (g * A_HEADS_PER_GROUP + h) * LANES
                    o_ref[0, 0, :, h, :] = z_ref[KV_ROWS - rows:, col:col + LANES]

            if keep >= seq:
                write()
            else:
                assert keep <= KV_ROWS
                pl.when(last)(write)


def kv_out(zatt, batch, seq):
    steps = seq // KV_ROWS
    in_specs = [pl.BlockSpec((KV_ROWS, A_WIDTH), lambda b, s: (b * steps + s, 1)),
                pl.BlockSpec((KV_ROWS, A_WIDTH), lambda b, s: (b * steps + s, 2))]
    out_specs, out_shape = [], []
    for win, _ in A_GROUPS:
        keep = min(win, seq)
        rows = min(keep, KV_ROWS)
        idx = (lambda b, s: (0, b, s, 0, 0)) if keep >= seq else (lambda b, s: (0, b, 0, 0, 0))
        for _ in range(2):
            out_specs.append(pl.BlockSpec((1, 1, rows, A_HEADS_PER_GROUP, A_HEAD_DIM), idx))
            out_shape.append(jax.ShapeDtypeStruct((1, batch, keep, A_HEADS_PER_GROUP, A_HEAD_DIM), F32))
    return pl.pallas_call(
        functools.partial(_kv_out_kernel, seq=seq),
        grid=(batch, steps),
        in_specs=in_specs,
        out_specs=out_specs,
        out_shape=out_shape,
        compiler_params=_cparams(("arbitrary", "arbitrary")),
        name="kv_out",
    )(zatt, zatt)


def _mlstm_kernel(bias_ref, q_ref, k_ref, v_ref, o_ref, gi_ref, gfr_ref, gfc_ref, mn_ref, *rest,
                  chunk, valid_len, has_state):
    if has_state:
        c0_ref, n0_ref, m0_ref, h_ref, c_out, n_out, m_out, c_s, n_s, m_s = rest
    else:
        h_ref, c_out, n_out, m_out, c_s, n_s, m_s = rest
    c = pl.program_id(1)

    @pl.when(c == 0)
    def _():
        if has_state:
            c_s[...] = c0_ref[0]
            n_s[...] = n0_ref[0]
            m_s[...] = m0_ref[0]
        else:
            c_s[...] = jnp.zeros_like(c_s)
            n_s[...] = jnp.zeros_like(n_s)
            m_s[...] = jnp.zeros_like(m_s)

    last = c == pl.num_programs(1) - 1
    for hd in range(M_HEADS):
        cols = slice(hd * M_HEAD_DIM, (hd + 1) * M_HEAD_DIM)
        _mlstm_head(hd, cols, last, bias_ref, q_ref, k_ref, v_ref, o_ref, gi_ref, gfr_ref, gfc_ref, mn_ref,
                    h_ref, c_out, n_out, m_out, c_s, n_s, m_s, chunk, valid_len)


def _mlstm_head(hd, cols, last, bias_ref, q_ref, k_ref, v_ref, o_ref, gi_ref, gfr_ref, gfc_ref, mn_ref,
                h_ref, c_out, n_out, m_out, c_s, n_s, m_s, chunk, valid_len):
    L = chunk
    q = q_ref[:, cols]
    k = k_ref[:, cols] * (M_HEAD_DIM ** -0.5)
    v = v_ref[:, cols]
    b_i = bias_ref[0, hd]
    b_f = bias_ref[1, hd]
    i_row = gi_ref[0, 0, hd:hd + 1, :] + b_i
    lf_row = _log_sigmoid(gfr_ref[0, 0, hd:hd + 1, :] + b_f)
    lf_col = _log_sigmoid(gfc_ref[0, 0, :, hd:hd + 1] + b_f)
    row_id = lax.broadcasted_iota(jnp.int32, (L, L), 0)
    col_id = lax.broadcasted_iota(jnp.int32, (L, L), 1)
    if valid_len < L:
        lane = lax.broadcasted_iota(jnp.int32, (1, L), 1)
        sub = lax.broadcasted_iota(jnp.int32, (L, 1), 0)
        i_row = jnp.where(lane < valid_len, i_row, NEG_INF)
        lf_row = jnp.where(lane < valid_len, lf_row, 0.0)
        lf_col = jnp.where(sub < valid_len, lf_col, 0.0)
    causal = col_id <= row_id
    b_col = jnp.sum(jnp.where(causal, lf_row, 0.0), axis=1, keepdims=True)
    b_row = jnp.sum(jnp.where(row_id <= col_id, lf_col, 0.0), axis=0, keepdims=True)
    m_prev = m_s[hd]
    dmat = jnp.where(causal, b_col - b_row + i_row, NEG_INF)
    inter = b_col + m_prev
    mj = jnp.maximum(inter, jnp.max(dmat, axis=1, keepdims=True))
    s = lax.dot_general(q, k, (((1,), (1,)), ((), ())), preferred_element_type=F32)
    sc = s * jnp.exp(dmat - mj)
    a_int = jnp.exp(inter - mj)
    c_prev = c_s[hd]
    n_prev = n_s[hd]
    qc = lax.dot_general(q, c_prev.astype(BF16), (((1,), (1,)), ((), ())), preferred_element_type=F32)
    num = jnp.dot(sc.astype(BF16), v, preferred_element_type=F32) + a_int * qc
    qn = jnp.sum(q.astype(F32) * n_prev, axis=1, keepdims=True)
    den = jnp.sum(sc, axis=1, keepdims=True) + a_int * qn
    h = num / jnp.maximum(jnp.abs(den), jnp.exp(-mj))
    h = h * _sigmoid(o_ref[:, cols].astype(F32))
    h = h * lax.rsqrt(jnp.mean(h * h, axis=-1, keepdims=True) + EPS) * mn_ref[:, cols]
    h_ref[:, cols] = h.astype(h_ref.dtype)

    bl = jnp.sum(lf_row, axis=1, keepdims=True)
    g_row = bl - b_row + i_row
    m_new = jnp.maximum(bl + m_prev, jnp.max(g_row, axis=1, keepdims=True))
    ws_row = jnp.exp(g_row - m_new)
    a_c = jnp.exp(bl + m_prev - m_new)
    vt = (v.astype(F32).T * ws_row).astype(BF16)
    c_new = a_c * c_prev + jnp.dot(vt, k, preferred_element_type=F32)
    ws8 = jnp.broadcast_to(ws_row, (8, L)).astype(BF16)
    n_new = a_c * n_prev + jnp.dot(ws8, k, preferred_element_type=F32)[0:1]
    c_s[hd] = c_new
    n_s[hd] = n_new
    m_s[hd] = m_new

    @pl.when(last)
    def _():
        c_out[0, hd] = c_new
        n_out[0, hd] = n_new
        m_out[0, hd] = m_new


def mlstm(zm, gates_row, gates_col, bias, m_norm, batch, seq, chunk, valid_len, row0, state=None):
    E = M_HEAD_DIM
    nc = seq // chunk
    blk0 = row0 // chunk

    def zspec(col):
        return pl.BlockSpec((chunk, M_WIDTH), lambda b, c: (blk0 + b * nc + c, col))

    in_specs = [pl.BlockSpec(memory_space=pltpu.SMEM),
                zspec(0), zspec(1), zspec(2), zspec(3),
                pl.BlockSpec((1, 1, M_HEADS, chunk), lambda b, c: (b, c, 0, 0)),
                pl.BlockSpec((1, 1, M_HEADS, chunk), lambda b, c: (b, c, 0, 0)),
                pl.BlockSpec((1, 1, chunk, M_HEADS), lambda b, c: (b, c, 0, 0)),
                pl.BlockSpec((1, M_WIDTH), lambda b, c: (0, 0))]
    gi_row, gf_row = gates_row
    args = [bias, zm, zm, zm, zm, gi_row, gf_row, gates_col, m_norm.reshape(1, M_WIDTH)]
    if state is not None:
        c0, n0, m0 = state
        in_specs += [pl.BlockSpec((1, M_HEADS, E, E), lambda b, c: (b, 0, 0, 0)),
                     pl.BlockSpec((1, M_HEADS, 1, E), lambda b, c: (b, 0, 0, 0)),
                     pl.BlockSpec((1, M_HEADS, 1, 1), lambda b, c: (b, 0, 0, 0))]
        args += [c0, n0.reshape(batch, M_HEADS, 1, E), m0.reshape(batch, M_HEADS, 1, 1)]
    out_specs = [pl.BlockSpec((chunk, M_WIDTH), lambda b, c: (b * nc + c, 0)),
                 pl.BlockSpec((1, M_HEADS, E, E), lambda b, c: (b, 0, 0, 0)),
                 pl.BlockSpec((1, M_HEADS, 1, E), lambda b, c: (b, 0, 0, 0)),
                 pl.BlockSpec((1, M_HEADS, 1, 1), lambda b, c: (b, 0, 0, 0))]
    out_shape = [jax.ShapeDtypeStruct((batch * seq, M_WIDTH), BF16),
                 jax.ShapeDtypeStruct((batch, M_HEADS, E, E), F32),
                 jax.ShapeDtypeStruct((batch, M_HEADS, 1, E), F32),
                 jax.ShapeDtypeStruct((batch, M_HEADS, 1, 1), F32)]
    h, c_f, n_f, m_f = pl.pallas_call(
        functools.partial(_mlstm_kernel, chunk=chunk, valid_len=valid_len, has_state=state is not None),
        grid=(batch, nc),
        in_specs=in_specs,
        out_specs=out_specs,
        out_shape=out_shape,
        scratch_shapes=[pltpu.VMEM((M_HEADS, E, E), F32), pltpu.VMEM((M_HEADS, 1, E), F32),
                        pltpu.VMEM((M_HEADS, 1, 1), F32)],
        compiler_params=_cparams(("parallel", "arbitrary")),
        name="mlstm",
    )(*args)
    return h, c_f, n_f.reshape(batch, M_HEADS, E), m_f.reshape(batch, M_HEADS)


ATT_BLK = 128
ATT_UNROLL = 4


def _band_block(qb, kcat, vcat, mask):
    s = lax.dot_general(qb, kcat, (((1,), (1,)), ((), ())), preferred_element_type=F32)
    s = jnp.where(mask, s, NEG_INF)
    mx = jnp.max(s, axis=1, keepdims=True)
    p = jnp.exp(s - mx)
    l = jnp.sum(p, axis=1, keepdims=True)
    o = jnp.dot(p.astype(BF16), vcat, preferred_element_type=F32) / l
    return o, mx + jnp.log(l)


def _attn_prompt_kernel(*refs, seq):
    qkv = refs[:9]
    y_ref = refs[9]
    o_scr = refs[10:13]
    lse_scr = refs[13:16]
    scale = A_HEAD_DIM ** -0.5
    qi = lax.broadcasted_iota(jnp.int32, (ATT_BLK, ATT_BLK), 0)
    ki = lax.broadcasted_iota(jnp.int32, (ATT_BLK, ATT_BLK), 1)
    cur_mask = ki <= qi
    prev_mask = ki >= qi
    band_mask = jnp.concatenate([prev_mask, cur_mask], axis=1)
    for g, (_, dil) in enumerate(A_GROUPS):
        q_ref, k_ref, v_ref = qkv[3 * g:3 * g + 3]
        L = seq // dil
        nb = L // ATT_BLK

        def residue(r, carry, q_ref=q_ref, k_ref=k_ref, v_ref=v_ref, dil=dil, L=L, nb=nb, g=g):
            def rows(first_blk, n_blk):
                if dil == 1:
                    return pl.ds(first_blk * ATT_BLK, n_blk * ATT_BLK)
                return pl.ds(r + first_blk * ATT_BLK * dil, n_blk * ATT_BLK, stride=dil)

            for n in range(nb):
                qb = (q_ref[rows(n, 1), :] * scale).astype(BF16)
                if n == 0:
                    kk, vv, mask = k_ref[rows(0, 1), :], v_ref[rows(0, 1), :], cur_mask
                else:
                    kk, vv, mask = k_ref[rows(n - 1, 2), :], v_ref[rows(n - 1, 2), :], band_mask
                o, lse = _band_block(qb, kk.astype(BF16), vv.astype(BF16), mask)
                o_scr[g][rows(n, 1), :] = o
                lse_scr[g][rows(n, 1), :] = lse
            return carry

        if dil == 1:
            residue(0, 0)
        else:
            lax.fori_loop(0, dil, residue, 0, unroll=ATT_UNROLL)
    l0, l1, l2 = lse_scr[0][...], lse_scr[1][...], lse_scr[2][...]
    mx = jnp.maximum(jnp.maximum(l0, l1), l2)
    w0, w1, w2 = jnp.exp(l0 - mx), jnp.exp(l1 - mx), jnp.exp(l2 - mx)
    y = (w0 * o_scr[0][...] + w1 * o_scr[1][...] + w2 * o_scr[2][...]) / (w0 + w1 + w2)
    y_ref[...] = y.astype(y_ref.dtype)


def attn_prompt(zatt, batch, seq):
    nh = A_WIDTH // LANES

    def spec(col0):
        return pl.BlockSpec((seq, LANES), lambda b, j: (b, col0 + j))

    in_specs, args = [], []
    for g in range(len(A_GROUPS)):
        for part in range(3):
            in_specs.append(spec(part * nh + g * A_HEADS_PER_GROUP))
            args.append(zatt)
    return pl.pallas_call(
        functools.partial(_attn_prompt_kernel, seq=seq),
        grid=(batch, A_HEADS_PER_GROUP),
        in_specs=in_specs,
        out_specs=pl.BlockSpec((seq, LANES), lambda b, j: (b, j)),
        out_shape=jax.ShapeDtypeStruct((batch * seq, A_GROUP_WIDTH), BF16),
        scratch_shapes=[pltpu.VMEM((seq, LANES), F32)] * 3 + [pltpu.VMEM((seq, 1), F32)] * 3,
        compiler_params=_cparams(("parallel", "parallel")),
        name="attn_prompt",
    )(*args)


def _attn_sample_kernel(new_ref, *refs, t_dec):
    caches = refs[:6]
    y_ref = refs[6]
    nh = A_WIDTH // LANES
    scale = A_HEAD_DIM ** -0.5
    jj = lax.broadcasted_iota(jnp.int32, (ATT_BLK, 1, 1), 0)
    for t in range(t_dec):
        outs, lses = [], []
        for g, (_, dil) in enumerate(A_GROUPS):
            kc_ref, vc_ref = caches[2 * g], caches[2 * g + 1]
            h0 = g * A_HEADS_PER_GROUP
            q = new_ref[0, t, h0:h0 + A_HEADS_PER_GROUP, :] * scale
            kc = kc_ref[0, :, t % dil]
            vc = vc_ref[0, :, t % dil]
            s_c = jnp.sum(kc * q[None], axis=-1, keepdims=True)
            if t // dil > 0:
                s_c = jnp.where(jj >= t // dil, s_c, NEG_INF)
            mx = jnp.max(s_c, axis=0)
            new_u = [u for u in range(t + 1) if (t - u) % dil == 0]
            s_new = []
            for u in new_u:
                k_u = new_ref[0, u, nh + h0:nh + h0 + A_HEADS_PER_GROUP, :]
                s_u = jnp.sum(k_u * q, axis=-1, keepdims=True)
                s_new.append(s_u)
                mx = jnp.maximum(mx, s_u)
            p_c = jnp.exp(s_c - mx[None])
            l = jnp.sum(p_c, axis=0)
            acc = jnp.sum(p_c * vc, axis=0)
            for u, s_u in zip(new_u, s_new):
                p_u = jnp.exp(s_u - mx)
                v_u = new_ref[0, u, 2 * nh + h0:2 * nh + h0 + A_HEADS_PER_GROUP, :]
                l = l + p_u
                acc = acc + p_u * v_u
            outs.append(acc / l)
            lses.append(mx + jnp.log(l))
        mxg = jnp.maximum(jnp.maximum(lses[0], lses[1]), lses[2])
        ws = [jnp.exp(l_g - mxg) for l_g in lses]
        y = (ws[0] * outs[0] + ws[1] * outs[1] + ws[2] * outs[2]) / (ws[0] + ws[1] + ws[2])
        y_ref[0, t] = y.astype(y_ref.dtype)


def attn_sample(new_qkv, caches, batch, t_dec):
    in_specs = [pl.BlockSpec((1, t_dec, new_qkv.shape[2], LANES), lambda b: (b, 0, 0, 0))]
    args = [new_qkv]
    for (win, dil), kv in zip(A_GROUPS, caches):
        assert win == ATT_BLK * dil and kv[0].shape[1] == win
        n_res = min(dil, t_dec)
        for buf in kv:
            in_specs.append(pl.BlockSpec((1, ATT_BLK, n_res, A_HEADS_PER_GROUP, LANES),
                                         lambda b: (b, 0, 0, 0, 0)))
            args.append(buf.reshape(batch, ATT_BLK, dil, A_HEADS_PER_GROUP, LANES))
    return pl.pallas_call(
        functools.partial(_attn_sample_kernel, t_dec=t_dec),
        grid=(batch,),
        in_specs=in_specs,
        out_specs=pl.BlockSpec((1, t_dec, A_HEADS_PER_GROUP, LANES), lambda b: (b, 0, 0, 0)),
        out_shape=jax.ShapeDtypeStruct((batch, t_dec, A_HEADS_PER_GROUP, LANES), BF16),
        compiler_params=_cparams(("parallel",)),
        name="attn_sample",
    )(*args)


MOE_TM = 256
MOE_CHUNK = 8
MOE_BM = 256
MOE_TILE_ROWS = TOP_K * MOE_TM + N_EXPERTS * MOE_CHUNK
ROUTE_EXPERT, ROUTE_POS, ROUTE_GATE = 0, TOP_K, 2 * TOP_K


def _merge_route_kernel(h_ref, ha_ref, hb_ref, ga_ref, gb_ref, wpa_ref, wpb_ref, wo_ref, g2_ref,
                        wrh_ref, wrl_ref, br_ref, h1_ref, xt_ref, route_ref, cnt_ref):
    tm = h_ref.shape[0]
    rows = xt_ref.shape[0]
    a = jnp.dot(ha_ref[...], wpa_ref[...], preferred_element_type=F32)
    b = jnp.dot(hb_ref[...], wpb_ref[...], preferred_element_type=F32)
    u = _sigmoid(ga_ref[...].astype(F32)) * a + _sigmoid(gb_ref[...].astype(F32)) * b
    h1 = h_ref[...] + jnp.dot(u.astype(BF16), wo_ref[...], preferred_element_type=F32)
    h1_ref[...] = h1
    xn = _rms(h1, g2_ref[...])
    x_hi = xn.astype(BF16)
    x_lo = (xn - x_hi.astype(F32)).astype(BF16)
    logits = (jnp.dot(x_hi, wrh_ref[...], preferred_element_type=F32)
              + jnp.dot(x_lo, wrh_ref[...], preferred_element_type=F32)
              + jnp.dot(x_hi, wrl_ref[...], preferred_element_type=F32)) + br_ref[...]
    lane = lax.broadcasted_iota(jnp.int32, (tm, LANES), 1)
    logits = jnp.where(lane < N_EXPERTS, logits, NEG_INF)
    vals, hots = [], []
    work = logits
    for _ in range(TOP_K):
        mx = jnp.max(work, axis=1, keepdims=True)
        idx = jnp.min(jnp.where(work == mx, lane, LANES), axis=1, keepdims=True)
        hot = lane == idx
        work = jnp.where(hot, NEG_INF, work)
        vals.append(mx)
        hots.append(hot)
    exps = [jnp.exp(v - vals[0]) for v in vals]
    denom = exps[0] + exps[1] + exps[2] + exps[3]
    hot_f = jnp.zeros((tm, LANES), F32)
    for hot in hots:
        hot_f = hot_f + jnp.where(hot, 1.0, 0.0)
    r_id = lax.broadcasted_iota(jnp.int32, (tm, tm), 0)
    c_id = lax.broadcasted_iota(jnp.int32, (tm, tm), 1)
    earlier = jnp.where(c_id < r_id, 1.0, 0.0).astype(BF16)
    rank = jnp.dot(earlier, hot_f.astype(BF16), preferred_element_type=F32)
    cnt = jnp.sum(hot_f, axis=0, keepdims=True)
    padded = jnp.floor((cnt + (MOE_CHUNK - 1)) * (1.0 / MOE_CHUNK)) * MOE_CHUNK
    la = lax.broadcasted_iota(jnp.int32, (LANES, LANES), 0)
    lb = lax.broadcasted_iota(jnp.int32, (LANES, LANES), 1)
    before = jnp.where(la < lb, 1.0, 0.0).astype(BF16)
    run_start = jnp.dot(jnp.broadcast_to(padded, (8, LANES)).astype(BF16), before,
                        preferred_element_type=F32)[0:1]
    pos_all = run_start + rank
    lane_f = lane.astype(F32)
    row_id = lax.broadcasted_iota(jnp.int32, (tm, rows), 1)
    place = jnp.zeros((tm, rows), F32)
    route = jnp.zeros((tm, LANES), F32)
    for k in range(TOP_K):
        e_k = jnp.sum(jnp.where(hots[k], lane_f, 0.0), axis=1, keepdims=True)
        p_k = jnp.sum(jnp.where(hots[k], pos_all, 0.0), axis=1, keepdims=True)
        place = place + jnp.where(row_id == p_k.astype(jnp.int32), 1.0, 0.0)
        route = jnp.where(lane == ROUTE_EXPERT + k, e_k, route)
        route = jnp.where(lane == ROUTE_POS + k, p_k, route)
        route = jnp.where(lane == ROUTE_GATE + k, exps[k] / denom, route)
    route_ref[...] = route
    cnt_ref[0] = cnt
    xt_ref[...] = lax.dot_general(place.astype(BF16), x_hi, (((0,), (0,)), ((), ())),
                                  preferred_element_type=F32)


def merge_route(h, ha, hb, zmg, w_pa, w_pb, w_o, norm2, w_r_hi, w_r_lo, b_r):
    n, d = h.shape
    tm = MOE_TM
    nt = n // tm

    def full(arr):
        return pl.BlockSpec(arr.shape, lambda i: (0,) * arr.ndim)

    weights = [w_pa, w_pb, w_o, norm2.reshape(1, d), w_r_hi, w_r_lo, b_r]
    return pl.pallas_call(
        _merge_route_kernel,
        grid=(nt,),
        in_specs=[pl.BlockSpec((tm, d), lambda i: (i, 0)),
                  pl.BlockSpec((tm, M_WIDTH), lambda i: (i, 0)),
                  pl.BlockSpec((tm, A_GROUP_WIDTH), lambda i: (i, 0)),
                  pl.BlockSpec((tm, d), lambda i: (i, 0)),
                  pl.BlockSpec((tm, d), lambda i: (i, 1))] + [full(wt) for wt in weights],
        out_specs=[pl.BlockSpec((tm, d), lambda i: (i, 0)),
                   pl.BlockSpec((MOE_TILE_ROWS, d), lambda i: (i, 0)),
                   pl.BlockSpec((tm, LANES), lambda i: (i, 0)),
                   pl.BlockSpec((1, 1, LANES), lambda i: (i, 0, 0))],
        out_shape=[jax.ShapeDtypeStruct((n, d), F32),
                   jax.ShapeDtypeStruct((nt * MOE_TILE_ROWS, d), F32),
                   jax.ShapeDtypeStruct((n, LANES), F32),
                   jax.ShapeDtypeStruct((nt, 1, LANES), F32)],
        compiler_params=_cparams(("parallel",)),
        name="merge_route",
    )(h, ha, hb, zmg, zmg, *weights)


def moe_plan(cnt, n_blocks):
    nt = cnt.shape[0]
    cpb = MOE_BM // MOE_CHUNK
    cpt = MOE_TILE_ROWS // MOE_CHUNK
    i32 = jnp.int32
    cnt = cnt[:, 0, :N_EXPERTS].astype(i32)
    nch = (cnt + MOE_CHUNK - 1) // MOE_CHUNK
    off_incl = jnp.cumsum(nch, axis=1)
    off_ch = off_incl - nch
    seq_incl = jnp.cumsum(nch, axis=0)
    base_ch = seq_incl - nch
    tot = seq_incl[-1]
    nb = (tot + cpb - 1) // cpb
    bend = jnp.cumsum(nb)
    bstart = bend - nb
    n_used = bend[-1:]
    blk = jnp.arange(n_blocks, dtype=i32)
    blk_exp = jnp.minimum(jnp.sum(blk[:, None] >= bend[None, :], axis=1), N_EXPERTS - 1).astype(i32)
    q = (blk - bstart[blk_exp])[:, None] * cpb + jnp.arange(cpb, dtype=i32)[None, :]
    seq_e = seq_incl.T[blk_exp]
    tile = jnp.minimum(jnp.sum(q[:, :, None] >= seq_e[:, None, :], axis=2), nt - 1).astype(i32)
    e_b = jnp.broadcast_to(blk_exp[:, None], tile.shape)
    src = (tile * cpt + off_ch[tile, e_b] + q - base_ch[tile, e_b]) * MOE_CHUNK
    src = jnp.where(q < tot[blk_exp][:, None], src, 0).astype(i32)
    slot = jnp.arange(cpt, dtype=i32)
    e_s = jnp.sum(slot[None, :, None] >= off_incl[:, None, :], axis=2).astype(i32)
    e_c = jnp.minimum(e_s, N_EXPERTS - 1)
    c = slot[None, :] - jnp.take_along_axis(off_ch, e_c, axis=1)
    back = (bstart[e_c] * cpb + jnp.take_along_axis(base_ch, e_c, axis=1) + c) * MOE_CHUNK
    back = jnp.where(e_s < N_EXPERTS, back, 0).astype(i32)
    return blk_exp, n_used.astype(i32), src.reshape(-1), back.reshape(-1)


def _swiglu(h):
    x_glu = jnp.minimum(h[:, :D_FF], SWIGLU_LIMIT)
    x_lin = jnp.clip(h[:, D_FF:], -SWIGLU_LIMIT, SWIGLU_LIMIT)
    return x_glu * _sigmoid(SWIGLU_ALPHA * x_glu) * (x_lin + 1.0)


def _chunk_gather_start(src_hbm, rows_ref, first, dst_ref, sem):
    def start(c, carry):
        r = pl.multiple_of(rows_ref[first + c], MOE_CHUNK)
        d0 = pl.multiple_of(c * MOE_CHUNK, MOE_CHUNK)
        pltpu.make_async_copy(src_hbm.at[pl.ds(r, MOE_CHUNK)], dst_ref.at[pl.ds(d0, MOE_CHUNK)], sem).start()
        return carry

    lax.fori_loop(0, dst_ref.shape[0] // MOE_CHUNK, start, 0)


def _chunk_gather_wait(src_hbm, dst_ref, sem):
    pltpu.make_async_copy(src_hbm.at[pl.ds(0, dst_ref.shape[0])], dst_ref, sem).wait()


def _moe_kernel(blk_exp_ref, n_used_ref, rows_ref, xt_hbm, w1_ref, b1_ref, w2_ref, b2_ref, y_ref,
                xbuf, w1b, w2b, sems):
    i = pl.program_id(0)
    n_used = n_used_ref[0]
    cpb = xbuf.shape[1] // MOE_CHUNK

    def fetch(blk, slot):
        _chunk_gather_start(xt_hbm, rows_ref, blk * cpb, xbuf.at[slot], sems.at[slot])

    @pl.when(jnp.logical_and(i == 0, n_used > 0))
    def _():
        fetch(0, 0)

    @pl.when(i + 1 < n_used)
    def _():
        fetch(i + 1, (i + 1) % 2)

    @pl.when(i < n_used)
    def _():
        slot = i % 2
        _chunk_gather_wait(xt_hbm, xbuf.at[slot], sems.at[slot])
        changed = jnp.logical_or(i == 0, blk_exp_ref[i] != blk_exp_ref[jnp.maximum(i - 1, 0)])

        @pl.when(changed)
        def _():
            w1b[...] = w1_ref[0].astype(BF16)
            w2b[...] = w2_ref[0].astype(BF16)

        h = jnp.dot(xbuf[slot].astype(BF16), w1b[...], preferred_element_type=F32) + b1_ref[0]
        act = _swiglu(h)
        y_ref[...] = jnp.dot(act.astype(BF16), w2b[...], preferred_element_type=F32) + b2_ref[0]

    @pl.when(i >= n_used)
    def _():
        y_ref[...] = jnp.zeros_like(y_ref)


def moe_experts(xt, blk_exp, n_used, chunk_rows, w1, b1, w2, b2):
    d = xt.shape[1]
    nblk = blk_exp.shape[0]
    ne = w1.shape[0]
    bm = MOE_BM
    grid_spec = pltpu.PrefetchScalarGridSpec(
        num_scalar_prefetch=3,
        grid=(nblk,),
        in_specs=[pl.BlockSpec(memory_space=pl.ANY),
                  pl.BlockSpec((1, d, 2 * D_FF), lambda i, be, nu, cr: (be[i], 0, 0)),
                  pl.BlockSpec((1, 1, 2 * D_FF), lambda i, be, nu, cr: (be[i], 0, 0)),
                  pl.BlockSpec((1, D_FF, d), lambda i, be, nu, cr: (be[i], 0, 0)),
                  pl.BlockSpec((1, 1, d), lambda i, be, nu, cr: (be[i], 0, 0))],
        out_specs=pl.BlockSpec((bm, d), lambda i, be, nu, cr: (i, 0)),
        scratch_shapes=[pltpu.VMEM((2, bm, d), F32), pltpu.VMEM((d, 2 * D_FF), BF16),
                        pltpu.VMEM((D_FF, d), BF16), pltpu.SemaphoreType.DMA((2,))],
    )
    return pl.pallas_call(
        _moe_kernel,
        grid_spec=grid_spec,
        out_shape=jax.ShapeDtypeStruct((nblk * bm, d), F32),
        compiler_params=_cparams(("arbitrary",)),
        name="moe_experts",
    )(blk_exp, n_used, chunk_rows, xt, w1, b1.reshape(ne, 1, -1), w2, b2.reshape(ne, 1, -1))


def _combine_kernel(rows_ref, ys_hbm, h1_ref, route_ref, p_ref, wple_ref, wpg_ref, g3_ref, gf_ref, y_ref,
                    ybuf, sems, *, tile0):
    i = pl.program_id(0)
    n = pl.num_programs(0)
    tm = h1_ref.shape[0]
    rows = ybuf.shape[1]
    cpt = rows // MOE_CHUNK

    def fetch(step, slot):
        _chunk_gather_start(ys_hbm, rows_ref, (tile0 + step) * cpt, ybuf.at[slot], sems.at[slot])

    @pl.when(i == 0)
    def _():
        fetch(0, 0)

    @pl.when(i + 1 < n)
    def _():
        fetch(i + 1, (i + 1) % 2)

    slot = i % 2
    _chunk_gather_wait(ys_hbm, ybuf.at[slot], sems.at[slot])
    route = route_ref[...]
    row_id = lax.broadcasted_iota(jnp.int32, (tm, rows), 1)
    weight = jnp.zeros((tm, rows), F32)
    for k in range(TOP_K):
        p_k = route[:, ROUTE_POS + k:ROUTE_POS + k + 1].astype(jnp.int32)
        weight = weight + jnp.where(row_id == p_k, route[:, ROUTE_GATE + k:ROUTE_GATE + k + 1], 0.0)
    h2 = h1_ref[...] + jnp.dot(weight.astype(BF16), ybuf[slot].astype(BF16), preferred_element_type=F32)
    ple = jnp.dot(p_ref[...].astype(BF16), wple_ref[...], preferred_element_type=F32)
    gate = _sigmoid(jnp.dot(_rms(h2, g3_ref[...]).astype(BF16), wpg_ref[...], preferred_element_type=F32))
    h3 = h2 + ple * gate
    y_ref[...] = _rms(h3, gf_ref[...])


def moe_combine(ys, back_rows, h1, route, p, w_ple, w_pg, norm3, norm_f, tile0, n_tiles):
    d = h1.shape[1]
    tm = MOE_TM

    def full(arr):
        return pl.BlockSpec(arr.shape, lambda i, br: (0,) * arr.ndim)

    weights = [w_ple, w_pg, norm3.reshape(1, d), norm_f.reshape(1, d)]
    grid_spec = pltpu.PrefetchScalarGridSpec(
        num_scalar_prefetch=1,
        grid=(n_tiles,),
        in_specs=[pl.BlockSpec(memory_space=pl.ANY),
                  pl.BlockSpec((tm, d), lambda i, br: (tile0 + i, 0)),
                  pl.BlockSpec((tm, LANES), lambda i, br: (tile0 + i, 0)),
                  pl.BlockSpec((tm, PLE_DIM), lambda i, br: (i, 0))] + [full(wt) for wt in weights],
        out_specs=pl.BlockSpec((tm, d), lambda i, br: (i, 0)),
        scratch_shapes=[pltpu.VMEM((2, MOE_TILE_ROWS, d), F32), pltpu.SemaphoreType.DMA((2,))],
    )
    return pl.pallas_call(
        functools.partial(_combine_kernel, tile0=tile0),
        grid_spec=grid_spec,
        out_shape=jax.ShapeDtypeStruct((n_tiles * tm, d), F32),
        compiler_params=_cparams(("arbitrary",)),
        name="moe_combine",
    )(back_rows, ys, h1, route, p, *weights)


def _gate_layouts(zg, batch, seq, chunk):
    nc = seq // chunk
    g = zg[:, :2 * M_HEADS].reshape(batch, nc, chunk, 2, M_HEADS)
    rows = jnp.transpose(g, (3, 0, 1, 4, 2))
    return (rows[0], rows[1]), g[:, :, :, 1, :]


def kernel(x_prompt, x_sample, state_C, state_n, state_m, cache_k0, cache_v0, cache_k1, cache_v1, cache_k2, cache_v2, p_prompt, p_sample, norm1, w_in, b_igate, b_fgate, m_norm, w_pa, w_pb, w_o, norm2, w_router, b_router, w1, b1, w2, b2, norm3, w_ple, w_ple_gate, norm_f):
    bp, seq, d = x_prompt.shape
    bs, t_dec, _ = x_sample.shape
    n_p, n_s = bp * seq, bs * t_dec
    n_all = n_p + n_s
    x_all = jnp.concatenate([x_prompt.reshape(n_p, d), x_sample.reshape(n_s, d)], axis=0)

    w = w_in[0]
    c_gate = 4 * M_WIDTH
    c_att = c_gate + 2 * M_HEADS
    c_mg = c_att + 3 * A_WIDTH
    w_m = w[:, :c_gate].astype(BF16)
    w_gate = jnp.pad(w[:, c_gate:c_att], ((0, 0), (0, LANES - 2 * M_HEADS))).astype(BF16)
    w_att = w[:, c_att:c_mg].astype(BF16)
    w_mg = w[:, c_mg:].astype(BF16)
    rope = rope_tables(np.concatenate([np.arange(seq), np.tile(np.arange(t_dec) + PAST_LEN, bs)]))
    tiles_p, tiles_seq = n_p // PROJ_TM, seq // PROJ_TM

    def rope_block(i):
        return jnp.where(i < tiles_p, i % tiles_seq, tiles_seq + i - tiles_p)

    zm, zgate = norm_proj(x_all, norm1[0], w_m, BF16, 2048, w_extra=w_gate)
    zatt = norm_proj(x_all, norm1[0], w_att, F32, A_WIDTH, rope=rope, n_rope_blocks=2, rope_block=rope_block)
    zmg = norm_proj(x_all, norm1[0], w_mg, BF16, 2048)

    kv_p = kv_out(zatt, bp, seq)
    z_new = zatt[n_p:]
    caches = ((cache_k0[0], cache_v0[0]), (cache_k1[0], cache_v1[0]), (cache_k2[0], cache_v2[0]))
    kv_s = []
    for g in range(len(A_GROUPS)):
        for part in range(2):
            col = (1 + part) * A_WIDTH + g * A_GROUP_WIDTH
            new = z_new[:, col:col + A_GROUP_WIDTH].reshape(bs, t_dec, A_HEADS_PER_GROUP, A_HEAD_DIM)
            kv_s.append(jnp.concatenate([caches[g][part][:, t_dec:], new], axis=1)[None])

    bias = jnp.stack([b_igate[0], b_fgate[0]])
    chunk = 128
    grow_p, gcol_p = _gate_layouts(zgate[:n_p], bp, seq, chunk)
    ha_p, c_p, nn_p, m_p = mlstm(zm, grow_p, gcol_p, bias, m_norm[0], bp, seq, chunk, chunk, 0)
    t_pad = 8
    zm_s = jnp.pad(zm[n_p:].reshape(bs, t_dec, -1), ((0, 0), (0, t_pad - t_dec), (0, 0))).reshape(bs * t_pad, -1)
    zg_s = jnp.pad(zgate[n_p:].reshape(bs, t_dec, -1), ((0, 0), (0, t_pad - t_dec), (0, 0))).reshape(bs * t_pad, -1)
    grow_s, gcol_s = _gate_layouts(zg_s, bs, t_pad, t_pad)
    ha_s, c_s, nn_s, m_s = mlstm(zm_s, grow_s, gcol_s, bias, m_norm[0], bs, t_pad, t_pad, t_dec, 0,
                                 state=(state_C[0], state_n[0], state_m[0]))
    hb_p = attn_prompt(zatt, bp, seq)
    hb_s = attn_sample(z_new.reshape(bs, t_dec, 3 * A_WIDTH // LANES, LANES), caches, bs, t_dec)
    ha_all = jnp.concatenate([ha_p, ha_s.reshape(bs, t_pad, M_WIDTH)[:, :t_dec].reshape(n_s, M_WIDTH)], axis=0)
    hb_all = jnp.concatenate([hb_p, hb_s.reshape(n_s, A_GROUP_WIDTH)], axis=0)

    w_r = jnp.pad(w_router[0], ((0, 0), (0, LANES - N_EXPERTS)))
    w_r_hi = w_r.astype(BF16)
    w_r_lo = (w_r - w_r_hi.astype(F32)).astype(BF16)
    b_r = jnp.pad(b_router[0], (0, LANES - N_EXPERTS)).reshape(1, LANES)
    h1, xt, route, cnt = merge_route(x_all, ha_all, hb_all, zmg, w_pa[0].astype(BF16), w_pb[0].astype(BF16),
                                     w_o[0].astype(BF16), norm2[0], w_r_hi, w_r_lo, b_r)
    nt = n_all // MOE_TM
    n_blocks = nt * MOE_TILE_ROWS // MOE_BM + N_EXPERTS + 1
    blk_exp, n_used, src_rows, back_rows = moe_plan(cnt, n_blocks)
    ys = moe_experts(xt, blk_exp, n_used, src_rows, w1[0], b1[0], w2[0], b2[0])
    tail_w = (w_ple[0].astype(BF16), w_ple_gate[0].astype(BF16), norm3[0], norm_f)
    nt_p = n_p // MOE_TM
    y_p = moe_combine(ys, back_rows, h1, route, p_prompt[0].reshape(n_p, PLE_DIM), *tail_w, 0, nt_p)
    y_s = moe_combine(ys, back_rows, h1, route, p_sample[0].reshape(n_s, PLE_DIM), *tail_w, nt_p, nt - nt_p)

    return (y_p.reshape(bp, seq, d), y_s.reshape(bs, t_dec, d),
            c_p[None], nn_p[None], m_p[None], *kv_p,
            c_s[None], nn_s[None], m_s[None], *kv_s)
```

```python
import functools

import jax
import jax.numpy as jnp
import numpy as np
from jax import lax
from jax.experimental import pallas as pl
from jax.experimental.pallas import tpu as pltpu

F32 = jnp.float32
BF16 = jnp.bfloat16

D_MODEL = 1024
PAST_LEN = 8192
M_HEADS = 4
M_HEAD_DIM = 256
M_WIDTH = M_HEADS * M_HEAD_DIM
A_GROUPS = ((128, 1), (512, 4), (2048, 16))
A_HEADS_PER_GROUP = 4
A_HEAD_DIM = 128
A_GROUP_WIDTH = A_HEADS_PER_GROUP * A_HEAD_DIM
A_WIDTH = len(A_GROUPS) * A_GROUP_WIDTH
ROPE_THETA = 500000.0
ROPE_DIM = A_HEAD_DIM // 4
N_EXPERTS = 32
TOP_K = 4
D_FF = D_MODEL
SWIGLU_ALPHA = 1.702
SWIGLU_LIMIT = 7.0
PLE_DIM = 256
EPS = 1e-6

LANES = 128
VMEM_LIMIT = 56 * 1024 * 1024
NEG_INF = float("-inf")
PROJ_TM = 512


def _cparams(sem):
    return pltpu.CompilerParams(dimension_semantics=sem, vmem_limit_bytes=VMEM_LIMIT)


def _rms(x, gain):
    return x * lax.rsqrt(jnp.mean(x * x, axis=-1, keepdims=True) + EPS) * gain


def _sigmoid(x):
    return 1.0 / (1.0 + jnp.exp(-x))


def _log_sigmoid(x):
    return jnp.minimum(x, 0.0) - jnp.log(1.0 + jnp.exp(-jnp.abs(x)))


def _norm_proj_kernel(x_ref, g_ref, w_ref, *rest, n_rope_blocks, heads_per_block, has_extra, col_axis):
    rest = list(rest)
    if n_rope_blocks:
        cos_ref, sinm_ref, sinp_ref = rest[:3]
        rest = rest[3:]
    if has_extra:
        we_ref, o_ref, oe_ref = rest
    else:
        o_ref, = rest
    j = pl.program_id(col_axis)
    xn = _rms(x_ref[...], g_ref[...]).astype(BF16)
    z = jnp.dot(xn, w_ref[...], preferred_element_type=F32)
    if has_extra:
        oe_ref[...] = jnp.dot(xn, we_ref[...], preferred_element_type=F32)
    if n_rope_blocks:
        @pl.when(j < n_rope_blocks)
        def _():
            cosf, sinm, sinp = cos_ref[...], sinm_ref[...], sinp_ref[...]
            for h in range(heads_per_block):
                zh = z[:, h * LANES:(h + 1) * LANES]
                rot = (zh * cosf + pltpu.roll(zh, LANES - ROPE_DIM // 2, 1) * sinm
                       + pltpu.roll(zh, ROPE_DIM // 2, 1) * sinp)
                o_ref[:, h * LANES:(h + 1) * LANES] = rot.astype(o_ref.dtype)

        @pl.when(j >= n_rope_blocks)
        def _():
            o_ref[...] = z.astype(o_ref.dtype)
    else:
        o_ref[...] = z.astype(o_ref.dtype)


def norm_proj(x, gain, w, out_dtype, tn, rope=None, n_rope_blocks=0, rope_block=None, w_extra=None):
    n, d = x.shape
    tm = PROJ_TM
    ncol = w.shape[1]
    rows_outer = w_extra is not None
    grid = (n // tm, ncol // tn) if rows_outer else (ncol // tn, n // tm)

    def ix(fn):
        return (lambda i, j: fn(i, j)) if rows_outer else (lambda j, i: fn(i, j))

    in_specs = [pl.BlockSpec((tm, d), ix(lambda i, j: (i, 0))),
                pl.BlockSpec((1, d), ix(lambda i, j: (0, 0))),
                pl.BlockSpec((d, tn), ix(lambda i, j: (0, j)))]
    args = [x, gain.reshape(1, d), w]
    if n_rope_blocks:
        in_specs += [pl.BlockSpec((tm, LANES), ix(lambda i, j: (rope_block(i), 0)))] * 3
        args += list(rope)
    out_specs = pl.BlockSpec((tm, tn), ix(lambda i, j: (i, j)))
    out_shape = jax.ShapeDtypeStruct((n, ncol), out_dtype)
    if w_extra is not None:
        in_specs.append(pl.BlockSpec((d, LANES), ix(lambda i, j: (0, 0))))
        args.append(w_extra)
        out_specs = [out_specs, pl.BlockSpec((tm, LANES), ix(lambda i, j: (i, 0)))]
        out_shape = [out_shape, jax.ShapeDtypeStruct((n, LANES), F32)]
    return pl.pallas_call(
        functools.partial(_norm_proj_kernel, n_rope_blocks=n_rope_blocks, heads_per_block=tn // LANES,
                          has_extra=w_extra is not None, col_axis=1 if rows_outer else 0),
        grid=grid,
        in_specs=in_specs,
        out_specs=out_specs,
        out_shape=out_shape,
        compiler_params=_cparams(("arbitrary", "arbitrary")),
        name="norm_proj",
    )(*args)


def rope_tables(pos):
    half = ROPE_DIM // 2
    inv = (1.0 / (np.float32(ROPE_THETA) ** (np.arange(0, ROPE_DIM, 2, dtype=np.float32) / ROPE_DIM))).astype(np.float32)
    ang = (pos.astype(np.float32)[:, None] * inv[None, :]).astype(np.float64)
    cos, sin = np.cos(ang), np.sin(ang)
    n = pos.shape[0]
    ones = np.ones((n, LANES - ROPE_DIM))
    zeros = np.zeros((n, LANES - ROPE_DIM))
    zh = np.zeros((n, half))
    cosf = np.concatenate([cos, cos, ones], axis=1).astype(np.float32)
    sinm = np.concatenate([-sin, zh, zeros], axis=1).astype(np.float32)
    sinp = np.concatenate([zh, sin, zeros], axis=1).astype(np.float32)
    return jnp.asarray(cosf), jnp.asarray(sinm), jnp.asarray(sinp)


KV_ROWS = 512


def _kv_out_kernel(zk_ref, zv_ref, *outs, seq):
    s = pl.program_id(1)
    last = s == pl.num_programs(1) - 1
    for g, (win, _) in enumerate(A_GROUPS):
        keep = min(win, seq)
        for z_ref, o_ref in ((zk_ref, outs[2 * g]), (zv_ref, outs[2 * g + 1])):
            def write(z_ref=z_ref, o_ref=o_ref, rows=min(keep, KV_ROWS), g=g):
                for h in range(A_HEADS_PER_GROUP):
                    col = (g * A_HEADS_PER_GROUP + h) * LANES
                    o_ref[0, 0, :, h, :] = z_ref[KV_ROWS - rows:, col:col + LANES]

            if keep >= seq:
                write()
            else:
                assert keep <= KV_ROWS
                pl.when(last)(write)


def kv_out(zatt, batch, seq):
    steps = seq // KV_ROWS
    in_specs = [pl.BlockSpec((KV_ROWS, A_WIDTH), lambda b, s: (b * steps + s, 1)),
                pl.BlockSpec((KV_ROWS, A_WIDTH), lambda b, s: (b * steps + s, 2))]
    out_specs, out_shape = [], []
    for win, _ in A_GROUPS:
        keep = min(win, seq)
        rows = min(keep, KV_ROWS)
        idx = (lambda b, s: (0, b, s, 0, 0)) if keep >= seq else (lambda b, s: (0, b, 0, 0, 0))
        for _ in range(2):
            out_specs.append(pl.BlockSpec((1, 1, rows, A_HEADS_PER_GROUP, A_HEAD_DIM), idx))
            out_shape.append(jax.ShapeDtypeStruct((1, batch, keep, A_HEADS_PER_GROUP, A_HEAD_DIM), F32))
    return pl.pallas_call(
        functools.partial(_kv_out_kernel, seq=seq),
        grid=(batch, steps),
        in_specs=in_specs,
        out_specs=out_specs,
        out_shape=out_shape,
        compiler_params=_cparams(("arbitrary", "arbitrary")),
        name="kv_out",
    )(zatt, zatt)


SHIFT_MAX_ROWS = 1024


def _shift_kernel(*refs, pieces, n_buf):
    caches, news, outs = refs[:n_buf], refs[n_buf:2 * n_buf], refs[2 * n_buf:3 * n_buf]
    bufs = refs[3 * n_buf:3 * n_buf + len(pieces)]
    sem_in, sem_out, sem_new = refs[3 * n_buf + len(pieces):]
    b = pl.program_id(0)
    nb = pl.num_programs(0)
    slot = b % 2
    t_dec = news[0].shape[1]

    def copy_in(p, row, sl):
        ci, r0, nr = pieces[p]
        return pltpu.make_async_copy(caches[ci].at[0, row, pl.ds(r0 + t_dec, nr)], bufs[p].at[sl], sem_in.at[sl, p])

    def copy_out(p, row, sl):
        ci, r0, nr = pieces[p]
        return pltpu.make_async_copy(bufs[p].at[sl], outs[ci].at[0, row, pl.ds(r0, nr)], sem_out.at[sl, p])

    def copy_new(ci):
        wb = outs[ci].shape[2]
        return pltpu.make_async_copy(news[ci], outs[ci].at[0, :, pl.ds(wb - t_dec, t_dec)], sem_new.at[ci])

    @pl.when(b == 0)
    def _():
        for ci in range(n_buf):
            copy_new(ci).start()
        for p in range(len(pieces)):
            copy_in(p, 0, 0).start()

    for p in range(len(pieces)):
        copy_in(p, b, slot).wait()
        copy_out(p, b, slot).start()

    @pl.when(b >= 1)
    def _():
        for p in range(len(pieces)):
            copy_out(p, b - 1, 1 - slot).wait()

    @pl.when(b + 1 < nb)
    def _():
        for p in range(len(pieces)):
            copy_in(p, b + 1, 1 - slot).start()

    @pl.when(b == nb - 1)
    def _():
        for p in range(len(pieces)):
            copy_out(p, b, slot).wait()
        for ci in range(n_buf):
            copy_new(ci).wait()


def shift_buffers(caches, news):
    n_buf = len(caches)
    batch = caches[0].shape[1]
    t_dec = news[0].shape[1]
    pieces = []
    for ci, cache in enumerate(caches):
        keep = cache.shape[2] - t_dec
        n_piece = -(-keep // SHIFT_MAX_ROWS)
        step = -(-keep // n_piece)
        for r0 in range(0, keep, step):
            pieces.append((ci, r0, min(step, keep - r0)))
    any_spec = pl.BlockSpec(memory_space=pl.ANY)
    scratch = [pltpu.VMEM((2, nr) + caches[ci].shape[3:], caches[ci].dtype) for ci, _, nr in pieces]
    scratch += [pltpu.SemaphoreType.DMA((2, len(pieces))), pltpu.SemaphoreType.DMA((2, len(pieces))),
                pltpu.SemaphoreType.DMA((n_buf,))]
    return pl.pallas_call(
        functools.partial(_shift_kernel, pieces=tuple(pieces), n_buf=n_buf),
        grid=(batch,),
        in_specs=[any_spec] * (2 * n_buf),
        out_specs=[any_spec] * n_buf,
        out_shape=[jax.ShapeDtypeStruct(c.shape, c.dtype) for c in caches],
        scratch_shapes=scratch,
        compiler_params=_cparams(("arbitrary",)),
        name="shift_buffers",
    )(*caches, *news)


def _mlstm_kernel(bias_ref, q_ref, k_ref, v_ref, o_ref, gi_ref, gfr_ref, gfc_ref, mn_ref, *rest,
                  chunk, valid_len, has_state):
    if has_state:
        c0_ref, n0_ref, m0_ref, h_ref, c_out, n_out, m_out, c_s, n_s, m_s = rest
    else:
        h_ref, c_out, n_out, m_out, c_s, n_s, m_s = rest
    c = pl.program_id(1)

    @pl.when(c == 0)
    def _():
        if has_state:
            c_s[...] = c0_ref[0]
            n_s[...] = n0_ref[0]
            m_s[...] = m0_ref[0]
        else:
            c_s[...] = jnp.zeros_like(c_s)
            n_s[...] = jnp.zeros_like(n_s)
            m_s[...] = jnp.zeros_like(m_s)

    last = c == pl.num_programs(1) - 1
    for hd in range(M_HEADS):
        cols = slice(hd * M_HEAD_DIM, (hd + 1) * M_HEAD_DIM)
        _mlstm_head(hd, cols, last, bias_ref, q_ref, k_ref, v_ref, o_ref, gi_ref, gfr_ref, gfc_ref, mn_ref,
                    h_ref, c_out, n_out, m_out, c_s, n_s, m_s, chunk, valid_len)


def _mlstm_head(hd, cols, last, bias_ref, q_ref, k_ref, v_ref, o_ref, gi_ref, gfr_ref, gfc_ref, mn_ref,
                h_ref, c_out, n_out, m_out, c_s, n_s, m_s, chunk, valid_len):
    L = chunk
    q = q_ref[:, cols]
    k = k_ref[:, cols] * (M_HEAD_DIM ** -0.5)
    v = v_ref[:, cols]
    b_i = bias_ref[0, hd]
    b_f = bias_ref[1, hd]
    i_row = gi_ref[0, 0, hd:hd + 1, :] + b_i
    lf_row = _log_sigmoid(gfr_ref[0, 0, hd:hd + 1, :] + b_f)
    lf_col = _log_sigmoid(gfc_ref[0, 0, :, hd:hd + 1] + b_f)
    row_id = lax.broadcasted_iota(jnp.int32, (L, L), 0)
    col_id = lax.broadcasted_iota(jnp.int32, (L, L), 1)
    if valid_len < L:
        lane = lax.broadcasted_iota(jnp.int32, (1, L), 1)
        sub = lax.broadcasted_iota(jnp.int32, (L, 1), 0)
        i_row = jnp.where(lane < valid_len, i_row, NEG_INF)
        lf_row = jnp.where(lane < valid_len, lf_row, 0.0)
        lf_col = jnp.where(sub < valid_len, lf_col, 0.0)
    causal = col_id <= row_id
    b_col = jnp.sum(jnp.where(causal, lf_row, 0.0), axis=1, keepdims=True)
    b_row = jnp.sum(jnp.where(row_id <= col_id, lf_col, 0.0), axis=0, keepdims=True)
    m_prev = m_s[hd]
    dmat = jnp.where(causal, b_col - b_row + i_row, NEG_INF)
    inter = b_col + m_prev
    mj = jnp.maximum(inter, jnp.max(dmat, axis=1, keepdims=True))
    s = lax.dot_general(q, k, (((1,), (1,)), ((), ())), preferred_element_type=F32)
    sc = s * jnp.exp(dmat - mj)
    a_int = jnp.exp(inter - mj)
    c_prev = c_s[hd]
    n_prev = n_s[hd]
    qc = lax.dot_general(q, c_prev.astype(BF16), (((1,), (1,)), ((), ())), preferred_element_type=F32)
    num = jnp.dot(sc.astype(BF16), v, preferred_element_type=F32) + a_int * qc
    qn = jnp.sum(q.astype(F32) * n_prev, axis=1, keepdims=True)
    den = jnp.sum(sc, axis=1, keepdims=True) + a_int * qn
    h = num / jnp.maximum(jnp.abs(den), jnp.exp(-mj))
    h = h * _sigmoid(o_ref[:, cols].astype(F32))
    h = h * lax.rsqrt(jnp.mean(h * h, axis=-1, keepdims=True) + EPS) * mn_ref[:, cols]
    h_ref[:, cols] = h.astype(h_ref.dtype)

    bl = jnp.sum(lf_row, axis=1, keepdims=True)
    g_row = bl - b_row + i_row
    m_new = jnp.maximum(bl + m_prev, jnp.max(g_row, axis=1, keepdims=True))
    ws_row = jnp.exp(g_row - m_new)
    a_c = jnp.exp(bl + m_prev - m_new)
    vt = (v.astype(F32).T * ws_row).astype(BF16)
    c_new = a_c * c_prev + jnp.dot(vt, k, preferred_element_type=F32)
    ws8 = jnp.broadcast_to(ws_row, (8, L)).astype(BF16)
    n_new = a_c * n_prev + jnp.dot(ws8, k, preferred_element_type=F32)[0:1]
    c_s[hd] = c_new
    n_s[hd] = n_new
    m_s[hd] = m_new

    @pl.when(last)
    def _():
        c_out[0, hd] = c_new
        n_out[0, hd] = n_new
        m_out[0, hd] = m_new


def mlstm(zm, gates_row, gates_col, bias, m_norm, batch, seq, chunk, valid_len, row0, state=None):
    E = M_HEAD_DIM
    nc = seq // chunk
    blk0 = row0 // chunk

    def zspec(col):
        return pl.BlockSpec((chunk, M_WIDTH), lambda b, c: (blk0 + b * nc + c, col))

    in_specs = [pl.BlockSpec(memory_space=pltpu.SMEM),
                zspec(0), zspec(1), zspec(2), zspec(3),
                pl.BlockSpec((1, 1, M_HEADS, chunk), lambda b, c: (b, c, 0, 0)),
                pl.BlockSpec((1, 1, M_HEADS, chunk), lambda b, c: (b, c, 0, 0)),
                pl.BlockSpec((1, 1, chunk, M_HEADS), lambda b, c: (b, c, 0, 0)),
                pl.BlockSpec((1, M_WIDTH), lambda b, c: (0, 0))]
    gi_row, gf_row = gates_row
    args = [bias, zm, zm, zm, zm, gi_row, gf_row, gates_col, m_norm.reshape(1, M_WIDTH)]
    if state is not None:
        c0, n0, m0 = state
        in_specs += [pl.BlockSpec((1, M_HEADS, E, E), lambda b, c: (b, 0, 0, 0)),
                     pl.BlockSpec((1, M_HEADS, 1, E), lambda b, c: (b, 0, 0, 0)),
                     pl.BlockSpec((1, M_HEADS, 1, 1), lambda b, c: (b, 0, 0, 0))]
        args += [c0, n0.reshape(batch, M_HEADS, 1, E), m0.reshape(batch, M_HEADS, 1, 1)]
    out_specs = [pl.BlockSpec((chunk, M_WIDTH), lambda b, c: (b * nc + c, 0)),
                 pl.BlockSpec((1, M_HEADS, E, E), lambda b, c: (b, 0, 0, 0)),
                 pl.BlockSpec((1, M_HEADS, 1, E), lambda b, c: (b, 0, 0, 0)),
                 pl.BlockSpec((1, M_HEADS, 1, 1), lambda b, c: (b, 0, 0, 0))]
    out_shape = [jax.ShapeDtypeStruct((batch * seq, M_WIDTH), BF16),
                 jax.ShapeDtypeStruct((batch, M_HEADS, E, E), F32),
                 jax.ShapeDtypeStruct((batch, M_HEADS, 1, E), F32),
                 jax.ShapeDtypeStruct((batch, M_HEADS, 1, 1), F32)]
    h, c_f, n_f, m_f = pl.pallas_call(
        functools.partial(_mlstm_kernel, chunk=chunk, valid_len=valid_len, has_state=state is not None),
        grid=(batch, nc),
        in_specs=in_specs,
        out_specs=out_specs,
        out_shape=out_shape,
        scratch_shapes=[pltpu.VMEM((M_HEADS, E, E), F32), pltpu.VMEM((M_HEADS, 1, E), F32),
                        pltpu.VMEM((M_HEADS, 1, 1), F32)],
        compiler_params=_cparams(("parallel", "arbitrary")),
        name="mlstm",
    )(*args)
    return h, c_f, n_f.reshape(batch, M_HEADS, E), m_f.reshape(batch, M_HEADS)


ATT_BLK = 128
ATT_UNROLL = 4


def _band_block(qb, kcat, vcat, mask):
    s = lax.dot_general(qb, kcat, (((1,), (1,)), ((), ())), preferred_element_type=F32)
    s = jnp.where(mask, s, NEG_INF)
    mx = jnp.max(s, axis=1, keepdims=True)
    p = jnp.exp(s - mx)
    l = jnp.sum(p, axis=1, keepdims=True)
    o = jnp.dot(p.astype(BF16), vcat, preferred_element_type=F32) / l
    return o, mx + jnp.log(l)


def _attn_prompt_kernel(*refs, seq):
    qkv = refs[:9]
    y_ref = refs[9]
    o_scr = refs[10:13]
    lse_scr = refs[13:16]
    scale = A_HEAD_DIM ** -0.5
    qi = lax.broadcasted_iota(jnp.int32, (ATT_BLK, ATT_BLK), 0)
    ki = lax.broadcasted_iota(jnp.int32, (ATT_BLK, ATT_BLK), 1)
    cur_mask = ki <= qi
    prev_mask = ki >= qi
    band_mask = jnp.concatenate([prev_mask, cur_mask], axis=1)
    for g, (_, dil) in enumerate(A_GROUPS):
        q_ref, k_ref, v_ref = qkv[3 * g:3 * g + 3]
        L = seq // dil
        nb = L // ATT_BLK

        def residue(r, carry, q_ref=q_ref, k_ref=k_ref, v_ref=v_ref, dil=dil, L=L, nb=nb, g=g):
            def rows(first_blk, n_blk):
                if dil == 1:
                    return pl.ds(first_blk * ATT_BLK, n_blk * ATT_BLK)
                return pl.ds(r + first_blk * ATT_BLK * dil, n_blk * ATT_BLK, stride=dil)

            for n in range(nb):
                qb = (q_ref[rows(n, 1), :] * scale).astype(BF16)
                if n == 0:
                    kk, vv, mask = k_ref[rows(0, 1), :], v_ref[rows(0, 1), :], cur_mask
                else:
                    kk, vv, mask = k_ref[rows(n - 1, 2), :], v_ref[rows(n - 1, 2), :], band_mask
                o, lse = _band_block(qb, kk.astype(BF16), vv.astype(BF16), mask)
                o_scr[g][rows(n, 1), :] = o
                lse_scr[g][rows(n, 1), :] = lse
            return carry

        if dil == 1:
            residue(0, 0)
        else:
            lax.fori_loop(0, dil, residue, 0, unroll=ATT_UNROLL)
    l0, l1, l2 = lse_scr[0][...], lse_scr[1][...], lse_scr[2][...]
    mx = jnp.maximum(jnp.maximum(l0, l1), l2)
    w0, w1, w2 = jnp.exp(l0 - mx), jnp.exp(l1 - mx), jnp.exp(l2 - mx)
    y = (w0 * o_scr[0][...] + w1 * o_scr[1][...] + w2 * o_scr[2][...]) / (w0 + w1 + w2)
    y_ref[...] = y.astype(y_ref.dtype)


def attn_prompt(zatt, batch, seq):
    nh = A_WIDTH // LANES

    def spec(col0):
        return pl.BlockSpec((seq, LANES), lambda b, j: (b, col0 + j))

    in_specs, args = [], []
    for g in range(len(A_GROUPS)):
        for part in range(3):
            in_specs.append(spec(part * nh + g * A_HEADS_PER_GROUP))
            args.append(zatt)
    return pl.pallas_call(
        functools.partial(_attn_prompt_kernel, seq=seq),
        grid=(batch, A_HEADS_PER_GROUP),
        in_specs=in_specs,
        out_specs=pl.BlockSpec((seq, LANES), lambda b, j: (b, j)),
        out_shape=jax.ShapeDtypeStruct((batch * seq, A_GROUP_WIDTH), BF16),
        scratch_shapes=[pltpu.VMEM((seq, LANES), F32)] * 3 + [pltpu.VMEM((seq, 1), F32)] * 3,
        compiler_params=_cparams(("parallel", "parallel")),
        name="attn_prompt",
    )(*args)


def _attn_sample_kernel(new_ref, *refs, t_dec):
    caches = refs[:6]
    y_ref = refs[6]
    nh = A_WIDTH // LANES
    scale = A_HEAD_DIM ** -0.5
    jj = lax.broadcasted_iota(jnp.int32, (ATT_BLK, 1, 1), 0)
    for t in range(t_dec):
        outs, lses = [], []
        for g, (_, dil) in enumerate(A_GROUPS):
            kc_ref, vc_ref = caches[2 * g], caches[2 * g + 1]
            h0 = g * A_HEADS_PER_GROUP
            q = new_ref[0, t, h0:h0 + A_HEADS_PER_GROUP, :] * scale
            kc = kc_ref[0, :, t % dil]
            vc = vc_ref[0, :, t % dil]
            s_c = jnp.sum(kc * q[None], axis=-1, keepdims=True)
            if t // dil > 0:
                s_c = jnp.where(jj >= t // dil, s_c, NEG_INF)
            mx = jnp.max(s_c, axis=0)
            new_u = [u for u in range(t + 1) if (t - u) % dil == 0]
            s_new = []
            for u in new_u:
                k_u = new_ref[0, u, nh + h0:nh + h0 + A_HEADS_PER_GROUP, :]
                s_u = jnp.sum(k_u * q, axis=-1, keepdims=True)
                s_new.append(s_u)
                mx = jnp.maximum(mx, s_u)
            p_c = jnp.exp(s_c - mx[None])
            l = jnp.sum(p_c, axis=0)
            acc = jnp.sum(p_c * vc, axis=0)
            for u, s_u in zip(new_u, s_new):
                p_u = jnp.exp(s_u - mx)
                v_u = new_ref[0, u, 2 * nh + h0:2 * nh + h0 + A_HEADS_PER_GROUP, :]
                l = l + p_u
                acc = acc + p_u * v_u
            outs.append(acc / l)
            lses.append(mx + jnp.log(l))
        mxg = jnp.maximum(jnp.maximum(lses[0], lses[1]), lses[2])
        ws = [jnp.exp(l_g - mxg) for l_g in lses]
        y = (ws[0] * outs[0] + ws[1] * outs[1] + ws[2] * outs[2]) / (ws[0] + ws[1] + ws[2])
        y_ref[0, t] = y.astype(y_ref.dtype)


def attn_sample(new_qkv, caches, batch, t_dec):
    in_specs = [pl.BlockSpec((1, t_dec, new_qkv.shape[2], LANES), lambda b: (b, 0, 0, 0))]
    args = [new_qkv]
    for (win, dil), kv in zip(A_GROUPS, caches):
        assert win == ATT_BLK * dil and kv[0].shape[1] == win
        n_res = min(dil, t_dec)
        for buf in kv:
            in_specs.append(pl.BlockSpec((1, ATT_BLK, n_res, A_HEADS_PER_GROUP, LANES),
                                         lambda b: (b, 0, 0, 0, 0)))
            args.append(buf.reshape(batch, ATT_BLK, dil, A_HEADS_PER_GROUP, LANES))
    return pl.pallas_call(
        functools.partial(_attn_sample_kernel, t_dec=t_dec),
        grid=(batch,),
        in_specs=in_specs,
        out_specs=pl.BlockSpec((1, t_dec, A_HEADS_PER_GROUP, LANES), lambda b: (b, 0, 0, 0)),
        out_shape=jax.ShapeDtypeStruct((batch, t_dec, A_HEADS_PER_GROUP, LANES), BF16),
        compiler_params=_cparams(("parallel",)),
        name="attn_sample",
    )(*args)


MOE_TM = 256
MOE_CHUNK = 8
MOE_BM = 256
MOE_TILE_ROWS = TOP_K * MOE_TM + N_EXPERTS * MOE_CHUNK
ROUTE_EXPERT, ROUTE_POS, ROUTE_GATE = 0, TOP_K, 2 * TOP_K


def _merge_route_kernel(h_ref, ha_ref, hb_ref, ga_ref, gb_ref, wpa_ref, wpb_ref, wo_ref, g2_ref,
                        wrh_ref, wrl_ref, br_ref, h1_ref, xt_ref, route_ref, cnt_ref):
    tm = h_ref.shape[0]
    rows = xt_ref.shape[0]
    a = jnp.dot(ha_ref[...], wpa_ref[...], preferred_element_type=F32)
    b = jnp.dot(hb_ref[...], wpb_ref[...], preferred_element_type=F32)
    u = _sigmoid(ga_ref[...].astype(F32)) * a + _sigmoid(gb_ref[...].astype(F32)) * b
    h1 = h_ref[...] + jnp.dot(u.astype(BF16), wo_ref[...], preferred_element_type=F32)
    h1_ref[...] = h1
    xn = _rms(h1, g2_ref[...])
    x_hi = xn.astype(BF16)
    x_lo = (xn - x_hi.astype(F32)).astype(BF16)
    logits = (jnp.dot(x_hi, wrh_ref[...], preferred_element_type=F32)
              + jnp.dot(x_lo, wrh_ref[...], preferred_element_type=F32)
              + jnp.dot(x_hi, wrl_ref[...], preferred_element_type=F32)) + br_ref[...]
    lane = lax.broadcasted_iota(jnp.int32, (tm, LANES), 1)
    logits = jnp.where(lane < N_EXPERTS, logits, NEG_INF)
    vals, hots = [], []
    work = logits
    for _ in range(TOP_K):
        mx = jnp.max(work, axis=1, keepdims=True)
        idx = jnp.min(jnp.where(work == mx, lane, LANES), axis=1, keepdims=True)
        hot = lane == idx
        work = jnp.where(hot, NEG_INF, work)
        vals.append(mx)
        hots.append(hot)
    exps = [jnp.exp(v - vals[0]) for v in vals]
    denom = exps[0] + exps[1] + exps[2] + exps[3]
    hot_f = jnp.zeros((tm, LANES), F32)
    for hot in hots:
        hot_f = hot_f + jnp.where(hot, 1.0, 0.0)
    r_id = lax.broadcasted_iota(jnp.int32, (tm, tm), 0)
    c_id = lax.broadcasted_iota(jnp.int32, (tm, tm), 1)
    earlier = jnp.where(c_id < r_id, 1.0, 0.0).astype(BF16)
    rank = jnp.dot(earlier, hot_f.astype(BF16), preferred_element_type=F32)
    cnt = jnp.sum(hot_f, axis=0, keepdims=True)
    padded = jnp.floor((cnt + (MOE_CHUNK - 1)) * (1.0 / MOE_CHUNK)) * MOE_CHUNK
    la = lax.broadcasted_iota(jnp.int32, (LANES, LANES), 0)
    lb = lax.broadcasted_iota(jnp.int32, (LANES, LANES), 1)
    before = jnp.where(la < lb, 1.0, 0.0).astype(BF16)
    run_start = jnp.dot(jnp.broadcast_to(padded, (8, LANES)).astype(BF16), before,
                        preferred_element_type=F32)[0:1]
    pos_all = run_start + rank
    lane_f = lane.astype(F32)
    row_id = lax.broadcasted_iota(jnp.int32, (tm, rows), 1)
    place = jnp.zeros((tm, rows), F32)
    route = jnp.zeros((tm, LANES), F32)
    for k in range(TOP_K):
        e_k = jnp.sum(jnp.where(hots[k], lane_f, 0.0), axis=1, keepdims=True)
        p_k = jnp.sum(jnp.where(hots[k], pos_all, 0.0), axis=1, keepdims=True)
        place = place + jnp.where(row_id == p_k.astype(jnp.int32), 1.0, 0.0)
        route = jnp.where(lane == ROUTE_EXPERT + k, e_k, route)
        route = jnp.where(lane == ROUTE_POS + k, p_k, route)
        route = jnp.where(lane == ROUTE_GATE + k, exps[k] / denom, route)
    route_ref[...] = route
    cnt_ref[0] = cnt
    xt_ref[...] = lax.dot_general(place.astype(BF16), x_hi, (((0,), (0,)), ((), ())),
                                  preferred_element_type=F32)


def merge_route(h, ha, hb, zmg, w_pa, w_pb, w_o, norm2, w_r_hi, w_r_lo, b_r):
    n, d = h.shape
    tm = MOE_TM
    nt = n // tm

    def full(arr):
        return pl.BlockSpec(arr.shape, lambda i: (0,) * arr.ndim)

    weights = [w_pa, w_pb, w_o, norm2.reshape(1, d), w_r_hi, w_r_lo, b_r]
    return pl.pallas_call(
        _merge_route_kernel,
        grid=(nt,),
        in_specs=[pl.BlockSpec((tm, d), lambda i: (i, 0)),
                  pl.BlockSpec((tm, M_WIDTH), lambda i: (i, 0)),
                  pl.BlockSpec((tm, A_GROUP_WIDTH), lambda i: (i, 0)),
                  pl.BlockSpec((tm, d), lambda i: (i, 0)),
                  pl.BlockSpec((tm, d), lambda i: (i, 1))] + [full(wt) for wt in weights],
        out_specs=[pl.BlockSpec((tm, d), lambda i: (i, 0)),
                   pl.BlockSpec((MOE_TILE_ROWS, d), lambda i: (i, 0)),
                   pl.BlockSpec((tm, LANES), lambda i: (i, 0)),
                   pl.BlockSpec((1, 1, LANES), lambda i: (i, 0, 0))],
        out_shape=[jax.ShapeDtypeStruct((n, d), F32),
                   jax.ShapeDtypeStruct((nt * MOE_TILE_ROWS, d), F32),
                   jax.ShapeDtypeStruct((n, LANES), F32),
                   jax.ShapeDtypeStruct((nt, 1, LANES), F32)],
        compiler_params=_cparams(("parallel",)),
        name="merge_route",
    )(h, ha, hb, zmg, zmg, *weights)


def moe_plan(cnt, n_blocks):
    nt = cnt.shape[0]
    cpb = MOE_BM // MOE_CHUNK
    cpt = MOE_TILE_ROWS // MOE_CHUNK
    i32 = jnp.int32
    cnt = cnt[:, 0, :N_EXPERTS].astype(i32)
    nch = (cnt + MOE_CHUNK - 1) // MOE_CHUNK
    off_incl = jnp.cumsum(nch, axis=1)
    off_ch = off_incl - nch
    seq_incl = jnp.cumsum(nch, axis=0)
    base_ch = seq_incl - nch
    tot = seq_incl[-1]
    nb = (tot + cpb - 1) // cpb
    bend = jnp.cumsum(nb)
    bstart = bend - nb
    n_used = bend[-1:]
    blk = jnp.arange(n_blocks, dtype=i32)
    blk_exp = jnp.minimum(jnp.sum(blk[:, None] >= bend[None, :], axis=1), N_EXPERTS - 1).astype(i32)
    experts = jnp.arange(N_EXPERTS, dtype=i32)
    tiles = jnp.arange(nt, dtype=i32)

    def pick(hot, table):
        return jnp.sum(jnp.where(hot, table, 0), axis=-1).astype(i32)

    hot_e = blk_exp[:, None] == experts[None, :]
    seq_b, off_b, base_b = (pick(hot_e[:, None, :], t[None]) for t in (seq_incl, off_ch, base_ch))
    q = (blk - pick(hot_e, bstart[None]))[:, None] * cpb + jnp.arange(cpb, dtype=i32)[None, :]
    tile = jnp.minimum(jnp.sum(q[:, :, None] >= seq_b[:, None, :], axis=2), nt - 1).astype(i32)
    hot_t = tile[:, :, None] == tiles[None, None, :]
    src = (tile * cpt + pick(hot_t, off_b[:, None, :]) + q - pick(hot_t, base_b[:, None, :])) * MOE_CHUNK
    src = jnp.where(q < pick(hot_e, tot[None])[:, None], src, 0).astype(i32)
    slot = jnp.arange(cpt, dtype=i32)
    e_s = jnp.sum(slot[None, :, None] >= off_incl[:, None, :], axis=2).astype(i32)
    hot_s = e_s[:, :, None] == experts[None, None, :]
    back = (pick(hot_s, bstart[None, None, :]) * cpb + pick(hot_s, base_ch[:, None, :])
            + slot[None, :] - pick(hot_s, off_ch[:, None, :])) * MOE_CHUNK
    back = jnp.where(e_s < N_EXPERTS, back, 0).astype(i32)
    return blk_exp, n_used.astype(i32), src.reshape(-1), back.reshape(-1)


def _swiglu(h):
    x_glu = jnp.minimum(h[:, :D_FF], SWIGLU_LIMIT)
    x_lin = jnp.clip(h[:, D_FF:], -SWIGLU_LIMIT, SWIGLU_LIMIT)
    return x_glu * _sigmoid(SWIGLU_ALPHA * x_glu) * (x_lin + 1.0)


def _chunk_gather_start(src_hbm, rows_ref, first, dst_ref, sem):
    for c in range(dst_ref.shape[0] // MOE_CHUNK):
        r = pl.multiple_of(rows_ref[first + c], MOE_CHUNK)
        pltpu.make_async_copy(src_hbm.at[pl.ds(r, MOE_CHUNK)],
                              dst_ref.at[pl.ds(c * MOE_CHUNK, MOE_CHUNK)], sem).start()


def _chunk_gather_wait(src_hbm, dst_ref, sem):
    pltpu.make_async_copy(src_hbm.at[pl.ds(0, dst_ref.shape[0])], dst_ref, sem).wait()


def _moe_kernel(blk_exp_ref, n_used_ref, rows_ref, xt_hbm, w1_ref, b1_ref, w2_ref, b2_ref, y_ref,
                xbuf, w1b, w2b, sems):
    i = pl.program_id(0)
    n_used = n_used_ref[0]
    cpb = xbuf.shape[1] // MOE_CHUNK

    def fetch(blk, slot):
        _chunk_gather_start(xt_hbm, rows_ref, blk * cpb, xbuf.at[slot], sems.at[slot])

    @pl.when(jnp.logical_and(i == 0, n_used > 0))
    def _():
        fetch(0, 0)

    @pl.when(i < n_used)
    def _():
        slot = i % 2
        fetch(jnp.minimum(i + 1, n_used - 1), 1 - slot)
        _chunk_gather_wait(xt_hbm, xbuf.at[slot], sems.at[slot])
        changed = jnp.logical_or(i == 0, blk_exp_ref[i] != blk_exp_ref[jnp.maximum(i - 1, 0)])

        @pl.when(changed)
        def _():
            w1b[...] = w1_ref[0].astype(BF16)
            w2b[...] = w2_ref[0].astype(BF16)

        h = jnp.dot(xbuf[slot].astype(BF16), w1b[...], preferred_element_type=F32) + b1_ref[0]
        act = _swiglu(h)
        y_ref[...] = jnp.dot(act.astype(BF16), w2b[...], preferred_element_type=F32) + b2_ref[0]

        @pl.when(i == n_used - 1)
        def _():
            _chunk_gather_wait(xt_hbm, xbuf.at[1 - slot], sems.at[1 - slot])

    @pl.when(i >= n_used)
    def _():
        y_ref[...] = jnp.zeros_like(y_ref)


def moe_experts(xt, blk_exp, n_used, chunk_rows, w1, b1, w2, b2):
    d = xt.shape[1]
    nblk = blk_exp.shape[0]
    ne = w1.shape[0]
    bm = MOE_BM
    grid_spec = pltpu.PrefetchScalarGridSpec(
        num_scalar_prefetch=3,
        grid=(nblk,),
        in_specs=[pl.BlockSpec(memory_space=pl.ANY),
                  pl.BlockSpec((1, d, 2 * D_FF), lambda i, be, nu, cr: (be[i], 0, 0)),
                  pl.BlockSpec((1, 1, 2 * D_FF), lambda i, be, nu, cr: (be[i], 0, 0)),
                  pl.BlockSpec((1, D_FF, d), lambda i, be, nu, cr: (be[i], 0, 0)),
                  pl.BlockSpec((1, 1, d), lambda i, be, nu, cr: (be[i], 0, 0))],
        out_specs=pl.BlockSpec((bm, d), lambda i, be, nu, cr: (i, 0)),
        scratch_shapes=[pltpu.VMEM((2, bm, d), F32), pltpu.VMEM((d, 2 * D_FF), BF16),
                        pltpu.VMEM((D_FF, d), BF16), pltpu.SemaphoreType.DMA((2,))],
    )
    return pl.pallas_call(
        _moe_kernel,
        grid_spec=grid_spec,
        out_shape=jax.ShapeDtypeStruct((nblk * bm, d), F32),
        compiler_params=_cparams(("arbitrary",)),
        name="moe_experts",
    )(blk_exp, n_used, chunk_rows, xt, w1, b1.reshape(ne, 1, -1), w2, b2.reshape(ne, 1, -1))


def _combine_kernel(rows_ref, ys_hbm, h1_ref, route_ref, p_ref, wple_ref, wpg_ref, g3_ref, gf_ref, y_ref,
                    ybuf, sems, *, tile0):
    i = pl.program_id(0)
    n = pl.num_programs(0)
    tm = h1_ref.shape[0]
    rows = ybuf.shape[1]
    cpt = rows // MOE_CHUNK

    def fetch(step, slot):
        _chunk_gather_start(ys_hbm, rows_ref, (tile0 + step) * cpt, ybuf.at[slot], sems.at[slot])

    @pl.when(i == 0)
    def _():
        fetch(0, 0)

    slot = i % 2
    fetch(jnp.minimum(i + 1, n - 1), 1 - slot)
    _chunk_gather_wait(ys_hbm, ybuf.at[slot], sems.at[slot])
    route = route_ref[...]
    row_id = lax.broadcasted_iota(jnp.int32, (tm, rows), 1)
    weight = jnp.zeros((tm, rows), F32)
    for k in range(TOP_K):
        p_k = route[:, ROUTE_POS + k:ROUTE_POS + k + 1].astype(jnp.int32)
        weight = weight + jnp.where(row_id == p_k, route[:, ROUTE_GATE + k:ROUTE_GATE + k + 1], 0.0)
    h2 = h1_ref[...] + jnp.dot(weight.astype(BF16), ybuf[slot].astype(BF16), preferred_element_type=F32)
    ple = jnp.dot(p_ref[...].astype(BF16), wple_ref[...], preferred_element_type=F32)
    gate = _sigmoid(jnp.dot(_rms(h2, g3_ref[...]).astype(BF16), wpg_ref[...], preferred_element_type=F32))
    h3 = h2 + ple * gate
    y_ref[...] = _rms(h3, gf_ref[...])

    @pl.when(i == n - 1)
    def _():
        _chunk_gather_wait(ys_hbm, ybuf.at[1 - slot], sems.at[1 - slot])


def moe_combine(ys, back_rows, h1, route, p, w_ple, w_pg, norm3, norm_f, tile0, n_tiles):
    d = h1.shape[1]
    tm = MOE_TM

    def full(arr):
        return pl.BlockSpec(arr.shape, lambda i, br: (0,) * arr.ndim)

    weights = [w_ple, w_pg, norm3.reshape(1, d), norm_f.reshape(1, d)]
    grid_spec = pltpu.PrefetchScalarGridSpec(
        num_scalar_prefetch=1,
        grid=(n_tiles,),
        in_specs=[pl.BlockSpec(memory_space=pl.ANY),
                  pl.BlockSpec((tm, d), lambda i, br: (tile0 + i, 0)),
                  pl.BlockSpec((tm, LANES), lambda i, br: (tile0 + i, 0)),
                  pl.BlockSpec((tm, PLE_DIM), lambda i, br: (i, 0))] + [full(wt) for wt in weights],
        out_specs=pl.BlockSpec((tm, d), lambda i, br: (i, 0)),
        scratch_shapes=[pltpu.VMEM((2, MOE_TILE_ROWS, d), F32), pltpu.SemaphoreType.DMA((2,))],
    )
    return pl.pallas_call(
        functools.partial(_combine_kernel, tile0=tile0),
        grid_spec=grid_spec,
        out_shape=jax.ShapeDtypeStruct((n_tiles * tm, d), F32),
        compiler_params=_cparams(("arbitrary",)),
        name="moe_combine",
    )(back_rows, ys, h1, route, p, *weights)


def _gate_layouts(zg, batch, seq, chunk):
    nc = seq // chunk
    g = zg[:, :2 * M_HEADS].reshape(batch, nc, chunk, 2, M_HEADS)
    rows = jnp.transpose(g, (3, 0, 1, 4, 2))
    return (rows[0], rows[1]), g[:, :, :, 1, :]


def kernel(x_prompt, x_sample, state_C, state_n, state_m, cache_k0, cache_v0, cache_k1, cache_v1, cache_k2, cache_v2, p_prompt, p_sample, norm1, w_in, b_igate, b_fgate, m_norm, w_pa, w_pb, w_o, norm2, w_router, b_router, w1, b1, w2, b2, norm3, w_ple, w_ple_gate, norm_f):
    bp, seq, d = x_prompt.shape
    bs, t_dec, _ = x_sample.shape
    n_p, n_s = bp * seq, bs * t_dec
    n_all = n_p + n_s
    x_all = jnp.concatenate([x_prompt.reshape(n_p, d), x_sample.reshape(n_s, d)], axis=0)

    w = w_in[0]
    c_gate = 4 * M_WIDTH
    c_att = c_gate + 2 * M_HEADS
    c_mg = c_att + 3 * A_WIDTH
    w_m = w[:, :c_gate].astype(BF16)
    w_gate = jnp.pad(w[:, c_gate:c_att], ((0, 0), (0, LANES - 2 * M_HEADS))).astype(BF16)
    w_att = w[:, c_att:c_mg].astype(BF16)
    w_mg = w[:, c_mg:].astype(BF16)
    rope = rope_tables(np.concatenate([np.arange(seq), np.tile(np.arange(t_dec) + PAST_LEN, bs)]))
    tiles_p, tiles_seq = n_p // PROJ_TM, seq // PROJ_TM

    def rope_block(i):
        return jnp.where(i < tiles_p, i % tiles_seq, tiles_seq + i - tiles_p)

    zm, zgate = norm_proj(x_all, norm1[0], w_m, BF16, 2048, w_extra=w_gate)
    zatt = norm_proj(x_all, norm1[0], w_att, F32, A_WIDTH, rope=rope, n_rope_blocks=2, rope_block=rope_block)
    zmg = norm_proj(x_all, norm1[0], w_mg, BF16, 2048)

    kv_p = kv_out(zatt, bp, seq)
    z_new = zatt[n_p:]
    caches5 = (cache_k0, cache_v0, cache_k1, cache_v1, cache_k2, cache_v2)
    caches = tuple((caches5[2 * g][0], caches5[2 * g + 1][0]) for g in range(len(A_GROUPS)))
    news = []
    for g in range(len(A_GROUPS)):
        for part in range(2):
            col = (1 + part) * A_WIDTH + g * A_GROUP_WIDTH
            news.append(z_new[:, col:col + A_GROUP_WIDTH].reshape(bs, t_dec, A_HEADS_PER_GROUP, A_HEAD_DIM))
    kv_s = shift_buffers(caches5, news)

    bias = jnp.stack([b_igate[0], b_fgate[0]])
    chunk = 128
    grow_p, gcol_p = _gate_layouts(zgate[:n_p], bp, seq, chunk)
    ha_p, c_p, nn_p, m_p = mlstm(zm, grow_p, gcol_p, bias, m_norm[0], bp, seq, chunk, chunk, 0)
    t_pad = 8
    zm_s = jnp.pad(zm[n_p:].reshape(bs, t_dec, -1), ((0, 0), (0, t_pad - t_dec), (0, 0))).reshape(bs * t_pad, -1)
    zg_s = jnp.pad(zgate[n_p:].reshape(bs, t_dec, -1), ((0, 0), (0, t_pad - t_dec), (0, 0))).reshape(bs * t_pad, -1)
    grow_s, gcol_s = _gate_layouts(zg_s, bs, t_pad, t_pad)
    ha_s, c_s, nn_s, m_s = mlstm(zm_s, grow_s, gcol_s, bias, m_norm[0], bs, t_pad, t_pad, t_dec, 0,
                                 state=(state_C[0], state_n[0], state_m[0]))
    hb_p = attn_prompt(zatt, bp, seq)
    hb_s = attn_sample(z_new.reshape(bs, t_dec, 3 * A_WIDTH // LANES, LANES), caches, bs, t_dec)
    ha_all = jnp.concatenate([ha_p, ha_s.reshape(bs, t_pad, M_WIDTH)[:, :t_dec].reshape(n_s, M_WIDTH)], axis=0)
    hb_all = jnp.concatenate([hb_p, hb_s.reshape(n_s, A_GROUP_WIDTH)], axis=0)

    w_r = jnp.pad(w_router[0], ((0, 0), (0, LANES - N_EXPERTS)))
    w_r_hi = w_r.astype(BF16)
    w_r_lo = (w_r - w_r_hi.astype(F32)).astype(BF16)
    b_r = jnp.pad(b_router[0], (0, LANES - N_EXPERTS)).reshape(1, LANES)
    h1, xt, route, cnt = merge_route(x_all, ha_all, hb_all, zmg, w_pa[0].astype(BF16), w_pb[0].astype(BF16),
                                     w_o[0].astype(BF16), norm2[0], w_r_hi, w_r_lo, b_r)
    nt = n_all // MOE_TM
    n_blocks = nt * MOE_TILE_ROWS // MOE_BM + N_EXPERTS + 1
    blk_exp, n_used, src_rows, back_rows = moe_plan(cnt, n_blocks)
    ys = moe_experts(xt, blk_exp, n_used, src_rows, w1[0], b1[0], w2[0], b2[0])
    tail_w = (w_ple[0].astype(BF16), w_ple_gate[0].astype(BF16), norm3[0], norm_f)
    nt_p = n_p // MOE_TM
    y_p = moe_combine(ys, back_rows, h1, route, p_prompt[0].reshape(n_p, PLE_DIM), *tail_w, 0, nt_p)
    y_s = moe_combine(ys, back_rows, h1, route, p_sample[0].reshape(n_s, PLE_DIM), *tail_w, nt_p, nt - nt_p)

    return (y_p.reshape(bp, seq, d), y_s.reshape(bs, t_dec, d),
            c_p[None], nn_p[None], m_p[None], *kv_p,
            c_s[None], nn_s[None], m_s[None], *kv_s)
```

```python
import functools

import jax
import jax.numpy as jnp
import numpy as np
from jax import lax
from jax.experimental import pallas as pl
from jax.experimental.pallas import tpu as pltpu

F32 = jnp.float32
BF16 = jnp.bfloat16

D_MODEL = 1024
PAST_LEN = 8192
M_HEADS = 4
M_HEAD_DIM = 256
M_WIDTH = M_HEADS * M_HEAD_DIM
A_GROUPS = ((128, 1), (512, 4), (2048, 16))
A_HEADS_PER_GROUP = 4
A_HEAD_DIM = 128
A_GROUP_WIDTH = A_HEADS_PER_GROUP * A_HEAD_DIM
A_WIDTH = len(A_GROUPS) * A_GROUP_WIDTH
ROPE_THETA = 500000.0
ROPE_DIM = A_HEAD_DIM // 4
N_EXPERTS = 32
TOP_K = 4
D_FF = D_MODEL
SWIGLU_ALPHA = 1.702
SWIGLU_LIMIT = 7.0
PLE_DIM = 256
EPS = 1e-6

LANES = 128
VMEM_LIMIT = 56 * 1024 * 1024
NEG_INF = float("-inf")
PROJ_TM = 512


def _cparams(sem):
    return pltpu.CompilerParams(dimension_semantics=sem, vmem_limit_bytes=VMEM_LIMIT)


def _rms(x, gain):
    return x * lax.rsqrt(jnp.mean(x * x, axis=-1, keepdims=True) + EPS) * gain


def _sigmoid(x):
    return 1.0 / (1.0 + jnp.exp(-x))


def _log_sigmoid(x):
    return jnp.minimum(x, 0.0) - jnp.log(1.0 + jnp.exp(-jnp.abs(x)))


def _norm_proj_kernel(x_ref, g_ref, w_ref, *rest, n_rope_blocks, heads_per_block, has_extra, col_axis):
    rest = list(rest)
    if n_rope_blocks:
        cos_ref, sinm_ref, sinp_ref = rest[:3]
        rest = rest[3:]
    if has_extra:
        we_ref, o_ref, oe_ref = rest
    else:
        o_ref, = rest
    j = pl.program_id(col_axis)
    xn = _rms(x_ref[...], g_ref[...]).astype(BF16)
    z = jnp.dot(xn, w_ref[...], preferred_element_type=F32)
    if has_extra:
        oe_ref[...] = jnp.dot(xn, we_ref[...], preferred_element_type=F32)
    if n_rope_blocks:
        @pl.when(j < n_rope_blocks)
        def _():
            cosf, sinm, sinp = cos_ref[...], sinm_ref[...], sinp_ref[...]
            for h in range(heads_per_block):
                zh = z[:, h * LANES:(h + 1) * LANES]
                rot = (zh * cosf + pltpu.roll(zh, LANES - ROPE_DIM // 2, 1) * sinm
                       + pltpu.roll(zh, ROPE_DIM // 2, 1) * sinp)
                o_ref[:, h * LANES:(h + 1) * LANES] = rot.astype(o_ref.dtype)

        @pl.when(j >= n_rope_blocks)
        def _():
            o_ref[...] = z.astype(o_ref.dtype)
    else:
        o_ref[...] = z.astype(o_ref.dtype)


def norm_proj(x, gain, w, out_dtype, tn, rope=None, n_rope_blocks=0, rope_block=None, w_extra=None):
    n, d = x.shape
    tm = PROJ_TM
    ncol = w.shape[1]
    rows_outer = w_extra is not None
    grid = (n // tm, ncol // tn) if rows_outer else (ncol // tn, n // tm)

    def ix(fn):
        return (lambda i, j: fn(i, j)) if rows_outer else (lambda j, i: fn(i, j))

    in_specs = [pl.BlockSpec((tm, d), ix(lambda i, j: (i, 0))),
                pl.BlockSpec((1, d), ix(lambda i, j: (0, 0))),
                pl.BlockSpec((d, tn), ix(lambda i, j: (0, j)))]
    args = [x, gain.reshape(1, d), w]
    if n_rope_blocks:
        in_specs += [pl.BlockSpec((tm, LANES), ix(lambda i, j: (rope_block(i), 0)))] * 3
        args += list(rope)
    out_specs = pl.BlockSpec((tm, tn), ix(lambda i, j: (i, j)))
    out_shape = jax.ShapeDtypeStruct((n, ncol), out_dtype)
    if w_extra is not None:
        in_specs.append(pl.BlockSpec((d, LANES), ix(lambda i, j: (0, 0))))
        args.append(w_extra)
        out_specs = [out_specs, pl.BlockSpec((tm, LANES), ix(lambda i, j: (i, 0)))]
        out_shape = [out_shape, jax.ShapeDtypeStruct((n, LANES), F32)]
    return pl.pallas_call(
        functools.partial(_norm_proj_kernel, n_rope_blocks=n_rope_blocks, heads_per_block=tn // LANES,
                          has_extra=w_extra is not None, col_axis=1 if rows_outer else 0),
        grid=grid,
        in_specs=in_specs,
        out_specs=out_specs,
        out_shape=out_shape,
        compiler_params=_cparams(("arbitrary", "arbitrary")),
        name="norm_proj",
    )(*args)


def rope_tables(pos):
    half = ROPE_DIM // 2
    inv = (1.0 / (np.float32(ROPE_THETA) ** (np.arange(0, ROPE_DIM, 2, dtype=np.float32) / ROPE_DIM))).astype(np.float32)
    ang = (pos.astype(np.float32)[:, None] * inv[None, :]).astype(np.float64)
    cos, sin = np.cos(ang), np.sin(ang)
    n = pos.shape[0]
    ones = np.ones((n, LANES - ROPE_DIM))
    zeros = np.zeros((n, LANES - ROPE_DIM))
    zh = np.zeros((n, half))
    cosf = np.concatenate([cos, cos, ones], axis=1).astype(np.float32)
    sinm = np.concatenate([-sin, zh, zeros], axis=1).astype(np.float32)
    sinp = np.concatenate([zh, sin, zeros], axis=1).astype(np.float32)
    return jnp.asarray(cosf), jnp.asarray(sinm), jnp.asarray(sinp)


KV_ROWS = 512


def _kv_out_kernel(zk_ref, zv_ref, *outs, seq):
    s = pl.program_id(1)
    last = s == pl.num_programs(1) - 1
    for g, (win, _) in enumerate(A_GROUPS):
        keep = min(win, seq)
        for z_ref, o_ref in ((zk_ref, outs[2 * g]), (zv_ref, outs[2 * g + 1])):
            def write(z_ref=z_ref, o_ref=o_ref, rows=min(keep, KV_ROWS), g=g):
                for h in range(A_HEADS_PER_GROUP):
                    col = (g * A_HEADS_PER_GROUP + h) * LANES
                    o_ref[0, 0, :, h, :] = z_ref[KV_ROWS - rows:, col:col + LANES]

            if keep >= seq:
                write()
            else:
                assert keep <= KV_ROWS
                pl.when(last)(write)


def kv_out(zatt, batch, seq):
    steps = seq // KV_ROWS
    in_specs = [pl.BlockSpec((KV_ROWS, A_WIDTH), lambda b, s: (b * steps + s, 1)),
                pl.BlockSpec((KV_ROWS, A_WIDTH), lambda b, s: (b * steps + s, 2))]
    out_specs, out_shape = [], []
    for win, _ in A_GROUPS:
        keep = min(win, seq)
        rows = min(keep, KV_ROWS)
        idx = (lambda b, s: (0, b, s, 0, 0)) if keep >= seq else (lambda b, s: (0, b, 0, 0, 0))
        for _ in range(2):
            out_specs.append(pl.BlockSpec((1, 1, rows, A_HEADS_PER_GROUP, A_HEAD_DIM), idx))
            out_shape.append(jax.ShapeDtypeStruct((1, batch, keep, A_HEADS_PER_GROUP, A_HEAD_DIM), F32))
    return pl.pallas_call(
        functools.partial(_kv_out_kernel, seq=seq),
        grid=(batch, steps),
        in_specs=in_specs,
        out_specs=out_specs,
        out_shape=out_shape,
        compiler_params=_cparams(("arbitrary", "arbitrary")),
        name="kv_out",
    )(zatt, zatt)


def _mlstm_kernel(bias_ref, q_ref, k_ref, v_ref, o_ref, gi_ref, gfr_ref, gfc_ref, mn_ref, *rest,
                  chunk, n_chunks, valid_len, has_state):
    if has_state:
        c0_ref, n0_ref, m0_ref, h_ref, c_out, n_out, m_out, c_s, n_s, m_s = rest
    else:
        h_ref, c_out, n_out, m_out, c_s, n_s, m_s = rest
    c = pl.program_id(0) % n_chunks

    @pl.when(c == 0)
    def _():
        if has_state:
            c_s[...] = c0_ref[0]
            n_s[...] = n0_ref[0]
            m_s[...] = m0_ref[0]
        else:
            c_s[...] = jnp.zeros_like(c_s)
            n_s[...] = jnp.zeros_like(n_s)
            m_s[...] = jnp.zeros_like(m_s)

    last = c == n_chunks - 1
    for hd in range(M_HEADS):
        cols = slice(hd * M_HEAD_DIM, (hd + 1) * M_HEAD_DIM)
        _mlstm_head(hd, cols, last, bias_ref, q_ref, k_ref, v_ref, o_ref, gi_ref, gfr_ref, gfc_ref, mn_ref,
                    h_ref, c_out, n_out, m_out, c_s, n_s, m_s, chunk, valid_len)


def _mlstm_head(hd, cols, last, bias_ref, q_ref, k_ref, v_ref, o_ref, gi_ref, gfr_ref, gfc_ref, mn_ref,
                h_ref, c_out, n_out, m_out, c_s, n_s, m_s, chunk, valid_len):
    L = chunk
    q = q_ref[:, cols]
    k = k_ref[:, cols] * (M_HEAD_DIM ** -0.5)
    v = v_ref[:, cols]
    b_i = bias_ref[0, hd]
    b_f = bias_ref[1, hd]
    i_row = gi_ref[0, 0, hd:hd + 1, :] + b_i
    lf_row = _log_sigmoid(gfr_ref[0, 0, hd:hd + 1, :] + b_f)
    lf_col = _log_sigmoid(gfc_ref[0, 0, :, hd:hd + 1] + b_f)
    row_id = lax.broadcasted_iota(jnp.int32, (L, L), 0)
    col_id = lax.broadcasted_iota(jnp.int32, (L, L), 1)
    if valid_len < L:
        lane = lax.broadcasted_iota(jnp.int32, (1, L), 1)
        sub = lax.broadcasted_iota(jnp.int32, (L, 1), 0)
        i_row = jnp.where(lane < valid_len, i_row, NEG_INF)
        lf_row = jnp.where(lane < valid_len, lf_row, 0.0)
        lf_col = jnp.where(sub < valid_len, lf_col, 0.0)
    causal = col_id <= row_id
    b_col = jnp.sum(jnp.where(causal, lf_row, 0.0), axis=1, keepdims=True)
    b_row = jnp.sum(jnp.where(row_id <= col_id, lf_col, 0.0), axis=0, keepdims=True)
    m_prev = m_s[hd]
    dmat = jnp.where(causal, b_col - b_row + i_row, NEG_INF)
    inter = b_col + m_prev
    mj = jnp.maximum(inter, jnp.max(dmat, axis=1, keepdims=True))
    s = lax.dot_general(q, k, (((1,), (1,)), ((), ())), preferred_element_type=F32)
    sc = s * jnp.exp(dmat - mj)
    a_int = jnp.exp(inter - mj)
    c_prev = c_s[hd]
    n_prev = n_s[hd]
    qc = lax.dot_general(q, c_prev.astype(BF16), (((1,), (1,)), ((), ())), preferred_element_type=F32)
    num = jnp.dot(sc.astype(BF16), v, preferred_element_type=F32) + a_int * qc
    qn = jnp.sum(q.astype(F32) * n_prev, axis=1, keepdims=True)
    den = jnp.sum(sc, axis=1, keepdims=True) + a_int * qn
    h = num / jnp.maximum(jnp.abs(den), jnp.exp(-mj))
    h = h * _sigmoid(o_ref[:, cols].astype(F32))
    h = h * lax.rsqrt(jnp.mean(h * h, axis=-1, keepdims=True) + EPS) * mn_ref[:, cols]
    h_ref[:, cols] = h.astype(h_ref.dtype)

    bl = jnp.sum(lf_row, axis=1, keepdims=True)
    g_row = bl - b_row + i_row
    m_new = jnp.maximum(bl + m_prev, jnp.max(g_row, axis=1, keepdims=True))
    ws_row = jnp.exp(g_row - m_new)
    a_c = jnp.exp(bl + m_prev - m_new)
    vt = (v.astype(F32).T * ws_row).astype(BF16)
    c_new = a_c * c_prev + jnp.dot(vt, k, preferred_element_type=F32)
    ws8 = jnp.broadcast_to(ws_row, (8, L)).astype(BF16)
    n_new = a_c * n_prev + jnp.dot(ws8, k, preferred_element_type=F32)[0:1]
    c_s[hd] = c_new
    n_s[hd] = n_new
    m_s[hd] = m_new

    @pl.when(last)
    def _():
        c_out[0, hd] = c_new
        n_out[0, hd] = n_new
        m_out[0, hd] = m_new


def mlstm_part(zm, gates_row, gates_col, bias, m_norm, batch, seq, chunk, valid_len, row0, state=None):
    E = M_HEAD_DIM
    nc = seq // chunk
    blk0 = row0 // chunk

    def zspec(col):
        return pl.BlockSpec((chunk, M_WIDTH), lambda i: (blk0 + i, col))

    def per_chunk(shape):
        return pl.BlockSpec(shape, lambda i: (i // nc, i % nc, 0, 0))

    def per_row(shape):
        return pl.BlockSpec(shape, lambda i: (i // nc, 0, 0, 0))

    in_specs = [pl.BlockSpec(memory_space=pltpu.SMEM),
                zspec(0), zspec(1), zspec(2), zspec(3),
                per_chunk((1, 1, M_HEADS, chunk)), per_chunk((1, 1, M_HEADS, chunk)),
                per_chunk((1, 1, chunk, M_HEADS)),
                pl.BlockSpec((1, M_WIDTH), lambda i: (0, 0))]
    gi_row, gf_row = gates_row
    args = [bias, zm, zm, zm, zm, gi_row, gf_row, gates_col, m_norm.reshape(1, M_WIDTH)]
    state_shapes = [(1, M_HEADS, E, E), (1, M_HEADS, 1, E), (1, M_HEADS, 1, 1)]
    if state is not None:
        c0, n0, m0 = state
        in_specs += [per_row(s) for s in state_shapes]
        args += [c0, n0.reshape(batch, M_HEADS, 1, E), m0.reshape(batch, M_HEADS, 1, 1)]
    return dict(
        body=functools.partial(_mlstm_kernel, chunk=chunk, n_chunks=nc, valid_len=valid_len,
                               has_state=state is not None),
        steps=batch * nc,
        in_specs=in_specs,
        args=args,
        out_specs=[pl.BlockSpec((chunk, M_WIDTH), lambda i: (i, 0))] + [per_row(s) for s in state_shapes],
        out_shape=[jax.ShapeDtypeStruct((batch * seq, M_WIDTH), BF16)]
        + [jax.ShapeDtypeStruct((batch,) + s[1:], F32) for s in state_shapes],
        scratch=[pltpu.VMEM(s[1:], F32) for s in state_shapes],
    )


def fused_call(parts, name):
    steps = parts[0]["steps"]
    assert all(p["steps"] == steps for p in parts)
    n_in = [len(p["in_specs"]) for p in parts]
    n_out = [len(p["out_specs"]) for p in parts]
    n_scr = [len(p["scratch"]) for p in parts]

    def body(*refs):
        ins, outs, scr = refs[:sum(n_in)], refs[sum(n_in):sum(n_in) + sum(n_out)], refs[sum(n_in) + sum(n_out):]
        a = b = c = 0
        for p, na, nb, nc in zip(parts, n_in, n_out, n_scr):
            p["body"](*ins[a:a + na], *outs[b:b + nb], *scr[c:c + nc])
            a, b, c = a + na, b + nb, c + nc

    res = pl.pallas_call(
        body,
        grid=(steps,),
        in_specs=[s for p in parts for s in p["in_specs"]],
        out_specs=[s for p in parts for s in p["out_specs"]],
        out_shape=[s for p in parts for s in p["out_shape"]],
        scratch_shapes=[s for p in parts for s in p["scratch"]],
        compiler_params=_cparams(("arbitrary",)),
        name=name,
    )(*[a for p in parts for a in p["args"]])
    grouped, b = [], 0
    for nb in n_out:
        grouped.append(list(res[b:b + nb]))
        b += nb
    return grouped


ATT_BLK = 128
ATT_UNROLL = 4


def _band_block(qb, kcat, vcat, mask):
    s = lax.dot_general(qb, kcat, (((1,), (1,)), ((), ())), preferred_element_type=F32)
    s = jnp.where(mask, s, NEG_INF)
    mx = jnp.max(s, axis=1, keepdims=True)
    p = jnp.exp(s - mx)
    l = jnp.sum(p, axis=1, keepdims=True)
    o = jnp.dot(p.astype(BF16), vcat, preferred_element_type=F32) / l
    return o, mx + jnp.log(l)


def _attn_prompt_kernel(*refs, seq):
    qkv = refs[:9]
    y_ref = refs[9]
    o_scr = refs[10:13]
    lse_scr = refs[13:16]
    scale = A_HEAD_DIM ** -0.5
    qi = lax.broadcasted_iota(jnp.int32, (ATT_BLK, ATT_BLK), 0)
    ki = lax.broadcasted_iota(jnp.int32, (ATT_BLK, ATT_BLK), 1)
    cur_mask = ki <= qi
    prev_mask = ki >= qi
    band_mask = jnp.concatenate([prev_mask, cur_mask], axis=1)
    for g, (_, dil) in enumerate(A_GROUPS):
        q_ref, k_ref, v_ref = qkv[3 * g:3 * g + 3]
        L = seq // dil
        nb = L // ATT_BLK

        def residue(r, carry, q_ref=q_ref, k_ref=k_ref, v_ref=v_ref, dil=dil, L=L, nb=nb, g=g):
            def rows(first_blk, n_blk):
                if dil == 1:
                    return pl.ds(first_blk * ATT_BLK, n_blk * ATT_BLK)
                return pl.ds(r + first_blk * ATT_BLK * dil, n_blk * ATT_BLK, stride=dil)

            for n in range(nb):
                qb = (q_ref[rows(n, 1), :] * scale).astype(BF16)
                if n == 0:
                    kk, vv, mask = k_ref[rows(0, 1), :], v_ref[rows(0, 1), :], cur_mask
                else:
                    kk, vv, mask = k_ref[rows(n - 1, 2), :], v_ref[rows(n - 1, 2), :], band_mask
                o, lse = _band_block(qb, kk.astype(BF16), vv.astype(BF16), mask)
                o_scr[g][rows(n, 1), :] = o
                lse_scr[g][rows(n, 1), :] = lse
            return carry

        if dil == 1:
            residue(0, 0)
        else:
            lax.fori_loop(0, dil, residue, 0, unroll=ATT_UNROLL)
    l0, l1, l2 = lse_scr[0][...], lse_scr[1][...], lse_scr[2][...]
    mx = jnp.maximum(jnp.maximum(l0, l1), l2)
    w0, w1, w2 = jnp.exp(l0 - mx), jnp.exp(l1 - mx), jnp.exp(l2 - mx)
    y = (w0 * o_scr[0][...] + w1 * o_scr[1][...] + w2 * o_scr[2][...]) / (w0 + w1 + w2)
    y_ref[...] = y.astype(y_ref.dtype)


def attn_prompt(zatt, batch, seq):
    nh = A_WIDTH // LANES

    def spec(col0):
        return pl.BlockSpec((seq, LANES), lambda b, j: (b, col0 + j))

    in_specs, args = [], []
    for g in range(len(A_GROUPS)):
        for part in range(3):
            in_specs.append(spec(part * nh + g * A_HEADS_PER_GROUP))
            args.append(zatt)
    return pl.pallas_call(
        functools.partial(_attn_prompt_kernel, seq=seq),
        grid=(batch, A_HEADS_PER_GROUP),
        in_specs=in_specs,
        out_specs=pl.BlockSpec((seq, LANES), lambda b, j: (b, j)),
        out_shape=jax.ShapeDtypeStruct((batch * seq, A_GROUP_WIDTH), BF16),
        scratch_shapes=[pltpu.VMEM((seq, LANES), F32)] * 3 + [pltpu.VMEM((seq, 1), F32)] * 3,
        compiler_params=_cparams(("parallel", "parallel")),
        name="attn_prompt",
    )(*args)


def _window_kernel(new_ref, *refs, t_dec):
    n_buf = 2 * len(A_GROUPS)
    caches, news = refs[:n_buf], refs[n_buf:2 * n_buf]
    y_ref = refs[2 * n_buf]
    outs = refs[2 * n_buf + 1:3 * n_buf + 1]
    bufs = refs[3 * n_buf + 1:4 * n_buf + 1]
    sem_in, sem_out, sem_new = refs[4 * n_buf + 1:]
    b = pl.program_id(0)
    nb = pl.num_programs(0)
    slot = b % 2

    def copy_in(ci, row, sl):
        return pltpu.make_async_copy(caches[ci].at[0, row], bufs[ci].at[sl], sem_in.at[sl, ci])

    def copy_out(ci, row, sl):
        wb = bufs[ci].shape[1]
        return pltpu.make_async_copy(bufs[ci].at[sl, pl.ds(t_dec, wb - t_dec)],
                                     outs[ci].at[0, row, pl.ds(0, wb - t_dec)], sem_out.at[sl, ci])

    def copy_new(ci):
        wb = bufs[ci].shape[1]
        return pltpu.make_async_copy(news[ci], outs[ci].at[0, :, pl.ds(wb - t_dec, t_dec)], sem_new.at[ci])

    @pl.when(b == 0)
    def _():
        for ci in range(n_buf):
            copy_new(ci).start()
            copy_in(ci, 0, 0).start()

    for ci in range(n_buf):
        copy_in(ci, b, slot).wait()
        copy_out(ci, b, slot).start()

    @pl.when(b >= 1)
    def _():
        for ci in range(n_buf):
            copy_out(ci, b - 1, 1 - slot).wait()

    @pl.when(b + 1 < nb)
    def _():
        for ci in range(n_buf):
            copy_in(ci, b + 1, 1 - slot).start()

    _attn_sample(new_ref, bufs, slot, y_ref, t_dec)

    @pl.when(b == nb - 1)
    def _():
        for ci in range(n_buf):
            copy_out(ci, b, slot).wait()
            copy_new(ci).wait()


def _attn_sample(new_ref, bufs, slot, y_ref, t_dec):
    nh = A_WIDTH // LANES
    scale = A_HEAD_DIM ** -0.5
    jj = lax.broadcasted_iota(jnp.int32, (ATT_BLK, 1, 1), 0)
    for t in range(t_dec):
        outs, lses = [], []
        for g, (_, dil) in enumerate(A_GROUPS):
            h0 = g * A_HEADS_PER_GROUP
            q = new_ref[0, t, h0:h0 + A_HEADS_PER_GROUP, :] * scale
            rows = pl.ds(t % dil, ATT_BLK, stride=dil) if dil > 1 else pl.ds(0, ATT_BLK)
            kc = bufs[2 * g][slot, rows]
            vc = bufs[2 * g + 1][slot, rows]
            s_c = jnp.sum(kc * q[None], axis=-1, keepdims=True)
            if t // dil > 0:
                s_c = jnp.where(jj >= t // dil, s_c, NEG_INF)
            mx = jnp.max(s_c, axis=0)
            new_u = [u for u in range(t + 1) if (t - u) % dil == 0]
            s_new = []
            for u in new_u:
                k_u = new_ref[0, u, nh + h0:nh + h0 + A_HEADS_PER_GROUP, :]
                s_u = jnp.sum(k_u * q, axis=-1, keepdims=True)
                s_new.append(s_u)
                mx = jnp.maximum(mx, s_u)
            p_c = jnp.exp(s_c - mx[None])
            l = jnp.sum(p_c, axis=0)
            acc = jnp.sum(p_c * vc, axis=0)
            for u, s_u in zip(new_u, s_new):
                p_u = jnp.exp(s_u - mx)
                v_u = new_ref[0, u, 2 * nh + h0:2 * nh + h0 + A_HEADS_PER_GROUP, :]
                l = l + p_u
                acc = acc + p_u * v_u
            outs.append(acc / l)
            lses.append(mx + jnp.log(l))
        mxg = jnp.maximum(jnp.maximum(lses[0], lses[1]), lses[2])
        ws = [jnp.exp(l_g - mxg) for l_g in lses]
        y = (ws[0] * outs[0] + ws[1] * outs[1] + ws[2] * outs[2]) / (ws[0] + ws[1] + ws[2])
        y_ref[0, t] = y.astype(y_ref.dtype)


def window_part(new_qkv, caches, news):
    batch, t_dec = new_qkv.shape[:2]
    for (win, dil), k_buf in zip(A_GROUPS, caches[::2]):
        assert win == ATT_BLK * dil and k_buf.shape[2] == win
    any_spec = pl.BlockSpec(memory_space=pl.ANY)
    n_buf = len(caches)
    return dict(
        body=functools.partial(_window_kernel, t_dec=t_dec),
        steps=batch,
        in_specs=[pl.BlockSpec((1, t_dec, new_qkv.shape[2], LANES), lambda b: (b, 0, 0, 0))]
        + [any_spec] * (2 * n_buf),
        args=[new_qkv, *caches, *news],
        out_specs=[pl.BlockSpec((1, t_dec, A_HEADS_PER_GROUP, LANES), lambda b: (b, 0, 0, 0))] + [any_spec] * n_buf,
        out_shape=[jax.ShapeDtypeStruct((batch, t_dec, A_HEADS_PER_GROUP, LANES), BF16)]
        + [jax.ShapeDtypeStruct(c.shape, c.dtype) for c in caches],
        scratch=[pltpu.VMEM((2,) + c.shape[2:], c.dtype) for c in caches]
        + [pltpu.SemaphoreType.DMA((2, n_buf)), pltpu.SemaphoreType.DMA((2, n_buf)),
           pltpu.SemaphoreType.DMA((n_buf,))],
    )


MOE_TM = 256
MOE_CHUNK = 8
MOE_BM = 256
MOE_TILE_ROWS = TOP_K * MOE_TM + N_EXPERTS * MOE_CHUNK
ROUTE_EXPERT, ROUTE_POS, ROUTE_GATE = 0, TOP_K, 2 * TOP_K


def _merge_route_kernel(h_ref, ha_ref, hb_ref, ga_ref, gb_ref, wpa_ref, wpb_ref, wo_ref, g2_ref,
                        wrh_ref, wrl_ref, br_ref, h1_ref, xt_ref, route_ref, cnt_ref):
    tm = h_ref.shape[0]
    rows = xt_ref.shape[0]
    a = jnp.dot(ha_ref[...], wpa_ref[...], preferred_element_type=F32)
    b = jnp.dot(hb_ref[...], wpb_ref[...], preferred_element_type=F32)
    u = _sigmoid(ga_ref[...].astype(F32)) * a + _sigmoid(gb_ref[...].astype(F32)) * b
    h1 = h_ref[...] + jnp.dot(u.astype(BF16), wo_ref[...], preferred_element_type=F32)
    h1_ref[...] = h1
    xn = _rms(h1, g2_ref[...])
    x_hi = xn.astype(BF16)
    x_lo = (xn - x_hi.astype(F32)).astype(BF16)
    logits = (jnp.dot(x_hi, wrh_ref[...], preferred_element_type=F32)
              + jnp.dot(x_lo, wrh_ref[...], preferred_element_type=F32)
              + jnp.dot(x_hi, wrl_ref[...], preferred_element_type=F32)) + br_ref[...]
    lane = lax.broadcasted_iota(jnp.int32, (tm, LANES), 1)
    logits = jnp.where(lane < N_EXPERTS, logits, NEG_INF)
    vals, hots = [], []
    work = logits
    for _ in range(TOP_K):
        mx = jnp.max(work, axis=1, keepdims=True)
        idx = jnp.min(jnp.where(work == mx, lane, LANES), axis=1, keepdims=True)
        hot = lane == idx
        work = jnp.where(hot, NEG_INF, work)
        vals.append(mx)
        hots.append(hot)
    exps = [jnp.exp(v - vals[0]) for v in vals]
    denom = exps[0] + exps[1] + exps[2] + exps[3]
    hot_f = jnp.zeros((tm, LANES), F32)
    for hot in hots:
        hot_f = hot_f + jnp.where(hot, 1.0, 0.0)
    r_id = lax.broadcasted_iota(jnp.int32, (tm, tm), 0)
    c_id = lax.broadcasted_iota(jnp.int32, (tm, tm), 1)
    earlier = jnp.where(c_id < r_id, 1.0, 0.0).astype(BF16)
    rank = jnp.dot(earlier, hot_f.astype(BF16), preferred_element_type=F32)
    cnt = jnp.sum(hot_f, axis=0, keepdims=True)
    padded = jnp.floor((cnt + (MOE_CHUNK - 1)) * (1.0 / MOE_CHUNK)) * MOE_CHUNK
    la = lax.broadcasted_iota(jnp.int32, (LANES, LANES), 0)
    lb = lax.broadcasted_iota(jnp.int32, (LANES, LANES), 1)
    before = jnp.where(la < lb, 1.0, 0.0).astype(BF16)
    run_start = jnp.dot(jnp.broadcast_to(padded, (8, LANES)).astype(BF16), before,
                        preferred_element_type=F32)[0:1]
    pos_all = run_start + rank
    lane_f = lane.astype(F32)
    row_id = lax.broadcasted_iota(jnp.int32, (tm, rows), 1)
    place = jnp.zeros((tm, rows), F32)
    route = jnp.zeros((tm, LANES), F32)
    for k in range(TOP_K):
        e_k = jnp.sum(jnp.where(hots[k], lane_f, 0.0), axis=1, keepdims=True)
        p_k = jnp.sum(jnp.where(hots[k], pos_all, 0.0), axis=1, keepdims=True)
        place = place + jnp.where(row_id == p_k.astype(jnp.int32), 1.0, 0.0)
        route = jnp.where(lane == ROUTE_EXPERT + k, e_k, route)
        route = jnp.where(lane == ROUTE_POS + k, p_k, route)
        route = jnp.where(lane == ROUTE_GATE + k, exps[k] / denom, route)
    route_ref[...] = route
    cnt_ref[0] = cnt
    xt_ref[...] = lax.dot_general(place.astype(BF16), x_hi, (((0,), (0,)), ((), ())),
                                  preferred_element_type=F32)


def merge_route(h, ha, hb, zmg, w_pa, w_pb, w_o, norm2, w_r_hi, w_r_lo, b_r):
    n, d = h.shape
    tm = MOE_TM
    nt = n // tm

    def full(arr):
        return pl.BlockSpec(arr.shape, lambda i: (0,) * arr.ndim)

    weights = [w_pa, w_pb, w_o, norm2.reshape(1, d), w_r_hi, w_r_lo, b_r]
    return pl.pallas_call(
        _merge_route_kernel,
        grid=(nt,),
        in_specs=[pl.BlockSpec((tm, d), lambda i: (i, 0)),
                  pl.BlockSpec((tm, M_WIDTH), lambda i: (i, 0)),
                  pl.BlockSpec((tm, A_GROUP_WIDTH), lambda i: (i, 0)),
                  pl.BlockSpec((tm, d), lambda i: (i, 0)),
                  pl.BlockSpec((tm, d), lambda i: (i, 1))] + [full(wt) for wt in weights],
        out_specs=[pl.BlockSpec((tm, d), lambda i: (i, 0)),
                   pl.BlockSpec((MOE_TILE_ROWS, d), lambda i: (i, 0)),
                   pl.BlockSpec((tm, LANES), lambda i: (i, 0)),
                   pl.BlockSpec((1, 1, LANES), lambda i: (i, 0, 0))],
        out_shape=[jax.ShapeDtypeStruct((n, d), F32),
                   jax.ShapeDtypeStruct((nt * MOE_TILE_ROWS, d), F32),
                   jax.ShapeDtypeStruct((n, LANES), F32),
                   jax.ShapeDtypeStruct((nt, 1, LANES), F32)],
        compiler_params=_cparams(("parallel",)),
        name="merge_route",
    )(h, ha, hb, zmg, zmg, *weights)


def moe_plan(cnt, n_blocks):
    nt = cnt.shape[0]
    cpb = MOE_BM // MOE_CHUNK
    cpt = MOE_TILE_ROWS // MOE_CHUNK
    i32 = jnp.int32
    cnt = cnt[:, 0, :N_EXPERTS].astype(i32)
    nch = (cnt + MOE_CHUNK - 1) // MOE_CHUNK
    off_incl = jnp.cumsum(nch, axis=1)
    off_ch = off_incl - nch
    seq_incl = jnp.cumsum(nch, axis=0)
    base_ch = seq_incl - nch
    tot = seq_incl[-1]
    nb = (tot + cpb - 1) // cpb
    bend = jnp.cumsum(nb)
    bstart = bend - nb
    n_used = bend[-1:]
    blk = jnp.arange(n_blocks, dtype=i32)
    blk_exp = jnp.minimum(jnp.sum(blk[:, None] >= bend[None, :], axis=1), N_EXPERTS - 1).astype(i32)
    experts = jnp.arange(N_EXPERTS, dtype=i32)
    tiles = jnp.arange(nt, dtype=i32)

    def pick(hot, table):
        return jnp.sum(jnp.where(hot, table, 0), axis=-1).astype(i32)

    hot_e = blk_exp[:, None] == experts[None, :]
    seq_b, off_b, base_b = (pick(hot_e[:, None, :], t[None]) for t in (seq_incl, off_ch, base_ch))
    q = (blk - pick(hot_e, bstart[None]))[:, None] * cpb + jnp.arange(cpb, dtype=i32)[None, :]
    tile = jnp.minimum(jnp.sum(q[:, :, None] >= seq_b[:, None, :], axis=2), nt - 1).astype(i32)
    hot_t = tile[:, :, None] == tiles[None, None, :]
    src = (tile * cpt + pick(hot_t, off_b[:, None, :]) + q - pick(hot_t, base_b[:, None, :])) * MOE_CHUNK
    src = jnp.where(q < pick(hot_e, tot[None])[:, None], src, 0).astype(i32)
    slot = jnp.arange(cpt, dtype=i32)
    e_s = jnp.sum(slot[None, :, None] >= off_incl[:, None, :], axis=2).astype(i32)
    hot_s = e_s[:, :, None] == experts[None, None, :]
    back = (pick(hot_s, bstart[None, None, :]) * cpb + pick(hot_s, base_ch[:, None, :])
            + slot[None, :] - pick(hot_s, off_ch[:, None, :])) * MOE_CHUNK
    back = jnp.where(e_s < N_EXPERTS, back, 0).astype(i32)
    return blk_exp, n_used.astype(i32), src.reshape(-1), back.reshape(-1)


def _swiglu(h):
    x_glu = jnp.minimum(h[:, :D_FF], SWIGLU_LIMIT)
    x_lin = jnp.clip(h[:, D_FF:], -SWIGLU_LIMIT, SWIGLU_LIMIT)
    return x_glu * _sigmoid(SWIGLU_ALPHA * x_glu) * (x_lin + 1.0)


def _chunk_gather_start(src_hbm, rows_ref, first, dst_ref, sem):
    for c in range(dst_ref.shape[0] // MOE_CHUNK):
        r = pl.multiple_of(rows_ref[first + c], MOE_CHUNK)
        pltpu.make_async_copy(src_hbm.at[pl.ds(r, MOE_CHUNK)],
                              dst_ref.at[pl.ds(c * MOE_CHUNK, MOE_CHUNK)], sem).start()


def _chunk_gather_wait(src_hbm, dst_ref, sem):
    pltpu.make_async_copy(src_hbm.at[pl.ds(0, dst_ref.shape[0])], dst_ref, sem).wait()


def _moe_kernel(blk_exp_ref, n_used_ref, rows_ref, xt_hbm, w1_ref, b1_ref, w2_ref, b2_ref, y_ref,
                xbuf, w1b, w2b, sems):
    i = pl.program_id(0)
    n_used = n_used_ref[0]
    cpb = xbuf.shape[1] // MOE_CHUNK

    def fetch(blk, slot):
        _chunk_gather_start(xt_hbm, rows_ref, blk * cpb, xbuf.at[slot], sems.at[slot])

    @pl.when(jnp.logical_and(i == 0, n_used > 0))
    def _():
        fetch(0, 0)

    @pl.when(i < n_used)
    def _():
        slot = i % 2
        fetch(jnp.minimum(i + 1, n_used - 1), 1 - slot)
        _chunk_gather_wait(xt_hbm, xbuf.at[slot], sems.at[slot])
        changed = jnp.logical_or(i == 0, blk_exp_ref[i] != blk_exp_ref[jnp.maximum(i - 1, 0)])

        @pl.when(changed)
        def _():
            w1b[...] = w1_ref[0].astype(BF16)
            w2b[...] = w2_ref[0].astype(BF16)

        h = jnp.dot(xbuf[slot].astype(BF16), w1b[...], preferred_element_type=F32) + b1_ref[0]
        act = _swiglu(h)
        y_ref[...] = jnp.dot(act.astype(BF16), w2b[...], preferred_element_type=F32) + b2_ref[0]

        @pl.when(i == n_used - 1)
        def _():
            _chunk_gather_wait(xt_hbm, xbuf.at[1 - slot], sems.at[1 - slot])

    @pl.when(i >= n_used)
    def _():
        y_ref[...] = jnp.zeros_like(y_ref)


def moe_experts(xt, blk_exp, n_used, chunk_rows, w1, b1, w2, b2):
    d = xt.shape[1]
    nblk = blk_exp.shape[0]
    ne = w1.shape[0]
    bm = MOE_BM
    grid_spec = pltpu.PrefetchScalarGridSpec(
        num_scalar_prefetch=3,
        grid=(nblk,),
        in_specs=[pl.BlockSpec(memory_space=pl.ANY),
                  pl.BlockSpec((1, d, 2 * D_FF), lambda i, be, nu, cr: (be[i], 0, 0)),
                  pl.BlockSpec((1, 1, 2 * D_FF), lambda i, be, nu, cr: (be[i], 0, 0)),
                  pl.BlockSpec((1, D_FF, d), lambda i, be, nu, cr: (be[i], 0, 0)),
                  pl.BlockSpec((1, 1, d), lambda i, be, nu, cr: (be[i], 0, 0))],
        out_specs=pl.BlockSpec((bm, d), lambda i, be, nu, cr: (i, 0)),
        scratch_shapes=[pltpu.VMEM((2, bm, d), F32), pltpu.VMEM((d, 2 * D_FF), BF16),
                        pltpu.VMEM((D_FF, d), BF16), pltpu.SemaphoreType.DMA((2,))],
    )
    return pl.pallas_call(
        _moe_kernel,
        grid_spec=grid_spec,
        out_shape=jax.ShapeDtypeStruct((nblk * bm, d), F32),
        compiler_params=_cparams(("arbitrary",)),
        name="moe_experts",
    )(blk_exp, n_used, chunk_rows, xt, w1, b1.reshape(ne, 1, -1), w2, b2.reshape(ne, 1, -1))


def _combine_kernel(rows_ref, ys_hbm, h1_ref, route_ref, p_ref, wple_ref, wpg_ref, g3_ref, gf_ref, y_ref,
                    ybuf, sems, *, tile0):
    i = pl.program_id(0)
    n = pl.num_programs(0)
    tm = h1_ref.shape[0]
    rows = ybuf.shape[1]
    cpt = rows // MOE_CHUNK

    def fetch(step, slot):
        _chunk_gather_start(ys_hbm, rows_ref, (tile0 + step) * cpt, ybuf.at[slot], sems.at[slot])

    @pl.when(i == 0)
    def _():
        fetch(0, 0)

    slot = i % 2
    fetch(jnp.minimum(i + 1, n - 1), 1 - slot)
    _chunk_gather_wait(ys_hbm, ybuf.at[slot], sems.at[slot])
    route = route_ref[...]
    row_id = lax.broadcasted_iota(jnp.int32, (tm, rows), 1)
    weight = jnp.zeros((tm, rows), F32)
    for k in range(TOP_K):
        p_k = route[:, ROUTE_POS + k:ROUTE_POS + k + 1].astype(jnp.int32)
        weight = weight + jnp.where(row_id == p_k, route[:, ROUTE_GATE + k:ROUTE_GATE + k + 1], 0.0)
    h2 = h1_ref[...] + jnp.dot(weight.astype(BF16), ybuf[slot].astype(BF16), preferred_element_type=F32)
    ple = jnp.dot(p_ref[...].astype(BF16), wple_ref[...], preferred_element_type=F32)
    gate = _sigmoid(jnp.dot(_rms(h2, g3_ref[...]).astype(BF16), wpg_ref[...], preferred_element_type=F32))
    h3 = h2 + ple * gate
    y_ref[...] = _rms(h3, gf_ref[...])

    @pl.when(i == n - 1)
    def _():
        _chunk_gather_wait(ys_hbm, ybuf.at[1 - slot], sems.at[1 - slot])


def moe_combine(ys, back_rows, h1, route, p, w_ple, w_pg, norm3, norm_f, tile0, n_tiles):
    d = h1.shape[1]
    tm = MOE_TM

    def full(arr):
        return pl.BlockSpec(arr.shape, lambda i, br: (0,) * arr.ndim)

    weights = [w_ple, w_pg, norm3.reshape(1, d), norm_f.reshape(1, d)]
    grid_spec = pltpu.PrefetchScalarGridSpec(
        num_scalar_prefetch=1,
        grid=(n_tiles,),
        in_specs=[pl.BlockSpec(memory_space=pl.ANY),
                  pl.BlockSpec((tm, d), lambda i, br: (tile0 + i, 0)),
                  pl.BlockSpec((tm, LANES), lambda i, br: (tile0 + i, 0)),
                  pl.BlockSpec((tm, PLE_DIM), lambda i, br: (i, 0))] + [full(wt) for wt in weights],
        out_specs=pl.BlockSpec((tm, d), lambda i, br: (i, 0)),
        scratch_shapes=[pltpu.VMEM((2, MOE_TILE_ROWS, d), F32), pltpu.SemaphoreType.DMA((2,))],
    )
    return pl.pallas_call(
        functools.partial(_combine_kernel, tile0=tile0),
        grid_spec=grid_spec,
        out_shape=jax.ShapeDtypeStruct((n_tiles * tm, d), F32),
        compiler_params=_cparams(("arbitrary",)),
        name="moe_combine",
    )(back_rows, ys, h1, route, p, *weights)


def _gate_layouts(zg, batch, seq, chunk):
    nc = seq // chunk
    g = zg[:, :2 * M_HEADS].reshape(batch, nc, chunk, 2, M_HEADS)
    rows = jnp.transpose(g, (3, 0, 1, 4, 2))
    return (rows[0], rows[1]), g[:, :, :, 1, :]


def kernel(x_prompt, x_sample, state_C, state_n, state_m, cache_k0, cache_v0, cache_k1, cache_v1, cache_k2, cache_v2, p_prompt, p_sample, norm1, w_in, b_igate, b_fgate, m_norm, w_pa, w_pb, w_o, norm2, w_router, b_router, w1, b1, w2, b2, norm3, w_ple, w_ple_gate, norm_f):
    bp, seq, d = x_prompt.shape
    bs, t_dec, _ = x_sample.shape
    n_p, n_s = bp * seq, bs * t_dec
    n_all = n_p + n_s
    x_all = jnp.concatenate([x_prompt.reshape(n_p, d), x_sample.reshape(n_s, d)], axis=0)

    w = w_in[0]
    c_gate = 4 * M_WIDTH
    c_att = c_gate + 2 * M_HEADS
    c_mg = c_att + 3 * A_WIDTH
    w_m = w[:, :c_gate].astype(BF16)
    w_gate = jnp.pad(w[:, c_gate:c_att], ((0, 0), (0, LANES - 2 * M_HEADS))).astype(BF16)
    w_att = w[:, c_att:c_mg].astype(BF16)
    w_mg = w[:, c_mg:].astype(BF16)
    rope = rope_tables(np.concatenate([np.arange(seq), np.tile(np.arange(t_dec) + PAST_LEN, bs)]))
    tiles_p, tiles_seq = n_p // PROJ_TM, seq // PROJ_TM

    def rope_block(i):
        return jnp.where(i < tiles_p, i % tiles_seq, tiles_seq + i - tiles_p)

    zm, zgate = norm_proj(x_all, norm1[0], w_m, BF16, 2048, w_extra=w_gate)
    zatt = norm_proj(x_all, norm1[0], w_att, F32, A_WIDTH, rope=rope, n_rope_blocks=2, rope_block=rope_block)
    zmg = norm_proj(x_all, norm1[0], w_mg, BF16, 2048)

    kv_p = kv_out(zatt, bp, seq)

    z_new = zatt[n_p:]
    caches = (cache_k0, cache_v0, cache_k1, cache_v1, cache_k2, cache_v2)
    news = []
    for g in range(len(A_GROUPS)):
        for part in range(2):
            col = (1 + part) * A_WIDTH + g * A_GROUP_WIDTH
            news.append(z_new[:, col:col + A_GROUP_WIDTH].reshape(bs, t_dec, A_HEADS_PER_GROUP, A_HEAD_DIM))
    bias = jnp.stack([b_igate[0], b_fgate[0]])
    chunk = 128
    grow_p, gcol_p = _gate_layouts(zgate[:n_p], bp, seq, chunk)
    t_pad = 8
    zm_s = jnp.pad(zm[n_p:].reshape(bs, t_dec, -1), ((0, 0), (0, t_pad - t_dec), (0, 0))).reshape(bs * t_pad, -1)
    zg_s = jnp.pad(zgate[n_p:].reshape(bs, t_dec, -1), ((0, 0), (0, t_pad - t_dec), (0, 0))).reshape(bs * t_pad, -1)
    grow_s, gcol_s = _gate_layouts(zg_s, bs, t_pad, t_pad)
    (hb_s, *kv_s), (ha_p, c_p, nn_p, m_p), (ha_s, c_s, nn_s, m_s) = fused_call([
        window_part(z_new.reshape(bs, t_dec, 3 * A_WIDTH // LANES, LANES), caches, news),
        mlstm_part(zm, grow_p, gcol_p, bias, m_norm[0], bp, seq, chunk, chunk, 0),
        mlstm_part(zm_s, grow_s, gcol_s, bias, m_norm[0], bs, t_pad, t_pad, t_dec, 0,
                   state=(state_C[0], state_n[0], state_m[0])),
    ], "window_mlstm")
    nn_p, nn_s = nn_p.reshape(bp, M_HEADS, M_HEAD_DIM), nn_s.reshape(bs, M_HEADS, M_HEAD_DIM)
    m_p, m_s = m_p.reshape(bp, M_HEADS), m_s.reshape(bs, M_HEADS)
    hb_p = attn_prompt(zatt, bp, seq)
    ha_all = jnp.concatenate([ha_p, ha_s.reshape(bs, t_pad, M_WIDTH)[:, :t_dec].reshape(n_s, M_WIDTH)], axis=0)
    hb_all = jnp.concatenate([hb_p, hb_s.reshape(n_s, A_GROUP_WIDTH)], axis=0)

    w_r = jnp.pad(w_router[0], ((0, 0), (0, LANES - N_EXPERTS)))
    w_r_hi = w_r.astype(BF16)
    w_r_lo = (w_r - w_r_hi.astype(F32)).astype(BF16)
    b_r = jnp.pad(b_router[0], (0, LANES - N_EXPERTS)).reshape(1, LANES)
    h1, xt, route, cnt = merge_route(x_all, ha_all, hb_all, zmg, w_pa[0].astype(BF16), w_pb[0].astype(BF16),
                                     w_o[0].astype(BF16), norm2[0], w_r_hi, w_r_lo, b_r)
    nt = n_all // MOE_TM
    n_blocks = nt * MOE_TILE_ROWS // MOE_BM + N_EXPERTS + 1
    blk_exp, n_used, src_rows, back_rows = moe_plan(cnt, n_blocks)
    ys = moe_experts(xt, blk_exp, n_used, src_rows, w1[0], b1[0], w2[0], b2[0])
    tail_w = (w_ple[0].astype(BF16), w_ple_gate[0].astype(BF16), norm3[0], norm_f)
    nt_p = n_p // MOE_TM
    y_p = moe_combine(ys, back_rows, h1, route, p_prompt[0].reshape(n_p, PLE_DIM), *tail_w, 0, nt_p)
    y_s = moe_combine(ys, back_rows, h1, route, p_sample[0].reshape(n_s, PLE_DIM), *tail_w, nt_p, nt - nt_p)

    return (y_p.reshape(bp, seq, d), y_s.reshape(bs, t_dec, d),
            c_p[None], nn_p[None], m_p[None], *kv_p,
            c_s[None], nn_s[None], m_s[None], *kv_s)
```

```python
import functools

import jax
import jax.numpy as jnp
import numpy as np
from jax import lax
from jax.experimental import pallas as pl
from jax.experimental.pallas import tpu as pltpu

F32 = jnp.float32
BF16 = jnp.bfloat16

D_MODEL = 1024
PAST_LEN = 8192
M_HEADS = 4
M_HEAD_DIM = 256
M_WIDTH = M_HEADS * M_HEAD_DIM
A_GROUPS = ((128, 1), (512, 4), (2048, 16))
A_HEADS_PER_GROUP = 4
A_HEAD_DIM = 128
A_GROUP_WIDTH = A_HEADS_PER_GROUP * A_HEAD_DIM
A_WIDTH = len(A_GROUPS) * A_GROUP_WIDTH
ROPE_THETA = 500000.0
ROPE_DIM = A_HEAD_DIM // 4
N_EXPERTS = 32
TOP_K = 4
D_FF = D_MODEL
SWIGLU_ALPHA = 1.702
SWIGLU_LIMIT = 7.0
PLE_DIM = 256
EPS = 1e-6

LANES = 128
VMEM_LIMIT = 56 * 1024 * 1024
FUSED_VMEM_LIMIT = 60 * 1024 * 1024
NEG_INF = float("-inf")
PROJ_TM = 512


def _cparams(sem, vmem_limit=VMEM_LIMIT):
    return pltpu.CompilerParams(dimension_semantics=sem, vmem_limit_bytes=vmem_limit)


def _rms(x, gain):
    return x * lax.rsqrt(jnp.mean(x * x, axis=-1, keepdims=True) + EPS) * gain


def _sigmoid(x):
    return 1.0 / (1.0 + jnp.exp(-x))


def _log_sigmoid(x):
    return jnp.minimum(x, 0.0) - jnp.log(1.0 + jnp.exp(-jnp.abs(x)))


def _two_source_specs(block, tiles_a, ix=lambda fn: fn):
    return [pl.BlockSpec(block, ix(lambda i, *_: (jnp.minimum(i, tiles_a - 1), 0))),
            pl.BlockSpec(block, ix(lambda i, *_: (jnp.maximum(i - tiles_a, 0), 0)))]


def _norm_proj_kernel(xa_ref, xb_ref, g_ref, w_ref, *rest, n_rope_blocks, heads_per_block, has_extra, col_axis,
                      tiles_a):
    rest = list(rest)
    if n_rope_blocks:
        cos_ref, sinm_ref, sinp_ref = rest[:3]
        rest = rest[3:]
    if has_extra:
        we_ref, o_ref, oe_ref = rest
    else:
        o_ref, = rest
    j = pl.program_id(col_axis)
    x = jnp.where(pl.program_id(1 - col_axis) < tiles_a, xa_ref[...], xb_ref[...])
    xn = _rms(x, g_ref[...]).astype(BF16)
    z = jnp.dot(xn, w_ref[...], preferred_element_type=F32)
    if has_extra:
        oe_ref[...] = jnp.dot(xn, we_ref[...], preferred_element_type=F32)
    if n_rope_blocks:
        @pl.when(j < n_rope_blocks)
        def _():
            cosf, sinm, sinp = cos_ref[...], sinm_ref[...], sinp_ref[...]
            for h in range(heads_per_block):
                zh = z[:, h * LANES:(h + 1) * LANES]
                rot = (zh * cosf + pltpu.roll(zh, LANES - ROPE_DIM // 2, 1) * sinm
                       + pltpu.roll(zh, ROPE_DIM // 2, 1) * sinp)
                o_ref[:, h * LANES:(h + 1) * LANES] = rot.astype(o_ref.dtype)

        @pl.when(j >= n_rope_blocks)
        def _():
            o_ref[...] = z.astype(o_ref.dtype)
    else:
        o_ref[...] = z.astype(o_ref.dtype)


def norm_proj(xa, xb, gain, w, out_dtype, tn, rope=None, n_rope_blocks=0, rope_block=None, w_extra=None):
    d = xa.shape[1]
    tm = PROJ_TM
    tiles_a = xa.shape[0] // tm
    n = xa.shape[0] + xb.shape[0]
    ncol = w.shape[1]
    rows_outer = w_extra is not None
    grid = (n // tm, ncol // tn) if rows_outer else (ncol // tn, n // tm)

    def ix(fn):
        return (lambda i, j: fn(i, j)) if rows_outer else (lambda j, i: fn(i, j))

    in_specs = _two_source_specs((tm, d), tiles_a, ix) + [
        pl.BlockSpec((1, d), ix(lambda i, j: (0, 0))),
        pl.BlockSpec((d, tn), ix(lambda i, j: (0, j)))]
    args = [xa, xb, gain.reshape(1, d), w]
    if n_rope_blocks:
        in_specs += [pl.BlockSpec((tm, LANES), ix(lambda i, j: (rope_block(i), 0)))] * 3
        args += list(rope)
    out_specs = pl.BlockSpec((tm, tn), ix(lambda i, j: (i, j)))
    out_shape = jax.ShapeDtypeStruct((n, ncol), out_dtype)
    if w_extra is not None:
        in_specs.append(pl.BlockSpec((d, LANES), ix(lambda i, j: (0, 0))))
        args.append(w_extra)
        out_specs = [out_specs, pl.BlockSpec((tm, LANES), ix(lambda i, j: (i, 0)))]
        out_shape = [out_shape, jax.ShapeDtypeStruct((n, LANES), F32)]
    return pl.pallas_call(
        functools.partial(_norm_proj_kernel, n_rope_blocks=n_rope_blocks, heads_per_block=tn // LANES,
                          has_extra=w_extra is not None, col_axis=1 if rows_outer else 0, tiles_a=tiles_a),
        grid=grid,
        in_specs=in_specs,
        out_specs=out_specs,
        out_shape=out_shape,
        compiler_params=_cparams(("arbitrary", "arbitrary")),
        name="norm_proj",
    )(*args)


def rope_tables(pos):
    half = ROPE_DIM // 2
    inv = (1.0 / (np.float32(ROPE_THETA) ** (np.arange(0, ROPE_DIM, 2, dtype=np.float32) / ROPE_DIM))).astype(np.float32)
    ang = (pos.astype(np.float32)[:, None] * inv[None, :]).astype(np.float64)
    cos, sin = np.cos(ang), np.sin(ang)
    n = pos.shape[0]
    ones = np.ones((n, LANES - ROPE_DIM))
    zeros = np.zeros((n, LANES - ROPE_DIM))
    zh = np.zeros((n, half))
    cosf = np.concatenate([cos, cos, ones], axis=1).astype(np.float32)
    sinm = np.concatenate([-sin, zh, zeros], axis=1).astype(np.float32)
    sinp = np.concatenate([zh, sin, zeros], axis=1).astype(np.float32)
    return jnp.asarray(cosf), jnp.asarray(sinm), jnp.asarray(sinp)


KV_ROWS = 512


def _kv_out_kernel(zk_ref, zv_ref, *outs, seq):
    s = pl.program_id(1)
    last = s == pl.num_programs(1) - 1
    for g, (win, _) in enumerate(A_GROUPS):
        keep = min(win, seq)
        for z_ref, o_ref in ((zk_ref, outs[2 * g]), (zv_ref, outs[2 * g + 1])):
            def write(z_ref=z_ref, o_ref=o_ref, rows=min(keep, KV_ROWS), g=g):
                for h in range(A_HEADS_PER_GROUP):
                    col = (g * A_HEADS_PER_GROUP + h) * LANES
                    o_ref[0, 0, :, h, :] = z_ref[KV_ROWS - rows:, col:col + LANES]

            if keep >= seq:
                write()
            else:
                assert keep <= KV_ROWS
                pl.when(last)(write)


def kv_out(zatt, batch, seq):
    steps = seq // KV_ROWS
    in_specs = [pl.BlockSpec((KV_ROWS, A_WIDTH), lambda b, s: (b * steps + s, 1)),
                pl.BlockSpec((KV_ROWS, A_WIDTH), lambda b, s: (b * steps + s, 2))]
    out_specs, out_shape = [], []
    for win, _ in A_GROUPS:
        keep = min(win, seq)
        rows = min(keep, KV_ROWS)
        idx = (lambda b, s: (0, b, s, 0, 0)) if keep >= seq else (lambda b, s: (0, b, 0, 0, 0))
        for _ in range(2):
            out_specs.append(pl.BlockSpec((1, 1, rows, A_HEADS_PER_GROUP, A_HEAD_DIM), idx))
            out_shape.append(jax.ShapeDtypeStruct((1, batch, keep, A_HEADS_PER_GROUP, A_HEAD_DIM), F32))
    return pl.pallas_call(
        functools.partial(_kv_out_kernel, seq=seq),
        grid=(batch, steps),
        in_specs=in_specs,
        out_specs=out_specs,
        out_shape=out_shape,
        compiler_params=_cparams(("arbitrary", "arbitrary")),
        name="kv_out",
    )(zatt, zatt)


def _mlstm_kernel(bias_ref, q_ref, k_ref, v_ref, o_ref, gi_ref, gfr_ref, gfc_ref, mn_ref, *rest,
                  chunk, n_chunks, valid_len, has_state):
    if has_state:
        c0_ref, n0_ref, m0_ref, h_ref, c_out, n_out, m_out, c_s, n_s, m_s = rest
    else:
        h_ref, c_out, n_out, m_out, c_s, n_s, m_s = rest
    c = pl.program_id(0) % n_chunks

    @pl.when(c == 0)
    def _():
        if has_state:
            c_s[...] = c0_ref[0]
            n_s[...] = n0_ref[0]
            m_s[...] = m0_ref[0]
        else:
            c_s[...] = jnp.zeros_like(c_s)
            n_s[...] = jnp.zeros_like(n_s)
            m_s[...] = jnp.zeros_like(m_s)

    last = c == n_chunks - 1
    for hd in range(M_HEADS):
        cols = slice(hd * M_HEAD_DIM, (hd + 1) * M_HEAD_DIM)
        _mlstm_head(hd, cols, last, bias_ref, q_ref, k_ref, v_ref, o_ref, gi_ref, gfr_ref, gfc_ref, mn_ref,
                    h_ref, c_out, n_out, m_out, c_s, n_s, m_s, chunk, valid_len)


def _mlstm_head(hd, cols, last, bias_ref, q_ref, k_ref, v_ref, o_ref, gi_ref, gfr_ref, gfc_ref, mn_ref,
                h_ref, c_out, n_out, m_out, c_s, n_s, m_s, chunk, valid_len):
    L = chunk
    q = q_ref[:, cols]
    k = k_ref[:, cols] * (M_HEAD_DIM ** -0.5)
    v = v_ref[:, cols]
    b_i = bias_ref[0, hd]
    b_f = bias_ref[1, hd]
    i_row = gi_ref[0, 0, hd:hd + 1, :] + b_i
    lf_row = _log_sigmoid(gfr_ref[0, 0, hd:hd + 1, :] + b_f)
    lf_col = _log_sigmoid(gfc_ref[0, 0, :, hd:hd + 1] + b_f)
    row_id = lax.broadcasted_iota(jnp.int32, (L, L), 0)
    col_id = lax.broadcasted_iota(jnp.int32, (L, L), 1)
    if valid_len < L:
        lane = lax.broadcasted_iota(jnp.int32, (1, L), 1)
        sub = lax.broadcasted_iota(jnp.int32, (L, 1), 0)
        i_row = jnp.where(lane < valid_len, i_row, NEG_INF)
        lf_row = jnp.where(lane < valid_len, lf_row, 0.0)
        lf_col = jnp.where(sub < valid_len, lf_col, 0.0)
    causal = col_id <= row_id
    b_col = jnp.sum(jnp.where(causal, lf_row, 0.0), axis=1, keepdims=True)
    b_row = jnp.sum(jnp.where(row_id <= col_id, lf_col, 0.0), axis=0, keepdims=True)
    m_prev = m_s[hd]
    dmat = jnp.where(causal, b_col - b_row + i_row, NEG_INF)
    inter = b_col + m_prev
    mj = jnp.maximum(inter, jnp.max(dmat, axis=1, keepdims=True))
    s = lax.dot_general(q, k, (((1,), (1,)), ((), ())), preferred_element_type=F32)
    sc = s * jnp.exp(dmat - mj)
    a_int = jnp.exp(inter - mj)
    c_prev = c_s[hd]
    n_prev = n_s[hd]
    qc = lax.dot_general(q, c_prev.astype(BF16), (((1,), (1,)), ((), ())), preferred_element_type=F32)
    num = jnp.dot(sc.astype(BF16), v, preferred_element_type=F32) + a_int * qc
    qn = jnp.sum(q.astype(F32) * n_prev, axis=1, keepdims=True)
    den = jnp.sum(sc, axis=1, keepdims=True) + a_int * qn
    h = num / jnp.maximum(jnp.abs(den), jnp.exp(-mj))
    h = h * _sigmoid(o_ref[:, cols].astype(F32))
    h = h * lax.rsqrt(jnp.mean(h * h, axis=-1, keepdims=True) + EPS) * mn_ref[:, cols]
    h_ref[:, cols] = h.astype(h_ref.dtype)

    bl = jnp.sum(lf_row, axis=1, keepdims=True)
    g_row = bl - b_row + i_row
    m_new = jnp.maximum(bl + m_prev, jnp.max(g_row, axis=1, keepdims=True))
    ws_row = jnp.exp(g_row - m_new)
    a_c = jnp.exp(bl + m_prev - m_new)
    vt = (v.astype(F32).T * ws_row).astype(BF16)
    c_new = a_c * c_prev + jnp.dot(vt, k, preferred_element_type=F32)
    ws8 = jnp.broadcast_to(ws_row, (8, L)).astype(BF16)
    n_new = a_c * n_prev + jnp.dot(ws8, k, preferred_element_type=F32)[0:1]
    c_s[hd] = c_new
    n_s[hd] = n_new
    m_s[hd] = m_new

    @pl.when(last)
    def _():
        c_out[0, hd] = c_new
        n_out[0, hd] = n_new
        m_out[0, hd] = m_new


def mlstm_part(zm, gates_row, gates_col, bias, m_norm, batch, seq, chunk, valid_len, row0, state=None):
    E = M_HEAD_DIM
    nc = seq // chunk
    blk0 = row0 // chunk

    def zspec(col):
        return pl.BlockSpec((chunk, M_WIDTH), lambda i: (blk0 + i, col))

    def per_chunk(shape):
        return pl.BlockSpec(shape, lambda i: (i // nc, i % nc, 0, 0))

    def per_row(shape):
        return pl.BlockSpec(shape, lambda i: (i // nc, 0, 0, 0))

    in_specs = [pl.BlockSpec(memory_space=pltpu.SMEM),
                zspec(0), zspec(1), zspec(2), zspec(3),
                per_chunk((1, 1, M_HEADS, chunk)), per_chunk((1, 1, M_HEADS, chunk)),
                per_chunk((1, 1, chunk, M_HEADS)),
                pl.BlockSpec((1, M_WIDTH), lambda i: (0, 0))]
    gi_row, gf_row = gates_row
    args = [bias, zm, zm, zm, zm, gi_row, gf_row, gates_col, m_norm.reshape(1, M_WIDTH)]
    state_shapes = [(1, M_HEADS, E, E), (1, M_HEADS, 1, E), (1, M_HEADS, 1, 1)]
    if state is not None:
        c0, n0, m0 = state
        in_specs += [per_row(s) for s in state_shapes]
        args += [c0, n0.reshape(batch, M_HEADS, 1, E), m0.reshape(batch, M_HEADS, 1, 1)]
    return dict(
        body=functools.partial(_mlstm_kernel, chunk=chunk, n_chunks=nc, valid_len=valid_len,
                               has_state=state is not None),
        steps=batch * nc,
        in_specs=in_specs,
        args=args,
        out_specs=[pl.BlockSpec((chunk, M_WIDTH), lambda i: (i, 0))] + [per_row(s) for s in state_shapes],
        out_shape=[jax.ShapeDtypeStruct((batch * seq, M_WIDTH), BF16)]
        + [jax.ShapeDtypeStruct((batch,) + s[1:], F32) for s in state_shapes],
        scratch=[pltpu.VMEM(s[1:], F32) for s in state_shapes],
    )


def fused_call(parts, name):
    steps = parts[0]["steps"]
    assert all(p["steps"] == steps for p in parts)
    n_in = [len(p["in_specs"]) for p in parts]
    n_out = [len(p["out_specs"]) for p in parts]
    n_scr = [len(p["scratch"]) for p in parts]

    def body(*refs):
        ins, outs, scr = refs[:sum(n_in)], refs[sum(n_in):sum(n_in) + sum(n_out)], refs[sum(n_in) + sum(n_out):]
        a = b = c = 0
        for p, na, nb, nc in zip(parts, n_in, n_out, n_scr):
            p["body"](*ins[a:a + na], *outs[b:b + nb], *scr[c:c + nc])
            a, b, c = a + na, b + nb, c + nc

    res = pl.pallas_call(
        body,
        grid=(steps,),
        in_specs=[s for p in parts for s in p["in_specs"]],
        out_specs=[s for p in parts for s in p["out_specs"]],
        out_shape=[s for p in parts for s in p["out_shape"]],
        scratch_shapes=[s for p in parts for s in p["scratch"]],
        compiler_params=_cparams(("arbitrary",), FUSED_VMEM_LIMIT),
        name=name,
    )(*[a for p in parts for a in p["args"]])
    grouped, b = [], 0
    for nb in n_out:
        grouped.append(list(res[b:b + nb]))
        b += nb
    return grouped


ATT_BLK = 128
ATT_SUBSTEPS = 4


def _band_block(qb, kcat, vcat, mask):
    s = lax.dot_general(qb, kcat, (((1,), (1,)), ((), ())), preferred_element_type=F32)
    s = jnp.where(mask, s, NEG_INF)
    mx = jnp.max(s, axis=1, keepdims=True)
    p = jnp.exp(s - mx)
    l = jnp.sum(p, axis=1, keepdims=True)
    o = jnp.dot(p.astype(BF16), vcat, preferred_element_type=F32) / l
    return o, mx + jnp.log(l)


def _att_substep_group(sub):
    last = len(A_GROUPS) - 1
    return jnp.minimum(sub, last) if not isinstance(sub, int) else min(sub, last)


def _attn_prompt_kernel(q_ref, k_ref, v_ref, y_ref, *scr, seq):
    o_scr, lse_scr = scr[:3], scr[3:6]
    scale = A_HEAD_DIM ** -0.5
    qi = lax.broadcasted_iota(jnp.int32, (ATT_BLK, ATT_BLK), 0)
    ki = lax.broadcasted_iota(jnp.int32, (ATT_BLK, ATT_BLK), 1)
    cur_mask = ki <= qi
    prev_mask = ki >= qi
    band_mask = jnp.concatenate([prev_mask, cur_mask], axis=1)

    def band_unit(g, r, n):
        dil = A_GROUPS[g][1]

        def rows(first_blk, n_blk):
            if dil == 1:
                return pl.ds(first_blk * ATT_BLK, n_blk * ATT_BLK)
            return pl.ds(r + first_blk * ATT_BLK * dil, n_blk * ATT_BLK, stride=dil)

        qb = (q_ref[rows(n, 1), :] * scale).astype(BF16)
        if n == 0:
            kk, vv, mask = k_ref[rows(0, 1), :], v_ref[rows(0, 1), :], cur_mask
        else:
            kk, vv, mask = k_ref[rows(n - 1, 2), :], v_ref[rows(n - 1, 2), :], band_mask
        o, lse = _band_block(qb, kk.astype(BF16), vv.astype(BF16), mask)
        o_scr[g][rows(n, 1), :] = o
        lse_scr[g][rows(n, 1), :] = lse

    sub = pl.program_id(0) % ATT_SUBSTEPS
    for s in range(ATT_SUBSTEPS):
        g = _att_substep_group(s)
        dil = A_GROUPS[g][1]
        units = [(g, r, n) for r in range(dil) for n in range(seq // dil // ATT_BLK)]
        shares = [t for t in range(ATT_SUBSTEPS) if _att_substep_group(t) == g]
        per_share = -(-len(units) // len(shares))
        k0 = shares.index(s) * per_share

        @pl.when(sub == s)
        def _(units=units[k0:k0 + per_share]):
            for unit in units:
                band_unit(*unit)

    @pl.when(sub == ATT_SUBSTEPS - 1)
    def _():
        l0, l1, l2 = lse_scr[0][...], lse_scr[1][...], lse_scr[2][...]
        mx = jnp.maximum(jnp.maximum(l0, l1), l2)
        w0, w1, w2 = jnp.exp(l0 - mx), jnp.exp(l1 - mx), jnp.exp(l2 - mx)
        y = (w0 * o_scr[0][...] + w1 * o_scr[1][...] + w2 * o_scr[2][...]) / (w0 + w1 + w2)
        y_ref[...] = y.astype(y_ref.dtype)


def attn_prompt_part(zatt, batch, seq):
    nh = A_WIDTH // LANES
    per_b = A_HEADS_PER_GROUP * ATT_SUBSTEPS

    assert len(A_GROUPS) == ATT_SUBSTEPS - 1

    def slot(i):
        return (i // ATT_SUBSTEPS) % A_HEADS_PER_GROUP

    def spec(part):
        return pl.BlockSpec((seq, LANES), lambda i: (
            i // per_b, part * nh + _att_substep_group(i % ATT_SUBSTEPS) * A_HEADS_PER_GROUP + slot(i)))

    return dict(
        body=functools.partial(_attn_prompt_kernel, seq=seq),
        steps=batch * per_b,
        in_specs=[spec(0), spec(1), spec(2)],
        args=[zatt, zatt, zatt],
        out_specs=[pl.BlockSpec((seq, LANES), lambda i: (i // per_b, slot(i)))],
        out_shape=[jax.ShapeDtypeStruct((batch * seq, A_GROUP_WIDTH), BF16)],
        scratch=[pltpu.VMEM((seq, LANES), F32)] * 3 + [pltpu.VMEM((seq, 1), F32)] * 3,
    )


def _window_kernel(new_ref, *refs, t_dec):
    n_buf = 2 * len(A_GROUPS)
    caches, news = refs[:n_buf], refs[n_buf:2 * n_buf]
    y_ref = refs[2 * n_buf]
    outs = refs[2 * n_buf + 1:3 * n_buf + 1]
    bufs = refs[3 * n_buf + 1:4 * n_buf + 1]
    sem_in, sem_out, sem_new = refs[4 * n_buf + 1:]
    b = pl.program_id(0)
    nb = pl.num_programs(0)
    slot = b % 2

    def copy_in(ci, row, sl):
        return pltpu.make_async_copy(caches[ci].at[0, row], bufs[ci].at[sl], sem_in.at[sl, ci])

    def copy_out(ci, row, sl):
        wb = bufs[ci].shape[1]
        return pltpu.make_async_copy(bufs[ci].at[sl, pl.ds(t_dec, wb - t_dec)],
                                     outs[ci].at[0, row, pl.ds(0, wb - t_dec)], sem_out.at[sl, ci])

    def copy_new(ci):
        wb = bufs[ci].shape[1]
        return pltpu.make_async_copy(news[ci], outs[ci].at[0, :, pl.ds(wb - t_dec, t_dec)], sem_new.at[ci])

    @pl.when(b == 0)
    def _():
        for ci in range(n_buf):
            copy_new(ci).start()
            copy_in(ci, 0, 0).start()

    for ci in range(n_buf):
        copy_in(ci, b, slot).wait()
        copy_out(ci, b, slot).start()

    @pl.when(b >= 1)
    def _():
        for ci in range(n_buf):
            copy_out(ci, b - 1, 1 - slot).wait()

    @pl.when(b + 1 < nb)
    def _():
        for ci in range(n_buf):
            copy_in(ci, b + 1, 1 - slot).start()

    _attn_sample(new_ref, bufs, slot, y_ref, t_dec)

    @pl.when(b == nb - 1)
    def _():
        for ci in range(n_buf):
            copy_out(ci, b, slot).wait()
            copy_new(ci).wait()


def _attn_sample(new_ref, bufs, slot, y_ref, t_dec):
    nh = A_WIDTH // LANES
    scale = A_HEAD_DIM ** -0.5
    jj = lax.broadcasted_iota(jnp.int32, (ATT_BLK, 1, 1), 0)
    for t in range(t_dec):
        outs, lses = [], []
        for g, (_, dil) in enumerate(A_GROUPS):
            h0 = g * A_HEADS_PER_GROUP
            q = new_ref[0, t, h0:h0 + A_HEADS_PER_GROUP, :] * scale
            rows = pl.ds(t % dil, ATT_BLK, stride=dil) if dil > 1 else pl.ds(0, ATT_BLK)
            kc = bufs[2 * g][slot, rows]
            vc = bufs[2 * g + 1][slot, rows]
            s_c = jnp.sum(kc * q[None], axis=-1, keepdims=True)
            if t // dil > 0:
                s_c = jnp.where(jj >= t // dil, s_c, NEG_INF)
            mx = jnp.max(s_c, axis=0)
            new_u = [u for u in range(t + 1) if (t - u) % dil == 0]
            s_new = []
            for u in new_u:
                k_u = new_ref[0, u, nh + h0:nh + h0 + A_HEADS_PER_GROUP, :]
                s_u = jnp.sum(k_u * q, axis=-1, keepdims=True)
                s_new.append(s_u)
                mx = jnp.maximum(mx, s_u)
            p_c = jnp.exp(s_c - mx[None])
            l = jnp.sum(p_c, axis=0)
            acc = jnp.sum(p_c * vc, axis=0)
            for u, s_u in zip(new_u, s_new):
                p_u = jnp.exp(s_u - mx)
                v_u = new_ref[0, u, 2 * nh + h0:2 * nh + h0 + A_HEADS_PER_GROUP, :]
                l = l + p_u
                acc = acc + p_u * v_u
            outs.append(acc / l)
            lses.append(mx + jnp.log(l))
        mxg = jnp.maximum(jnp.maximum(lses[0], lses[1]), lses[2])
        ws = [jnp.exp(l_g - mxg) for l_g in lses]
        y = (ws[0] * outs[0] + ws[1] * outs[1] + ws[2] * outs[2]) / (ws[0] + ws[1] + ws[2])
        y_ref[0, t] = y.astype(y_ref.dtype)


def window_part(new_qkv, caches, news):
    batch, t_dec = new_qkv.shape[:2]
    for (win, dil), k_buf in zip(A_GROUPS, caches[::2]):
        assert win == ATT_BLK * dil and k_buf.shape[2] == win
    any_spec = pl.BlockSpec(memory_space=pl.ANY)
    n_buf = len(caches)
    return dict(
        body=functools.partial(_window_kernel, t_dec=t_dec),
        steps=batch,
        in_specs=[pl.BlockSpec((1, t_dec, new_qkv.shape[2], LANES), lambda b: (b, 0, 0, 0))]
        + [any_spec] * (2 * n_buf),
        args=[new_qkv, *caches, *news],
        out_specs=[pl.BlockSpec((1, t_dec, A_HEADS_PER_GROUP, LANES), lambda b: (b, 0, 0, 0))] + [any_spec] * n_buf,
        out_shape=[jax.ShapeDtypeStruct((batch, t_dec, A_HEADS_PER_GROUP, LANES), BF16)]
        + [jax.ShapeDtypeStruct(c.shape, c.dtype) for c in caches],
        scratch=[pltpu.VMEM((2,) + c.shape[2:], c.dtype) for c in caches]
        + [pltpu.SemaphoreType.DMA((2, n_buf)), pltpu.SemaphoreType.DMA((2, n_buf)),
           pltpu.SemaphoreType.DMA((n_buf,))],
    )


MOE_TM = 256
MOE_CHUNK = 8
MOE_BM = 256
MOE_TILE_ROWS = TOP_K * MOE_TM + N_EXPERTS * MOE_CHUNK
ROUTE_EXPERT, ROUTE_POS, ROUTE_GATE = 0, TOP_K, 2 * TOP_K


def _merge_route_kernel(h_a, h_b, ha_a, ha_b, hb_a, hb_b, ga_ref, gb_ref, wpa_ref, wpb_ref, wo_ref, g2_ref,
                        wrh_ref, wrl_ref, br_ref, h1_ref, xt_ref, route_ref, cnt_ref, *, tiles_a):
    tm = h_a.shape[0]
    rows = xt_ref.shape[0]
    first = pl.program_id(0) < tiles_a
    h = jnp.where(first, h_a[...], h_b[...])
    a = jnp.dot(jnp.where(first, ha_a[...], ha_b[...]), wpa_ref[...], preferred_element_type=F32)
    b = jnp.dot(jnp.where(first, hb_a[...], hb_b[...]), wpb_ref[...], preferred_element_type=F32)
    u = _sigmoid(ga_ref[...].astype(F32)) * a + _sigmoid(gb_ref[...].astype(F32)) * b
    h1 = h + jnp.dot(u.astype(BF16), wo_ref[...], preferred_element_type=F32)
    h1_ref[...] = h1
    xn = _rms(h1, g2_ref[...])
    x_hi = xn.astype(BF16)
    x_lo = (xn - x_hi.astype(F32)).astype(BF16)
    logits = (jnp.dot(x_hi, wrh_ref[...], preferred_element_type=F32)
              + jnp.dot(x_lo, wrh_ref[...], preferred_element_type=F32)
              + jnp.dot(x_hi, wrl_ref[...], preferred_element_type=F32)) + br_ref[...]
    lane = lax.broadcasted_iota(jnp.int32, (tm, LANES), 1)
    logits = jnp.where(lane < N_EXPERTS, logits, NEG_INF)
    vals, hots = [], []
    work = logits
    for _ in range(TOP_K):
        mx = jnp.max(work, axis=1, keepdims=True)
        idx = jnp.min(jnp.where(work == mx, lane, LANES), axis=1, keepdims=True)
        hot = lane == idx
        work = jnp.where(hot, NEG_INF, work)
        vals.append(mx)
        hots.append(hot)
    exps = [jnp.exp(v - vals[0]) for v in vals]
    denom = exps[0] + exps[1] + exps[2] + exps[3]
    hot_f = jnp.zeros((tm, LANES), F32)
    for hot in hots:
        hot_f = hot_f + jnp.where(hot, 1.0, 0.0)
    r_id = lax.broadcasted_iota(jnp.int32, (tm, tm), 0)
    c_id = lax.broadcasted_iota(jnp.int32, (tm, tm), 1)
    earlier = jnp.where(c_id < r_id, 1.0, 0.0).astype(BF16)
    rank = jnp.dot(earlier, hot_f.astype(BF16), preferred_element_type=F32)
    cnt = jnp.sum(hot_f, axis=0, keepdims=True)
    padded = jnp.floor((cnt + (MOE_CHUNK - 1)) * (1.0 / MOE_CHUNK)) * MOE_CHUNK
    la = lax.broadcasted_iota(jnp.int32, (LANES, LANES), 0)
    lb = lax.broadcasted_iota(jnp.int32, (LANES, LANES), 1)
    before = jnp.where(la < lb, 1.0, 0.0).astype(BF16)
    run_start = jnp.dot(jnp.broadcast_to(padded, (8, LANES)).astype(BF16), before,
                        preferred_element_type=F32)[0:1]
    pos_all = run_start + rank
    lane_f = lane.astype(F32)
    row_id = lax.broadcasted_iota(jnp.int32, (tm, rows), 1)
    place = jnp.zeros((tm, rows), F32)
    route = jnp.zeros((tm, LANES), F32)
    for k in range(TOP_K):
        e_k = jnp.sum(jnp.where(hots[k], lane_f, 0.0), axis=1, keepdims=True)
        p_k = jnp.sum(jnp.where(hots[k], pos_all, 0.0), axis=1, keepdims=True)
        place = place + jnp.where(row_id == p_k.astype(jnp.int32), 1.0, 0.0)
        route = jnp.where(lane == ROUTE_EXPERT + k, e_k, route)
        route = jnp.where(lane == ROUTE_POS + k, p_k, route)
        route = jnp.where(lane == ROUTE_GATE + k, exps[k] / denom, route)
    route_ref[...] = route
    cnt_ref[0] = cnt
    xt_ref[...] = lax.dot_general(place.astype(BF16), x_hi, (((0,), (0,)), ((), ())),
                                  preferred_element_type=F32)


def merge_route(h, ha, hb, zmg, w_pa, w_pb, w_o, norm2, w_r_hi, w_r_lo, b_r):
    d = h[0].shape[1]
    tm = MOE_TM
    tiles_a = h[0].shape[0] // tm
    n = h[0].shape[0] + h[1].shape[0]
    nt = n // tm

    def full(arr):
        return pl.BlockSpec(arr.shape, lambda i: (0,) * arr.ndim)

    weights = [w_pa, w_pb, w_o, norm2.reshape(1, d), w_r_hi, w_r_lo, b_r]
    return pl.pallas_call(
        functools.partial(_merge_route_kernel, tiles_a=tiles_a),
        grid=(nt,),
        in_specs=_two_source_specs((tm, d), tiles_a) + _two_source_specs((tm, M_WIDTH), tiles_a)
        + _two_source_specs((tm, A_GROUP_WIDTH), tiles_a)
        + [pl.BlockSpec((tm, d), lambda i: (i, 0)),
           pl.BlockSpec((tm, d), lambda i: (i, 1))] + [full(wt) for wt in weights],
        out_specs=[pl.BlockSpec((tm, d), lambda i: (i, 0)),
                   pl.BlockSpec((MOE_TILE_ROWS, d), lambda i: (i, 0)),
                   pl.BlockSpec((tm, LANES), lambda i: (i, 0)),
                   pl.BlockSpec((1, 1, LANES), lambda i: (i, 0, 0))],
        out_shape=[jax.ShapeDtypeStruct((n, d), F32),
                   jax.ShapeDtypeStruct((nt * MOE_TILE_ROWS, d), F32),
                   jax.ShapeDtypeStruct((n, LANES), F32),
                   jax.ShapeDtypeStruct((nt, 1, LANES), F32)],
        compiler_params=_cparams(("parallel",)),
        name="merge_route",
    )(*h, *ha, *hb, zmg, zmg, *weights)


def moe_plan(cnt, n_blocks):
    nt = cnt.shape[0]
    cpb = MOE_BM // MOE_CHUNK
    cpt = MOE_TILE_ROWS // MOE_CHUNK
    i32 = jnp.int32
    cnt = cnt[:, 0, :N_EXPERTS].astype(i32)
    nch = (cnt + MOE_CHUNK - 1) // MOE_CHUNK
    off_incl = jnp.cumsum(nch, axis=1)
    off_ch = off_incl - nch
    seq_incl = jnp.cumsum(nch, axis=0)
    base_ch = seq_incl - nch
    tot = seq_incl[-1]
    nb = (tot + cpb - 1) // cpb
    bend = jnp.cumsum(nb)
    bstart = bend - nb
    n_used = bend[-1:]
    blk = jnp.arange(n_blocks, dtype=i32)
    blk_exp = jnp.minimum(jnp.sum(blk[:, None] >= bend[None, :], axis=1), N_EXPERTS - 1).astype(i32)
    experts = jnp.arange(N_EXPERTS, dtype=i32)
    tiles = jnp.arange(nt, dtype=i32)

    def pick(hot, table):
        return jnp.sum(jnp.where(hot, table, 0), axis=-1).astype(i32)

    hot_e = blk_exp[:, None] == experts[None, :]
    seq_b, off_b, base_b = (pick(hot_e[:, None, :], t[None]) for t in (seq_incl, off_ch, base_ch))
    q = (blk - pick(hot_e, bstart[None]))[:, None] * cpb + jnp.arange(cpb, dtype=i32)[None, :]
    tile = jnp.minimum(jnp.sum(q[:, :, None] >= seq_b[:, None, :], axis=2), nt - 1).astype(i32)
    hot_t = tile[:, :, None] == tiles[None, None, :]
    src = (tile * cpt + pick(hot_t, off_b[:, None, :]) + q - pick(hot_t, base_b[:, None, :])) * MOE_CHUNK
    src = jnp.where(q < pick(hot_e, tot[None])[:, None], src, 0).astype(i32)
    slot = jnp.arange(cpt, dtype=i32)
    e_s = jnp.sum(slot[None, :, None] >= off_incl[:, None, :], axis=2).astype(i32)
    hot_s = e_s[:, :, None] == experts[None, None, :]
    back = (pick(hot_s, bstart[None, None, :]) * cpb + pick(hot_s, base_ch[:, None, :])
            + slot[None, :] - pick(hot_s, off_ch[:, None, :])) * MOE_CHUNK
    back = jnp.where(e_s < N_EXPERTS, back, 0).astype(i32)
    return blk_exp, n_used.astype(i32), src.reshape(-1), back.reshape(-1)


def _swiglu(h):
    x_glu = jnp.minimum(h[:, :D_FF], SWIGLU_LIMIT)
    x_lin = jnp.clip(h[:, D_FF:], -SWIGLU_LIMIT, SWIGLU_LIMIT)
    return x_glu * _sigmoid(SWIGLU_ALPHA * x_glu) * (x_lin + 1.0)


def _chunk_gather_start(src_hbm, rows_ref, first, dst_ref, sem):
    for c in range(dst_ref.shape[0] // MOE_CHUNK):
        r = pl.multiple_of(rows_ref[first + c], MOE_CHUNK)
        pltpu.make_async_copy(src_hbm.at[pl.ds(r, MOE_CHUNK)],
                              dst_ref.at[pl.ds(c * MOE_CHUNK, MOE_CHUNK)], sem).start()


def _chunk_gather_wait(src_hbm, dst_ref, sem):
    pltpu.make_async_copy(src_hbm.at[pl.ds(0, dst_ref.shape[0])], dst_ref, sem).wait()


def _moe_kernel(blk_exp_ref, n_used_ref, rows_ref, xt_hbm, w1_ref, b1_ref, w2_ref, b2_ref, y_ref,
                xbuf, w1b, w2b, sems):
    i = pl.program_id(0)
    n_used = n_used_ref[0]
    cpb = xbuf.shape[1] // MOE_CHUNK

    def fetch(blk, slot):
        _chunk_gather_start(xt_hbm, rows_ref, blk * cpb, xbuf.at[slot], sems.at[slot])

    @pl.when(jnp.logical_and(i == 0, n_used > 0))
    def _():
        fetch(0, 0)

    @pl.when(i < n_used)
    def _():
        slot = i % 2
        fetch(jnp.minimum(i + 1, n_used - 1), 1 - slot)
        _chunk_gather_wait(xt_hbm, xbuf.at[slot], sems.at[slot])
        changed = jnp.logical_or(i == 0, blk_exp_ref[i] != blk_exp_ref[jnp.maximum(i - 1, 0)])

        @pl.when(changed)
        def _():
            w1b[...] = w1_ref[0].astype(BF16)
            w2b[...] = w2_ref[0].astype(BF16)

        h = jnp.dot(xbuf[slot].astype(BF16), w1b[...], preferred_element_type=F32) + b1_ref[0]
        act = _swiglu(h)
        y_ref[...] = jnp.dot(act.astype(BF16), w2b[...], preferred_element_type=F32) + b2_ref[0]

        @pl.when(i == n_used - 1)
        def _():
            _chunk_gather_wait(xt_hbm, xbuf.at[1 - slot], sems.at[1 - slot])

    @pl.when(i >= n_used)
    def _():
        y_ref[...] = jnp.zeros_like(y_ref)


def moe_experts(xt, blk_exp, n_used, chunk_rows, w1, b1, w2, b2):
    d = xt.shape[1]
    nblk = blk_exp.shape[0]
    ne = w1.shape[0]
    bm = MOE_BM
    grid_spec = pltpu.PrefetchScalarGridSpec(
        num_scalar_prefetch=3,
        grid=(nblk,),
        in_specs=[pl.BlockSpec(memory_space=pl.ANY),
                  pl.BlockSpec((1, d, 2 * D_FF), lambda i, be, nu, cr: (be[i], 0, 0)),
                  pl.BlockSpec((1, 1, 2 * D_FF), lambda i, be, nu, cr: (be[i], 0, 0)),
                  pl.BlockSpec((1, D_FF, d), lambda i, be, nu, cr: (be[i], 0, 0)),
                  pl.BlockSpec((1, 1, d), lambda i, be, nu, cr: (be[i], 0, 0))],
        out_specs=pl.BlockSpec((bm, d), lambda i, be, nu, cr: (i, 0)),
        scratch_shapes=[pltpu.VMEM((2, bm, d), F32), pltpu.VMEM((d, 2 * D_FF), BF16),
                        pltpu.VMEM((D_FF, d), BF16), pltpu.SemaphoreType.DMA((2,))],
    )
    return pl.pallas_call(
        _moe_kernel,
        grid_spec=grid_spec,
        out_shape=jax.ShapeDtypeStruct((nblk * bm, d), F32),
        compiler_params=_cparams(("arbitrary",)),
        name="moe_experts",
    )(blk_exp, n_used, chunk_rows, xt, w1, b1.reshape(ne, 1, -1), w2, b2.reshape(ne, 1, -1))


def _combine_kernel(rows_ref, ys_hbm, h1_ref, route_ref, p_ref, wple_ref, wpg_ref, g3_ref, gf_ref, y_ref,
                    ybuf, sems, *, tile0):
    i = pl.program_id(0)
    n = pl.num_programs(0)
    tm = h1_ref.shape[0]
    rows = ybuf.shape[1]
    cpt = rows // MOE_CHUNK

    def fetch(step, slot):
        _chunk_gather_start(ys_hbm, rows_ref, (tile0 + step) * cpt, ybuf.at[slot], sems.at[slot])

    @pl.when(i == 0)
    def _():
        fetch(0, 0)

    slot = i % 2
    fetch(jnp.minimum(i + 1, n - 1), 1 - slot)
    _chunk_gather_wait(ys_hbm, ybuf.at[slot], sems.at[slot])
    route = route_ref[...]
    row_id = lax.broadcasted_iota(jnp.int32, (tm, rows), 1)
    weight = jnp.zeros((tm, rows), F32)
    for k in range(TOP_K):
        p_k = route[:, ROUTE_POS + k:ROUTE_POS + k + 1].astype(jnp.int32)
        weight = weight + jnp.where(row_id == p_k, route[:, ROUTE_GATE + k:ROUTE_GATE + k + 1], 0.0)
    h2 = h1_ref[...] + jnp.dot(weight.astype(BF16), ybuf[slot].astype(BF16), preferred_element_type=F32)
    ple = jnp.dot(p_ref[...].astype(BF16), wple_ref[...], preferred_element_type=F32)
    gate = _sigmoid(jnp.dot(_rms(h2, g3_ref[...]).astype(BF16), wpg_ref[...], preferred_element_type=F32))
    h3 = h2 + ple * gate
    y_ref[...] = _rms(h3, gf_ref[...])

    @pl.when(i == n - 1)
    def _():
        _chunk_gather_wait(ys_hbm, ybuf.at[1 - slot], sems.at[1 - slot])


def moe_combine(ys, back_rows, h1, route, p, w_ple, w_pg, norm3, norm_f, tile0, n_tiles):
    d = h1.shape[1]
    tm = MOE_TM

    def full(arr):
        return pl.BlockSpec(arr.shape, lambda i, br: (0,) * arr.ndim)

    weights = [w_ple, w_pg, norm3.reshape(1, d), norm_f.reshape(1, d)]
    grid_spec = pltpu.PrefetchScalarGridSpec(
        num_scalar_prefetch=1,
        grid=(n_tiles,),
        in_specs=[pl.BlockSpec(memory_space=pl.ANY),
                  pl.BlockSpec((tm, d), lambda i, br: (tile0 + i, 0)),
                  pl.BlockSpec((tm, LANES), lambda i, br: (tile0 + i, 0)),
                  pl.BlockSpec((tm, PLE_DIM), lambda i, br: (i, 0))] + [full(wt) for wt in weights],
        out_specs=pl.BlockSpec((tm, d), lambda i, br: (i, 0)),
        scratch_shapes=[pltpu.VMEM((2, MOE_TILE_ROWS, d), F32), pltpu.SemaphoreType.DMA((2,))],
    )
    return pl.pallas_call(
        functools.partial(_combine_kernel, tile0=tile0),
        grid_spec=grid_spec,
        out_shape=jax.ShapeDtypeStruct((n_tiles * tm, d), F32),
        compiler_params=_cparams(("arbitrary",)),
        name="moe_combine",
    )(back_rows, ys, h1, route, p, *weights)


def _gate_layouts(zg, batch, seq, chunk):
    nc = seq // chunk
    g = zg[:, :2 * M_HEADS].reshape(batch, nc, chunk, 2, M_HEADS)
    rows = jnp.transpose(g, (3, 0, 1, 4, 2))
    return (rows[0], rows[1]), g[:, :, :, 1, :]


def kernel(x_prompt, x_sample, state_C, state_n, state_m, cache_k0, cache_v0, cache_k1, cache_v1, cache_k2, cache_v2, p_prompt, p_sample, norm1, w_in, b_igate, b_fgate, m_norm, w_pa, w_pb, w_o, norm2, w_router, b_router, w1, b1, w2, b2, norm3, w_ple, w_ple_gate, norm_f):
    bp, seq, d = x_prompt.shape
    bs, t_dec, _ = x_sample.shape
    n_p, n_s = bp * seq, bs * t_dec
    n_all = n_p + n_s
    x_p, x_s = x_prompt.reshape(n_p, d), x_sample.reshape(n_s, d)

    w = w_in[0]
    c_gate = 4 * M_WIDTH
    c_att = c_gate + 2 * M_HEADS
    c_mg = c_att + 3 * A_WIDTH
    w_m = w[:, :c_gate].astype(BF16)
    w_gate = jnp.pad(w[:, c_gate:c_att], ((0, 0), (0, LANES - 2 * M_HEADS))).astype(BF16)
    w_att = w[:, c_att:c_mg].astype(BF16)
    w_mg = w[:, c_mg:].astype(BF16)
    rope = rope_tables(np.concatenate([np.arange(seq), np.tile(np.arange(t_dec) + PAST_LEN, bs)]))
    tiles_p, tiles_seq = n_p // PROJ_TM, seq // PROJ_TM

    def rope_block(i):
        return jnp.where(i < tiles_p, i % tiles_seq, tiles_seq + i - tiles_p)

    zm, zgate = norm_proj(x_p, x_s, norm1[0], w_m, BF16, 2048, w_extra=w_gate)
    zatt = norm_proj(x_p, x_s, norm1[0], w_att, F32, A_WIDTH, rope=rope, n_rope_blocks=2, rope_block=rope_block)
    zmg = norm_proj(x_p, x_s, norm1[0], w_mg, BF16, 2048)

    kv_p = kv_out(zatt, bp, seq)

    z_new = zatt[n_p:]
    caches = (cache_k0, cache_v0, cache_k1, cache_v1, cache_k2, cache_v2)
    news = []
    for g in range(len(A_GROUPS)):
        for part in range(2):
            col = (1 + part) * A_WIDTH + g * A_GROUP_WIDTH
            news.append(z_new[:, col:col + A_GROUP_WIDTH].reshape(bs, t_dec, A_HEADS_PER_GROUP, A_HEAD_DIM))
    bias = jnp.stack([b_igate[0], b_fgate[0]])
    chunk = 128
    grow_p, gcol_p = _gate_layouts(zgate[:n_p], bp, seq, chunk)
    t_pad = 8
    zm_s = jnp.pad(zm[n_p:].reshape(bs, t_dec, -1), ((0, 0), (0, t_pad - t_dec), (0, 0))).reshape(bs * t_pad, -1)
    zg_s = jnp.pad(zgate[n_p:].reshape(bs, t_dec, -1), ((0, 0), (0, t_pad - t_dec), (0, 0))).reshape(bs * t_pad, -1)
    grow_s, gcol_s = _gate_layouts(zg_s, bs, t_pad, t_pad)
    (hb_s, *kv_s), (ha_p, c_p, nn_p, m_p), (ha_s, c_s, nn_s, m_s), (hb_p,) = fused_call([
        window_part(z_new.reshape(bs, t_dec, 3 * A_WIDTH // LANES, LANES), caches, news),
        mlstm_part(zm, grow_p, gcol_p, bias, m_norm[0], bp, seq, chunk, chunk, 0),
        mlstm_part(zm_s, grow_s, gcol_s, bias, m_norm[0], bs, t_pad, t_pad, t_dec, 0,
                   state=(state_C[0], state_n[0], state_m[0])),
        attn_prompt_part(zatt, bp, seq),
    ], "mixers")
    nn_p, nn_s = nn_p.reshape(bp, M_HEADS, M_HEAD_DIM), nn_s.reshape(bs, M_HEADS, M_HEAD_DIM)
    m_p, m_s = m_p.reshape(bp, M_HEADS), m_s.reshape(bs, M_HEADS)
    ha_s = ha_s.reshape(bs, t_pad, M_WIDTH)[:, :t_dec].reshape(n_s, M_WIDTH)
    hb_s = hb_s.reshape(n_s, A_GROUP_WIDTH)

    w_r = jnp.pad(w_router[0], ((0, 0), (0, LANES - N_EXPERTS)))
    w_r_hi = w_r.astype(BF16)
    w_r_lo = (w_r - w_r_hi.astype(F32)).astype(BF16)
    b_r = jnp.pad(b_router[0], (0, LANES - N_EXPERTS)).reshape(1, LANES)
    h1, xt, route, cnt = merge_route((x_p, x_s), (ha_p, ha_s), (hb_p, hb_s), zmg, w_pa[0].astype(BF16),
                                     w_pb[0].astype(BF16), w_o[0].astype(BF16), norm2[0], w_r_hi, w_r_lo, b_r)
    nt = n_all // MOE_TM
    n_blocks = nt * MOE_TILE_ROWS // MOE_BM + N_EXPERTS + 1
    blk_exp, n_used, src_rows, back_rows = moe_plan(cnt, n_blocks)
    ys = moe_experts(xt, blk_exp, n_used, src_rows, w1[0], b1[0], w2[0], b2[0])
    tail_w = (w_ple[0].astype(BF16), w_ple_gate[0].astype(BF16), norm3[0], norm_f)
    nt_p = n_p // MOE_TM
    y_p = moe_combine(ys, back_rows, h1, route, p_prompt[0].reshape(n_p, PLE_DIM), *tail_w, 0, nt_p)
    y_s = moe_combine(ys, back_rows, h1, route, p_sample[0].reshape(n_s, PLE_DIM), *tail_w, nt_p, nt - nt_p)

    return (y_p.reshape(bp, seq, d), y_s.reshape(bs, t_dec, d),
            c_p[None], nn_p[None], m_p[None], *kv_p,
            c_s[None], nn_s[None], m_s[None], *kv_s)
```

```python
import functools

import jax
import jax.numpy as jnp
import numpy as np
from jax import lax
from jax.experimental import pallas as pl
from jax.experimental.pallas import tpu as pltpu

F32 = jnp.float32
BF16 = jnp.bfloat16

D_MODEL = 1024
PAST_LEN = 8192
M_HEADS = 4
M_HEAD_DIM = 256
M_WIDTH = M_HEADS * M_HEAD_DIM
A_GROUPS = ((128, 1), (512, 4), (2048, 16))
A_HEADS_PER_GROUP = 4
A_HEAD_DIM = 128
A_GROUP_WIDTH = A_HEADS_PER_GROUP * A_HEAD_DIM
A_WIDTH = len(A_GROUPS) * A_GROUP_WIDTH
ROPE_THETA = 500000.0
ROPE_DIM = A_HEAD_DIM // 4
N_EXPERTS = 32
TOP_K = 4
D_FF = D_MODEL
SWIGLU_ALPHA = 1.702
SWIGLU_LIMIT = 7.0
PLE_DIM = 256
EPS = 1e-6

LANES = 128
VMEM_LIMIT = 56 * 1024 * 1024
FUSED_VMEM_LIMIT = 60 * 1024 * 1024
NEG_INF = float("-inf")
PROJ_TM = 512


def _cparams(sem, vmem_limit=VMEM_LIMIT):
    return pltpu.CompilerParams(dimension_semantics=sem, vmem_limit_bytes=vmem_limit)


def _rms(x, gain):
    return x * lax.rsqrt(jnp.mean(x * x, axis=-1, keepdims=True) + EPS) * gain


def _sigmoid(x):
    return 1.0 / (1.0 + jnp.exp(-x))


def _log_sigmoid(x):
    return jnp.minimum(x, 0.0) - jnp.log(1.0 + jnp.exp(-jnp.abs(x)))


def _two_source_specs(block, tiles_a, ix=lambda fn: fn):
    return [pl.BlockSpec(block, ix(lambda i, *_: (jnp.minimum(i, tiles_a - 1), 0))),
            pl.BlockSpec(block, ix(lambda i, *_: (jnp.maximum(i - tiles_a, 0), 0)))]


def _norm_proj_kernel(xa_ref, xb_ref, g_ref, w_ref, *rest, n_rope_blocks, heads_per_block, has_extra, col_axis,
                      tiles_a):
    rest = list(rest)
    if n_rope_blocks:
        cos_ref, sinm_ref, sinp_ref = rest[:3]
        rest = rest[3:]
    if has_extra:
        we_ref, o_ref, oe_ref = rest
    else:
        o_ref, = rest
    j = pl.program_id(col_axis)
    x = jnp.where(pl.program_id(1 - col_axis) < tiles_a, xa_ref[...], xb_ref[...])
    xn = _rms(x, g_ref[...]).astype(BF16)
    z = jnp.dot(xn, w_ref[...], preferred_element_type=F32)
    if has_extra:
        oe_ref[...] = jnp.dot(xn, we_ref[...], preferred_element_type=F32)
    if n_rope_blocks:
        @pl.when(j < n_rope_blocks)
        def _():
            cosf, sinm, sinp = cos_ref[...], sinm_ref[...], sinp_ref[...]
            for h in range(heads_per_block):
                zh = z[:, h * LANES:(h + 1) * LANES]
                rot = (zh * cosf + pltpu.roll(zh, LANES - ROPE_DIM // 2, 1) * sinm
                       + pltpu.roll(zh, ROPE_DIM // 2, 1) * sinp)
                o_ref[:, h * LANES:(h + 1) * LANES] = rot.astype(o_ref.dtype)

        @pl.when(j >= n_rope_blocks)
        def _():
            o_ref[...] = z.astype(o_ref.dtype)
    else:
        o_ref[...] = z.astype(o_ref.dtype)


def norm_proj(xa, xb, gain, w, out_dtype, tn, rope=None, n_rope_blocks=0, rope_block=None, w_extra=None):
    d = xa.shape[1]
    tm = PROJ_TM
    tiles_a = xa.shape[0] // tm
    n = xa.shape[0] + xb.shape[0]
    ncol = w.shape[1]
    rows_outer = w_extra is not None
    grid = (n // tm, ncol // tn) if rows_outer else (ncol // tn, n // tm)

    def ix(fn):
        return (lambda i, j: fn(i, j)) if rows_outer else (lambda j, i: fn(i, j))

    in_specs = _two_source_specs((tm, d), tiles_a, ix) + [
        pl.BlockSpec((1, d), ix(lambda i, j: (0, 0))),
        pl.BlockSpec((d, tn), ix(lambda i, j: (0, j)))]
    args = [xa, xb, gain.reshape(1, d), w]
    if n_rope_blocks:
        in_specs += [pl.BlockSpec((tm, LANES), ix(lambda i, j: (rope_block(i), 0)))] * 3
        args += list(rope)
    out_specs = pl.BlockSpec((tm, tn), ix(lambda i, j: (i, j)))
    out_shape = jax.ShapeDtypeStruct((n, ncol), out_dtype)
    if w_extra is not None:
        in_specs.append(pl.BlockSpec((d, LANES), ix(lambda i, j: (0, 0))))
        args.append(w_extra)
        out_specs = [out_specs, pl.BlockSpec((tm, LANES), ix(lambda i, j: (i, 0)))]
        out_shape = [out_shape, jax.ShapeDtypeStruct((n, LANES), F32)]
    return pl.pallas_call(
        functools.partial(_norm_proj_kernel, n_rope_blocks=n_rope_blocks, heads_per_block=tn // LANES,
                          has_extra=w_extra is not None, col_axis=1 if rows_outer else 0, tiles_a=tiles_a),
        grid=grid,
        in_specs=in_specs,
        out_specs=out_specs,
        out_shape=out_shape,
        compiler_params=_cparams(("arbitrary", "arbitrary")),
        name="norm_proj",
    )(*args)


def rope_tables(pos):
    half = ROPE_DIM // 2
    inv = (1.0 / (np.float32(ROPE_THETA) ** (np.arange(0, ROPE_DIM, 2, dtype=np.float32) / ROPE_DIM))).astype(np.float32)
    ang = (pos.astype(np.float32)[:, None] * inv[None, :]).astype(np.float64)
    cos, sin = np.cos(ang), np.sin(ang)
    n = pos.shape[0]
    ones = np.ones((n, LANES - ROPE_DIM))
    zeros = np.zeros((n, LANES - ROPE_DIM))
    zh = np.zeros((n, half))
    cosf = np.concatenate([cos, cos, ones], axis=1).astype(np.float32)
    sinm = np.concatenate([-sin, zh, zeros], axis=1).astype(np.float32)
    sinp = np.concatenate([zh, sin, zeros], axis=1).astype(np.float32)
    return jnp.asarray(cosf), jnp.asarray(sinm), jnp.asarray(sinp)


KV_ROWS = 512


def _kv_out_kernel(zk_ref, zv_ref, *outs, seq):
    s = pl.program_id(1)
    last = s == pl.num_programs(1) - 1
    for g, (win, _) in enumerate(A_GROUPS):
        keep = min(win, seq)
        for z_ref, o_ref in ((zk_ref, outs[2 * g]), (zv_ref, outs[2 * g + 1])):
            def write(z_ref=z_ref, o_ref=o_ref, rows=min(keep, KV_ROWS), g=g):
                for h in range(A_HEADS_PER_GROUP):
                    col = (g * A_HEADS_PER_GROUP + h) * LANES
                    o_ref[0, 0, :, h, :] = z_ref[KV_ROWS - rows:, col:col + LANES]

            if keep >= seq:
                write()
            else:
                assert keep <= KV_ROWS
                pl.when(last)(write)


def kv_out(zatt, batch, seq):
    steps = seq // KV_ROWS
    in_specs = [pl.BlockSpec((KV_ROWS, A_WIDTH), lambda b, s: (b * steps + s, 1)),
                pl.BlockSpec((KV_ROWS, A_WIDTH), lambda b, s: (b * steps + s, 2))]
    out_specs, out_shape = [], []
    for win, _ in A_GROUPS:
        keep = min(win, seq)
        rows = min(keep, KV_ROWS)
        idx = (lambda b, s: (0, b, s, 0, 0)) if keep >= seq else (lambda b, s: (0, b, 0, 0, 0))
        for _ in range(2):
            out_specs.append(pl.BlockSpec((1, 1, rows, A_HEADS_PER_GROUP, A_HEAD_DIM), idx))
            out_shape.append(jax.ShapeDtypeStruct((1, batch, keep, A_HEADS_PER_GROUP, A_HEAD_DIM), F32))
    return pl.pallas_call(
        functools.partial(_kv_out_kernel, seq=seq),
        grid=(batch, steps),
        in_specs=in_specs,
        out_specs=out_specs,
        out_shape=out_shape,
        compiler_params=_cparams(("arbitrary", "arbitrary")),
        name="kv_out",
    )(zatt, zatt)


def _mlstm_kernel(bias_ref, q_ref, k_ref, v_ref, o_ref, gi_ref, gfr_ref, gfc_ref, mn_ref, *rest,
                  chunk, n_chunks, valid_len, has_state):
    if has_state:
        c0_ref, n0_ref, m0_ref, h_ref, c_out, n_out, m_out, c_s, n_s, m_s = rest
    else:
        h_ref, c_out, n_out, m_out, c_s, n_s, m_s = rest
    c = pl.program_id(0) % n_chunks

    @pl.when(c == 0)
    def _():
        if has_state:
            c_s[...] = c0_ref[0]
            n_s[...] = n0_ref[0]
            m_s[...] = m0_ref[0]
        else:
            c_s[...] = jnp.zeros_like(c_s)
            n_s[...] = jnp.zeros_like(n_s)
            m_s[...] = jnp.zeros_like(m_s)

    last = c == n_chunks - 1
    for hd in range(M_HEADS):
        cols = slice(hd * M_HEAD_DIM, (hd + 1) * M_HEAD_DIM)
        _mlstm_head(hd, cols, last, bias_ref, q_ref, k_ref, v_ref, o_ref, gi_ref, gfr_ref, gfc_ref, mn_ref,
                    h_ref, c_out, n_out, m_out, c_s, n_s, m_s, chunk, valid_len)


def _mlstm_head(hd, cols, last, bias_ref, q_ref, k_ref, v_ref, o_ref, gi_ref, gfr_ref, gfc_ref, mn_ref,
                h_ref, c_out, n_out, m_out, c_s, n_s, m_s, chunk, valid_len):
    L = chunk
    q = q_ref[:, cols]
    k = k_ref[:, cols] * (M_HEAD_DIM ** -0.5)
    v = v_ref[:, cols]
    b_i = bias_ref[0, hd]
    b_f = bias_ref[1, hd]
    i_row = gi_ref[0, 0, hd:hd + 1, :] + b_i
    lf_row = _log_sigmoid(gfr_ref[0, 0, hd:hd + 1, :] + b_f)
    lf_col = _log_sigmoid(gfc_ref[0, 0, :, hd:hd + 1] + b_f)
    row_id = lax.broadcasted_iota(jnp.int32, (L, L), 0)
    col_id = lax.broadcasted_iota(jnp.int32, (L, L), 1)
    if valid_len < L:
        lane = lax.broadcasted_iota(jnp.int32, (1, L), 1)
        sub = lax.broadcasted_iota(jnp.int32, (L, 1), 0)
        i_row = jnp.where(lane < valid_len, i_row, NEG_INF)
        lf_row = jnp.where(lane < valid_len, lf_row, 0.0)
        lf_col = jnp.where(sub < valid_len, lf_col, 0.0)
    causal = col_id <= row_id
    b_col = jnp.sum(jnp.where(causal, lf_row, 0.0), axis=1, keepdims=True)
    b_row = jnp.sum(jnp.where(row_id <= col_id, lf_col, 0.0), axis=0, keepdims=True)
    m_prev = m_s[hd]
    dmat = jnp.where(causal, b_col - b_row + i_row, NEG_INF)
    inter = b_col + m_prev
    mj = jnp.maximum(inter, jnp.max(dmat, axis=1, keepdims=True))
    s = lax.dot_general(q, k, (((1,), (1,)), ((), ())), preferred_element_type=F32)
    sc = s * jnp.exp(dmat - mj)
    a_int = jnp.exp(inter - mj)
    c_prev = c_s[hd]
    n_prev = n_s[hd]
    qc = lax.dot_general(q, c_prev.astype(BF16), (((1,), (1,)), ((), ())), preferred_element_type=F32)
    num = jnp.dot(sc.astype(BF16), v, preferred_element_type=F32) + a_int * qc
    qn = jnp.sum(q.astype(F32) * n_prev, axis=1, keepdims=True)
    den = jnp.sum(sc, axis=1, keepdims=True) + a_int * qn
    h = num / jnp.maximum(jnp.abs(den), jnp.exp(-mj))
    h = h * _sigmoid(o_ref[:, cols].astype(F32))
    h = h * lax.rsqrt(jnp.mean(h * h, axis=-1, keepdims=True) + EPS) * mn_ref[:, cols]
    h_ref[:, cols] = h.astype(h_ref.dtype)

    bl = jnp.sum(lf_row, axis=1, keepdims=True)
    g_row = bl - b_row + i_row
    m_new = jnp.maximum(bl + m_prev, jnp.max(g_row, axis=1, keepdims=True))
    ws_row = jnp.exp(g_row - m_new)
    a_c = jnp.exp(bl + m_prev - m_new)
    vt = (v.astype(F32).T * ws_row).astype(BF16)
    c_new = a_c * c_prev + jnp.dot(vt, k, preferred_element_type=F32)
    ws8 = jnp.broadcast_to(ws_row, (8, L)).astype(BF16)
    n_new = a_c * n_prev + jnp.dot(ws8, k, preferred_element_type=F32)[0:1]
    c_s[hd] = c_new
    n_s[hd] = n_new
    m_s[hd] = m_new

    @pl.when(last)
    def _():
        c_out[0, hd] = c_new
        n_out[0, hd] = n_new
        m_out[0, hd] = m_new


def mlstm_part(zm, gates_row, gates_col, bias, m_norm, batch, seq, chunk, valid_len, row0, state=None):
    E = M_HEAD_DIM
    nc = seq // chunk
    blk0 = row0 // chunk

    def zspec(col):
        return pl.BlockSpec((chunk, M_WIDTH), lambda i: (blk0 + i, col))

    def per_chunk(shape):
        return pl.BlockSpec(shape, lambda i: (i // nc, i % nc, 0, 0))

    def per_row(shape):
        return pl.BlockSpec(shape, lambda i: (i // nc, 0, 0, 0))

    in_specs = [pl.BlockSpec(memory_space=pltpu.SMEM),
                zspec(0), zspec(1), zspec(2), zspec(3),
                per_chunk((1, 1, M_HEADS, chunk)), per_chunk((1, 1, M_HEADS, chunk)),
                per_chunk((1, 1, chunk, M_HEADS)),
                pl.BlockSpec((1, M_WIDTH), lambda i: (0, 0))]
    gi_row, gf_row = gates_row
    args = [bias, zm, zm, zm, zm, gi_row, gf_row, gates_col, m_norm.reshape(1, M_WIDTH)]
    state_shapes = [(1, M_HEADS, E, E), (1, M_HEADS, 1, E), (1, M_HEADS, 1, 1)]
    if state is not None:
        c0, n0, m0 = state
        in_specs += [per_row(s) for s in state_shapes]
        args += [c0, n0.reshape(batch, M_HEADS, 1, E), m0.reshape(batch, M_HEADS, 1, 1)]
    return dict(
        body=functools.partial(_mlstm_kernel, chunk=chunk, n_chunks=nc, valid_len=valid_len,
                               has_state=state is not None),
        steps=batch * nc,
        in_specs=in_specs,
        args=args,
        out_specs=[pl.BlockSpec((chunk, M_WIDTH), lambda i: (i, 0))] + [per_row(s) for s in state_shapes],
        out_shape=[jax.ShapeDtypeStruct((batch * seq, M_WIDTH), BF16)]
        + [jax.ShapeDtypeStruct((batch,) + s[1:], F32) for s in state_shapes],
        scratch=[pltpu.VMEM(s[1:], F32) for s in state_shapes],
    )


def fused_call(parts, name):
    steps = parts[0]["steps"]
    assert all(p["steps"] == steps for p in parts)
    n_in = [len(p["in_specs"]) for p in parts]
    n_out = [len(p["out_specs"]) for p in parts]
    n_scr = [len(p["scratch"]) for p in parts]

    def body(*refs):
        ins, outs, scr = refs[:sum(n_in)], refs[sum(n_in):sum(n_in) + sum(n_out)], refs[sum(n_in) + sum(n_out):]
        a = b = c = 0
        for p, na, nb, nc in zip(parts, n_in, n_out, n_scr):
            p["body"](*ins[a:a + na], *outs[b:b + nb], *scr[c:c + nc])
            a, b, c = a + na, b + nb, c + nc

    res = pl.pallas_call(
        body,
        grid=(steps,),
        in_specs=[s for p in parts for s in p["in_specs"]],
        out_specs=[s for p in parts for s in p["out_specs"]],
        out_shape=[s for p in parts for s in p["out_shape"]],
        scratch_shapes=[s for p in parts for s in p["scratch"]],
        compiler_params=_cparams(("arbitrary",), FUSED_VMEM_LIMIT),
        name=name,
    )(*[a for p in parts for a in p["args"]])
    grouped, b = [], 0
    for nb in n_out:
        grouped.append(list(res[b:b + nb]))
        b += nb
    return grouped


ATT_BLK = 128
ATT_SUBSTEPS = 4


def _band_block(qb, kcat, vcat, mask):
    s = lax.dot_general(qb, kcat, (((1,), (1,)), ((), ())), preferred_element_type=F32)
    s = jnp.where(mask, s, NEG_INF)
    mx = jnp.max(s, axis=1, keepdims=True)
    p = jnp.exp(s - mx)
    l = jnp.sum(p, axis=1, keepdims=True)
    o = jnp.dot(p.astype(BF16), vcat, preferred_element_type=F32) / l
    return o, mx + jnp.log(l)


def _att_substep_group(sub):
    last = len(A_GROUPS) - 1
    return jnp.minimum(sub, last) if not isinstance(sub, int) else min(sub, last)


def _attn_prompt_kernel(q_ref, k_ref, v_ref, y_ref, *scr, seq):
    o_scr, lse_scr = scr[:3], scr[3:6]
    scale = A_HEAD_DIM ** -0.5
    qi = lax.broadcasted_iota(jnp.int32, (ATT_BLK, ATT_BLK), 0)
    ki = lax.broadcasted_iota(jnp.int32, (ATT_BLK, ATT_BLK), 1)
    cur_mask = ki <= qi
    prev_mask = ki >= qi
    band_mask = jnp.concatenate([prev_mask, cur_mask], axis=1)

    def band_unit(g, r, n):
        dil = A_GROUPS[g][1]

        def rows(first_blk, n_blk):
            if dil == 1:
                return pl.ds(first_blk * ATT_BLK, n_blk * ATT_BLK)
            return pl.ds(r + first_blk * ATT_BLK * dil, n_blk * ATT_BLK, stride=dil)

        qb = (q_ref[rows(n, 1), :] * scale).astype(BF16)
        if n == 0:
            kk, vv, mask = k_ref[rows(0, 1), :], v_ref[rows(0, 1), :], cur_mask
        else:
            kk, vv, mask = k_ref[rows(n - 1, 2), :], v_ref[rows(n - 1, 2), :], band_mask
        o, lse = _band_block(qb, kk.astype(BF16), vv.astype(BF16), mask)
        o_scr[g][rows(n, 1), :] = o
        lse_scr[g][rows(n, 1), :] = lse

    sub = pl.program_id(0) % ATT_SUBSTEPS
    for s in range(ATT_SUBSTEPS):
        g = _att_substep_group(s)
        dil = A_GROUPS[g][1]
        units = [(g, r, n) for r in range(dil) for n in range(seq // dil // ATT_BLK)]
        shares = [t for t in range(ATT_SUBSTEPS) if _att_substep_group(t) == g]
        per_share = -(-len(units) // len(shares))
        k0 = shares.index(s) * per_share

        @pl.when(sub == s)
        def _(units=units[k0:k0 + per_share]):
            for unit in units:
                band_unit(*unit)

    @pl.when(sub == ATT_SUBSTEPS - 1)
    def _():
        l0, l1, l2 = lse_scr[0][...], lse_scr[1][...], lse_scr[2][...]
        mx = jnp.maximum(jnp.maximum(l0, l1), l2)
        w0, w1, w2 = jnp.exp(l0 - mx), jnp.exp(l1 - mx), jnp.exp(l2 - mx)
        y = (w0 * o_scr[0][...] + w1 * o_scr[1][...] + w2 * o_scr[2][...]) / (w0 + w1 + w2)
        y_ref[...] = y.astype(y_ref.dtype)


def attn_prompt_part(zatt, batch, seq):
    nh = A_WIDTH // LANES
    per_b = A_HEADS_PER_GROUP * ATT_SUBSTEPS

    assert len(A_GROUPS) == ATT_SUBSTEPS - 1

    def slot(i):
        return (i // ATT_SUBSTEPS) % A_HEADS_PER_GROUP

    def spec(part):
        return pl.BlockSpec((seq, LANES), lambda i: (
            i // per_b, part * nh + _att_substep_group(i % ATT_SUBSTEPS) * A_HEADS_PER_GROUP + slot(i)))

    return dict(
        body=functools.partial(_attn_prompt_kernel, seq=seq),
        steps=batch * per_b,
        in_specs=[spec(0), spec(1), spec(2)],
        args=[zatt, zatt, zatt],
        out_specs=[pl.BlockSpec((seq, LANES), lambda i: (i // per_b, slot(i)))],
        out_shape=[jax.ShapeDtypeStruct((batch * seq, A_GROUP_WIDTH), BF16)],
        scratch=[pltpu.VMEM((seq, LANES), F32)] * 3 + [pltpu.VMEM((seq, 1), F32)] * 3,
    )


def _window_kernel(new_ref, *refs, t_dec):
    n_buf = 2 * len(A_GROUPS)
    caches, news = refs[:n_buf], refs[n_buf:2 * n_buf]
    y_ref = refs[2 * n_buf]
    outs = refs[2 * n_buf + 1:3 * n_buf + 1]
    bufs = refs[3 * n_buf + 1:4 * n_buf + 1]
    sem_in, sem_out, sem_new = refs[4 * n_buf + 1:]
    b = pl.program_id(0)
    nb = pl.num_programs(0)
    slot = b % 2

    def copy_in(ci, row, sl):
        return pltpu.make_async_copy(caches[ci].at[0, row], bufs[ci].at[sl], sem_in.at[sl, ci])

    def copy_out(ci, row, sl):
        wb = bufs[ci].shape[1]
        return pltpu.make_async_copy(bufs[ci].at[sl, pl.ds(t_dec, wb - t_dec)],
                                     outs[ci].at[0, row, pl.ds(0, wb - t_dec)], sem_out.at[sl, ci])

    def copy_new(ci):
        wb = bufs[ci].shape[1]
        return pltpu.make_async_copy(news[ci], outs[ci].at[0, :, pl.ds(wb - t_dec, t_dec)], sem_new.at[ci])

    @pl.when(b == 0)
    def _():
        for ci in range(n_buf):
            copy_new(ci).start()
            copy_in(ci, 0, 0).start()

    for ci in range(n_buf):
        copy_in(ci, b, slot).wait()
        copy_out(ci, b, slot).start()

    @pl.when(b >= 1)
    def _():
        for ci in range(n_buf):
            copy_out(ci, b - 1, 1 - slot).wait()

    @pl.when(b + 1 < nb)
    def _():
        for ci in range(n_buf):
            copy_in(ci, b + 1, 1 - slot).start()

    _attn_sample(new_ref, bufs, slot, y_ref, t_dec)

    @pl.when(b == nb - 1)
    def _():
        for ci in range(n_buf):
            copy_out(ci, b, slot).wait()
            copy_new(ci).wait()


def _attn_sample(new_ref, bufs, slot, y_ref, t_dec):
    nh = A_WIDTH // LANES
    scale = A_HEAD_DIM ** -0.5
    jj = lax.broadcasted_iota(jnp.int32, (ATT_BLK, 1, 1), 0)
    for t in range(t_dec):
        outs, lses = [], []
        for g, (_, dil) in enumerate(A_GROUPS):
            h0 = g * A_HEADS_PER_GROUP
            q = new_ref[0, t, h0:h0 + A_HEADS_PER_GROUP, :] * scale
            rows = pl.ds(t % dil, ATT_BLK, stride=dil) if dil > 1 else pl.ds(0, ATT_BLK)
            kc = bufs[2 * g][slot, rows]
            vc = bufs[2 * g + 1][slot, rows]
            s_c = jnp.sum(kc * q[None], axis=-1, keepdims=True)
            if t // dil > 0:
                s_c = jnp.where(jj >= t // dil, s_c, NEG_INF)
            mx = jnp.max(s_c, axis=0)
            new_u = [u for u in range(t + 1) if (t - u) % dil == 0]
            s_new = []
            for u in new_u:
                k_u = new_ref[0, u, nh + h0:nh + h0 + A_HEADS_PER_GROUP, :]
                s_u = jnp.sum(k_u * q, axis=-1, keepdims=True)
                s_new.append(s_u)
                mx = jnp.maximum(mx, s_u)
            p_c = jnp.exp(s_c - mx[None])
            l = jnp.sum(p_c, axis=0)
            acc = jnp.sum(p_c * vc, axis=0)
            for u, s_u in zip(new_u, s_new):
                p_u = jnp.exp(s_u - mx)
                v_u = new_ref[0, u, 2 * nh + h0:2 * nh + h0 + A_HEADS_PER_GROUP, :]
                l = l + p_u
                acc = acc + p_u * v_u
            outs.append(acc / l)
            lses.append(mx + jnp.log(l))
        mxg = jnp.maximum(jnp.maximum(lses[0], lses[1]), lses[2])
        ws = [jnp.exp(l_g - mxg) for l_g in lses]
        y = (ws[0] * outs[0] + ws[1] * outs[1] + ws[2] * outs[2]) / (ws[0] + ws[1] + ws[2])
        y_ref[0, t] = y.astype(y_ref.dtype)


def window_part(new_qkv, caches, news):
    batch, t_dec = new_qkv.shape[:2]
    for (win, dil), k_buf in zip(A_GROUPS, caches[::2]):
        assert win == ATT_BLK * dil and k_buf.shape[2] == win
    any_spec = pl.BlockSpec(memory_space=pl.ANY)
    n_buf = len(caches)
    return dict(
        body=functools.partial(_window_kernel, t_dec=t_dec),
        steps=batch,
        in_specs=[pl.BlockSpec((1, t_dec, new_qkv.shape[2], LANES), lambda b: (b, 0, 0, 0))]
        + [any_spec] * (2 * n_buf),
        args=[new_qkv, *caches, *news],
        out_specs=[pl.BlockSpec((1, t_dec, A_HEADS_PER_GROUP, LANES), lambda b: (b, 0, 0, 0))] + [any_spec] * n_buf,
        out_shape=[jax.ShapeDtypeStruct((batch, t_dec, A_HEADS_PER_GROUP, LANES), BF16)]
        + [jax.ShapeDtypeStruct(c.shape, c.dtype) for c in caches],
        scratch=[pltpu.VMEM((2,) + c.shape[2:], c.dtype) for c in caches]
        + [pltpu.SemaphoreType.DMA((2, n_buf)), pltpu.SemaphoreType.DMA((2, n_buf)),
           pltpu.SemaphoreType.DMA((n_buf,))],
    )


MOE_TM = 256
MOE_CHUNK = 8
MOE_BM = 512
MOE_TILE_ROWS = TOP_K * MOE_TM + N_EXPERTS * MOE_CHUNK
ROUTE_EXPERT, ROUTE_POS, ROUTE_GATE = 0, TOP_K, 2 * TOP_K


def _merge_route_kernel(h_a, h_b, ha_a, ha_b, hb_a, hb_b, ga_ref, gb_ref, wpa_ref, wpb_ref, wo_ref, g2_ref,
                        wrh_ref, wrl_ref, br_ref, h1_ref, xt_ref, route_ref, cnt_ref, *, tiles_a):
    tm = h_a.shape[0]
    rows = xt_ref.shape[0]
    first = pl.program_id(0) < tiles_a
    h = jnp.where(first, h_a[...], h_b[...])
    a = jnp.dot(jnp.where(first, ha_a[...], ha_b[...]), wpa_ref[...], preferred_element_type=F32)
    b = jnp.dot(jnp.where(first, hb_a[...], hb_b[...]), wpb_ref[...], preferred_element_type=F32)
    u = _sigmoid(ga_ref[...].astype(F32)) * a + _sigmoid(gb_ref[...].astype(F32)) * b
    h1 = h + jnp.dot(u.astype(BF16), wo_ref[...], preferred_element_type=F32)
    h1_ref[...] = h1
    xn = _rms(h1, g2_ref[...])
    x_hi = xn.astype(BF16)
    x_lo = (xn - x_hi.astype(F32)).astype(BF16)
    logits = (jnp.dot(x_hi, wrh_ref[...], preferred_element_type=F32)
              + jnp.dot(x_lo, wrh_ref[...], preferred_element_type=F32)
              + jnp.dot(x_hi, wrl_ref[...], preferred_element_type=F32)) + br_ref[...]
    lane = lax.broadcasted_iota(jnp.int32, (tm, LANES), 1)
    logits = jnp.where(lane < N_EXPERTS, logits, NEG_INF)
    vals, hots = [], []
    work = logits
    for _ in range(TOP_K):
        mx = jnp.max(work, axis=1, keepdims=True)
        idx = jnp.min(jnp.where(work == mx, lane, LANES), axis=1, keepdims=True)
        hot = lane == idx
        work = jnp.where(hot, NEG_INF, work)
        vals.append(mx)
        hots.append(hot)
    exps = [jnp.exp(v - vals[0]) for v in vals]
    denom = exps[0] + exps[1] + exps[2] + exps[3]
    hot_f = jnp.zeros((tm, LANES), F32)
    for hot in hots:
        hot_f = hot_f + jnp.where(hot, 1.0, 0.0)
    r_id = lax.broadcasted_iota(jnp.int32, (tm, tm), 0)
    c_id = lax.broadcasted_iota(jnp.int32, (tm, tm), 1)
    earlier = jnp.where(c_id < r_id, 1.0, 0.0).astype(BF16)
    rank = jnp.dot(earlier, hot_f.astype(BF16), preferred_element_type=F32)
    cnt = jnp.sum(hot_f, axis=0, keepdims=True)
    padded = jnp.floor((cnt + (MOE_CHUNK - 1)) * (1.0 / MOE_CHUNK)) * MOE_CHUNK
    la = lax.broadcasted_iota(jnp.int32, (LANES, LANES), 0)
    lb = lax.broadcasted_iota(jnp.int32, (LANES, LANES), 1)
    before = jnp.where(la < lb, 1.0, 0.0).astype(BF16)
    run_start = jnp.dot(jnp.broadcast_to(padded, (8, LANES)).astype(BF16), before,
                        preferred_element_type=F32)[0:1]
    pos_all = run_start + rank
    lane_f = lane.astype(F32)
    row_id = lax.broadcasted_iota(jnp.int32, (tm, rows), 1)
    place = jnp.zeros((tm, rows), F32)
    route = jnp.zeros((tm, LANES), F32)
    for k in range(TOP_K):
        e_k = jnp.sum(jnp.where(hots[k], lane_f, 0.0), axis=1, keepdims=True)
        p_k = jnp.sum(jnp.where(hots[k], pos_all, 0.0), axis=1, keepdims=True)
        place = place + jnp.where(row_id == p_k.astype(jnp.int32), 1.0, 0.0)
        route = jnp.where(lane == ROUTE_EXPERT + k, e_k, route)
        route = jnp.where(lane == ROUTE_POS + k, p_k, route)
        route = jnp.where(lane == ROUTE_GATE + k, exps[k] / denom, route)
    route_ref[...] = route
    cnt_ref[0] = cnt
    xt_ref[...] = lax.dot_general(place.astype(BF16), x_hi, (((0,), (0,)), ((), ())),
                                  preferred_element_type=F32)


def merge_route(h, ha, hb, zmg, w_pa, w_pb, w_o, norm2, w_r_hi, w_r_lo, b_r):
    d = h[0].shape[1]
    tm = MOE_TM
    tiles_a = h[0].shape[0] // tm
    n = h[0].shape[0] + h[1].shape[0]
    nt = n // tm

    def full(arr):
        return pl.BlockSpec(arr.shape, lambda i: (0,) * arr.ndim)

    weights = [w_pa, w_pb, w_o, norm2.reshape(1, d), w_r_hi, w_r_lo, b_r]
    return pl.pallas_call(
        functools.partial(_merge_route_kernel, tiles_a=tiles_a),
        grid=(nt,),
        in_specs=_two_source_specs((tm, d), tiles_a) + _two_source_specs((tm, M_WIDTH), tiles_a)
        + _two_source_specs((tm, A_GROUP_WIDTH), tiles_a)
        + [pl.BlockSpec((tm, d), lambda i: (i, 0)),
           pl.BlockSpec((tm, d), lambda i: (i, 1))] + [full(wt) for wt in weights],
        out_specs=[pl.BlockSpec((tm, d), lambda i: (i, 0)),
                   pl.BlockSpec((MOE_TILE_ROWS, d), lambda i: (i, 0)),
                   pl.BlockSpec((tm, LANES), lambda i: (i, 0)),
                   pl.BlockSpec((1, 1, LANES), lambda i: (i, 0, 0))],
        out_shape=[jax.ShapeDtypeStruct((n, d), F32),
                   jax.ShapeDtypeStruct((nt * MOE_TILE_ROWS, d), F32),
                   jax.ShapeDtypeStruct((n, LANES), F32),
                   jax.ShapeDtypeStruct((nt, 1, LANES), F32)],
        compiler_params=_cparams(("parallel",)),
        name="merge_route",
    )(*h, *ha, *hb, zmg, zmg, *weights)


def moe_plan(cnt, n_blocks):
    nt = cnt.shape[0]
    cpb = MOE_BM // MOE_CHUNK
    cpt = MOE_TILE_ROWS // MOE_CHUNK
    i32 = jnp.int32
    cnt = cnt[:, 0, :N_EXPERTS].astype(i32)
    nch = (cnt + MOE_CHUNK - 1) // MOE_CHUNK
    off_incl = jnp.cumsum(nch, axis=1)
    off_ch = off_incl - nch
    seq_incl = jnp.cumsum(nch, axis=0)
    base_ch = seq_incl - nch
    tot = seq_incl[-1]
    nb = (tot + cpb - 1) // cpb
    bend = jnp.cumsum(nb)
    bstart = bend - nb
    n_used = bend[-1:]
    blk = jnp.arange(n_blocks, dtype=i32)
    blk_exp = jnp.minimum(jnp.sum(blk[:, None] >= bend[None, :], axis=1), N_EXPERTS - 1).astype(i32)
    experts = jnp.arange(N_EXPERTS, dtype=i32)
    tiles = jnp.arange(nt, dtype=i32)

    def pick(hot, table):
        return jnp.sum(jnp.where(hot, table, 0), axis=-1).astype(i32)

    hot_e = blk_exp[:, None] == experts[None, :]
    seq_b, off_b, base_b = (pick(hot_e[:, None, :], t[None]) for t in (seq_incl, off_ch, base_ch))
    q = (blk - pick(hot_e, bstart[None]))[:, None] * cpb + jnp.arange(cpb, dtype=i32)[None, :]
    tile = jnp.minimum(jnp.sum(q[:, :, None] >= seq_b[:, None, :], axis=2), nt - 1).astype(i32)
    hot_t = tile[:, :, None] == tiles[None, None, :]
    src = (tile * cpt + pick(hot_t, off_b[:, None, :]) + q - pick(hot_t, base_b[:, None, :])) * MOE_CHUNK
    src = jnp.where(q < pick(hot_e, tot[None])[:, None], src, 0).astype(i32)
    slot = jnp.arange(cpt, dtype=i32)
    e_s = jnp.sum(slot[None, :, None] >= off_incl[:, None, :], axis=2).astype(i32)
    hot_s = e_s[:, :, None] == experts[None, None, :]
    back = (pick(hot_s, bstart[None, None, :]) * cpb + pick(hot_s, base_ch[:, None, :])
            + slot[None, :] - pick(hot_s, off_ch[:, None, :])) * MOE_CHUNK
    back = jnp.where(e_s < N_EXPERTS, back, 0).astype(i32)
    used = nb > 0
    parity = (jnp.cumsum(used.astype(i32)) - 1) % 2
    later = used[None, :] & (experts[None, :] > experts[:, None])
    succ = jnp.min(jnp.where(later, experts[None, :], N_EXPERTS), axis=1)
    w_meta = jnp.stack([pick(hot_e, parity[None]), pick(hot_e, succ[None])]).astype(i32).reshape(-1)
    return blk_exp, n_used.astype(i32), src.reshape(-1), w_meta, back.reshape(-1)


def _swiglu(h):
    x_glu = jnp.minimum(h[:, :D_FF], SWIGLU_LIMIT)
    x_lin = jnp.clip(h[:, D_FF:], -SWIGLU_LIMIT, SWIGLU_LIMIT)
    return x_glu * _sigmoid(SWIGLU_ALPHA * x_glu) * (x_lin + 1.0)


def _chunk_gather_start(src_hbm, rows_ref, first, dst_ref, sem):
    for c in range(dst_ref.shape[0] // MOE_CHUNK):
        r = pl.multiple_of(rows_ref[first + c], MOE_CHUNK)
        pltpu.make_async_copy(src_hbm.at[pl.ds(r, MOE_CHUNK)],
                              dst_ref.at[pl.ds(c * MOE_CHUNK, MOE_CHUNK)], sem).start()


def _chunk_gather_wait(src_hbm, dst_ref, sem):
    pltpu.make_async_copy(src_hbm.at[pl.ds(0, dst_ref.shape[0])], dst_ref, sem).wait()


def _moe_kernel(blk_exp_ref, n_used_ref, rows_ref, wmeta_ref, xt_hbm, w1_hbm, b1_ref, w2_hbm, b2_ref, y_ref,
                xbuf, w1f, w2f, w1b, w2b, sems, wsems):
    i = pl.program_id(0)
    n_used = n_used_ref[0]
    n_blk = pl.num_programs(0)
    cpb = xbuf.shape[1] // MOE_CHUNK

    def fetch(blk, slot):
        _chunk_gather_start(xt_hbm, rows_ref, blk * cpb, xbuf.at[slot], sems.at[slot])

    def weight_copies(expert, slot):
        return (pltpu.make_async_copy(w1_hbm.at[expert], w1f.at[slot], wsems.at[0, slot]),
                pltpu.make_async_copy(w2_hbm.at[expert], w2f.at[slot], wsems.at[1, slot]))

    @pl.when(jnp.logical_and(i == 0, n_used > 0))
    def _():
        fetch(0, 0)
        for cp in weight_copies(blk_exp_ref[0], wmeta_ref[0]):
            cp.start()

    @pl.when(i < n_used)
    def _():
        slot = i % 2
        fetch(jnp.minimum(i + 1, n_used - 1), 1 - slot)
        _chunk_gather_wait(xt_hbm, xbuf.at[slot], sems.at[slot])
        expert = blk_exp_ref[i]
        changed = jnp.logical_or(i == 0, expert != blk_exp_ref[jnp.maximum(i - 1, 0)])

        @pl.when(changed)
        def _():
            w_slot = wmeta_ref[i]
            succ = wmeta_ref[n_blk + i]
            for cp in weight_copies(expert, w_slot):
                cp.wait()
            w1b[...] = w1f[w_slot].astype(BF16)
            w2b[...] = w2f[w_slot].astype(BF16)

            @pl.when(succ < N_EXPERTS)
            def _():
                for cp in weight_copies(succ, 1 - w_slot):
                    cp.start()

        h = jnp.dot(xbuf[slot].astype(BF16), w1b[...], preferred_element_type=F32) + b1_ref[0]
        act = _swiglu(h)
        y_ref[...] = jnp.dot(act.astype(BF16), w2b[...], preferred_element_type=F32) + b2_ref[0]

        @pl.when(i == n_used - 1)
        def _():
            _chunk_gather_wait(xt_hbm, xbuf.at[1 - slot], sems.at[1 - slot])

    @pl.when(i >= n_used)
    def _():
        y_ref[...] = jnp.zeros_like(y_ref)


def moe_experts(xt, blk_exp, n_used, chunk_rows, w_meta, w1, b1, w2, b2):
    d = xt.shape[1]
    nblk = blk_exp.shape[0]
    ne = w1.shape[0]
    bm = MOE_BM
    any_spec = pl.BlockSpec(memory_space=pl.ANY)
    grid_spec = pltpu.PrefetchScalarGridSpec(
        num_scalar_prefetch=4,
        grid=(nblk,),
        in_specs=[any_spec,
                  any_spec,
                  pl.BlockSpec((1, 1, 2 * D_FF), lambda i, be, nu, cr, wm: (be[i], 0, 0)),
                  any_spec,
                  pl.BlockSpec((1, 1, d), lambda i, be, nu, cr, wm: (be[i], 0, 0))],
        out_specs=pl.BlockSpec((bm, d), lambda i, be, nu, cr, wm: (i, 0)),
        scratch_shapes=[pltpu.VMEM((2, bm, d), F32),
                        pltpu.VMEM((2, d, 2 * D_FF), w1.dtype), pltpu.VMEM((2, D_FF, d), w2.dtype),
                        pltpu.VMEM((d, 2 * D_FF), BF16), pltpu.VMEM((D_FF, d), BF16),
                        pltpu.SemaphoreType.DMA((2,)), pltpu.SemaphoreType.DMA((2, 2))],
    )
    return pl.pallas_call(
        _moe_kernel,
        grid_spec=grid_spec,
        out_shape=jax.ShapeDtypeStruct((nblk * bm, d), F32),
        compiler_params=_cparams(("arbitrary",)),
        name="moe_experts",
    )(blk_exp, n_used, chunk_rows, w_meta, xt, w1, b1.reshape(ne, 1, -1), w2, b2.reshape(ne, 1, -1))


def _combine_kernel(rows_ref, ys_hbm, h1_ref, route_ref, p_ref, wple_ref, wpg_ref, g3_ref, gf_ref, y_ref,
                    ybuf, sems, *, tile0):
    i = pl.program_id(0)
    n = pl.num_programs(0)
    tm = h1_ref.shape[0]
    rows = ybuf.shape[1]
    cpt = rows // MOE_CHUNK

    def fetch(step, slot):
        _chunk_gather_start(ys_hbm, rows_ref, (tile0 + step) * cpt, ybuf.at[slot], sems.at[slot])

    @pl.when(i == 0)
    def _():
        fetch(0, 0)

    slot = i % 2
    fetch(jnp.minimum(i + 1, n - 1), 1 - slot)
    _chunk_gather_wait(ys_hbm, ybuf.at[slot], sems.at[slot])
    route = route_ref[...]
    row_id = lax.broadcasted_iota(jnp.int32, (tm, rows), 1)
    weight = jnp.zeros((tm, rows), F32)
    for k in range(TOP_K):
        p_k = route[:, ROUTE_POS + k:ROUTE_POS + k + 1].astype(jnp.int32)
        weight = weight + jnp.where(row_id == p_k, route[:, ROUTE_GATE + k:ROUTE_GATE + k + 1], 0.0)
    h2 = h1_ref[...] + jnp.dot(weight.astype(BF16), ybuf[slot].astype(BF16), preferred_element_type=F32)
    ple = jnp.dot(p_ref[...].astype(BF16), wple_ref[...], preferred_element_type=F32)
    gate = _sigmoid(jnp.dot(_rms(h2, g3_ref[...]).astype(BF16), wpg_ref[...], preferred_element_type=F32))
    h3 = h2 + ple * gate
    y_ref[...] = _rms(h3, gf_ref[...])

    @pl.when(i == n - 1)
    def _():
        _chunk_gather_wait(ys_hbm, ybuf.at[1 - slot], sems.at[1 - slot])


def moe_combine(ys, back_rows, h1, route, p, w_ple, w_pg, norm3, norm_f, tile0, n_tiles):
    d = h1.shape[1]
    tm = MOE_TM

    def full(arr):
        return pl.BlockSpec(arr.shape, lambda i, br: (0,) * arr.ndim)

    weights = [w_ple, w_pg, norm3.reshape(1, d), norm_f.reshape(1, d)]
    grid_spec = pltpu.PrefetchScalarGridSpec(
        num_scalar_prefetch=1,
        grid=(n_tiles,),
        in_specs=[pl.BlockSpec(memory_space=pl.ANY),
                  pl.BlockSpec((tm, d), lambda i, br: (tile0 + i, 0)),
                  pl.BlockSpec((tm, LANES), lambda i, br: (tile0 + i, 0)),
                  pl.BlockSpec((tm, PLE_DIM), lambda i, br: (i, 0))] + [full(wt) for wt in weights],
        out_specs=pl.BlockSpec((tm, d), lambda i, br: (i, 0)),
        scratch_shapes=[pltpu.VMEM((2, MOE_TILE_ROWS, d), F32), pltpu.SemaphoreType.DMA((2,))],
    )
    return pl.pallas_call(
        functools.partial(_combine_kernel, tile0=tile0),
        grid_spec=grid_spec,
        out_shape=jax.ShapeDtypeStruct((n_tiles * tm, d), F32),
        compiler_params=_cparams(("arbitrary",)),
        name="moe_combine",
    )(back_rows, ys, h1, route, p, *weights)


def _gate_layouts(zg, batch, seq, chunk):
    nc = seq // chunk
    g = zg[:, :2 * M_HEADS].reshape(batch, nc, chunk, 2, M_HEADS)
    rows = jnp.transpose(g, (3, 0, 1, 4, 2))
    return (rows[0], rows[1]), g[:, :, :, 1, :]


def kernel(x_prompt, x_sample, state_C, state_n, state_m, cache_k0, cache_v0, cache_k1, cache_v1, cache_k2, cache_v2, p_prompt, p_sample, norm1, w_in, b_igate, b_fgate, m_norm, w_pa, w_pb, w_o, norm2, w_router, b_router, w1, b1, w2, b2, norm3, w_ple, w_ple_gate, norm_f):
    bp, seq, d = x_prompt.shape
    bs, t_dec, _ = x_sample.shape
    n_p, n_s = bp * seq, bs * t_dec
    n_all = n_p + n_s
    x_p, x_s = x_prompt.reshape(n_p, d), x_sample.reshape(n_s, d)

    w = w_in[0]
    c_gate = 4 * M_WIDTH
    c_att = c_gate + 2 * M_HEADS
    c_mg = c_att + 3 * A_WIDTH
    w_m = w[:, :c_gate].astype(BF16)
    w_gate = jnp.pad(w[:, c_gate:c_att], ((0, 0), (0, LANES - 2 * M_HEADS))).astype(BF16)
    w_att = w[:, c_att:c_mg].astype(BF16)
    w_mg = w[:, c_mg:].astype(BF16)
    rope = rope_tables(np.concatenate([np.arange(seq), np.tile(np.arange(t_dec) + PAST_LEN, bs)]))
    tiles_p, tiles_seq = n_p // PROJ_TM, seq // PROJ_TM

    def rope_block(i):
        return jnp.where(i < tiles_p, i % tiles_seq, tiles_seq + i - tiles_p)

    zm, zgate = norm_proj(x_p, x_s, norm1[0], w_m, BF16, 4 * M_WIDTH, w_extra=w_gate)
    zatt = norm_proj(x_p, x_s, norm1[0], w_att, F32, A_WIDTH, rope=rope, n_rope_blocks=2, rope_block=rope_block)
    zmg = norm_proj(x_p, x_s, norm1[0], w_mg, BF16, 2048)

    kv_p = kv_out(zatt, bp, seq)

    z_new = zatt[n_p:]
    caches = (cache_k0, cache_v0, cache_k1, cache_v1, cache_k2, cache_v2)
    news = []
    for g in range(len(A_GROUPS)):
        for part in range(2):
            col = (1 + part) * A_WIDTH + g * A_GROUP_WIDTH
            news.append(z_new[:, col:col + A_GROUP_WIDTH].reshape(bs, t_dec, A_HEADS_PER_GROUP, A_HEAD_DIM))
    bias = jnp.stack([b_igate[0], b_fgate[0]])
    chunk = 128
    grow_p, gcol_p = _gate_layouts(zgate[:n_p], bp, seq, chunk)
    t_pad = 8
    zm_s = jnp.pad(zm[n_p:].reshape(bs, t_dec, -1), ((0, 0), (0, t_pad - t_dec), (0, 0))).reshape(bs * t_pad, -1)
    zg_s = jnp.pad(zgate[n_p:].reshape(bs, t_dec, -1), ((0, 0), (0, t_pad - t_dec), (0, 0))).reshape(bs * t_pad, -1)
    grow_s, gcol_s = _gate_layouts(zg_s, bs, t_pad, t_pad)
    (hb_s, *kv_s), (ha_p, c_p, nn_p, m_p), (ha_s, c_s, nn_s, m_s), (hb_p,) = fused_call([
        window_part(z_new.reshape(bs, t_dec, 3 * A_WIDTH // LANES, LANES), caches, news),
        mlstm_part(zm, grow_p, gcol_p, bias, m_norm[0], bp, seq, chunk, chunk, 0),
        mlstm_part(zm_s, grow_s, gcol_s, bias, m_norm[0], bs, t_pad, t_pad, t_dec, 0,
                   state=(state_C[0], state_n[0], state_m[0])),
        attn_prompt_part(zatt, bp, seq),
    ], "mixers")
    nn_p, nn_s = nn_p.reshape(bp, M_HEADS, M_HEAD_DIM), nn_s.reshape(bs, M_HEADS, M_HEAD_DIM)
    m_p, m_s = m_p.reshape(bp, M_HEADS), m_s.reshape(bs, M_HEADS)
    ha_s = ha_s.reshape(bs, t_pad, M_WIDTH)[:, :t_dec].reshape(n_s, M_WIDTH)
    hb_s = hb_s.reshape(n_s, A_GROUP_WIDTH)

    w_r = jnp.pad(w_router[0], ((0, 0), (0, LANES - N_EXPERTS)))
    w_r_hi = w_r.astype(BF16)
    w_r_lo = (w_r - w_r_hi.astype(F32)).astype(BF16)
    b_r = jnp.pad(b_router[0], (0, LANES - N_EXPERTS)).reshape(1, LANES)
    h1, xt, route, cnt = merge_route((x_p, x_s), (ha_p, ha_s), (hb_p, hb_s), zmg, w_pa[0].astype(BF16),
                                     w_pb[0].astype(BF16), w_o[0].astype(BF16), norm2[0], w_r_hi, w_r_lo, b_r)
    nt = n_all // MOE_TM
    n_blocks = nt * MOE_TILE_ROWS // MOE_BM + N_EXPERTS + 1
    blk_exp, n_used, src_rows, w_meta, back_rows = moe_plan(cnt, n_blocks)
    ys = moe_experts(xt, blk_exp, n_used, src_rows, w_meta, w1[0], b1[0], w2[0], b2[0])
    tail_w = (w_ple[0].astype(BF16), w_ple_gate[0].astype(BF16), norm3[0], norm_f)
    nt_p = n_p // MOE_TM
    y_p = moe_combine(ys, back_rows, h1, route, p_prompt[0].reshape(n_p, PLE_DIM), *tail_w, 0, nt_p)
    y_s = moe_combine(ys, back_rows, h1, route, p_sample[0].reshape(n_s, PLE_DIM), *tail_w, nt_p, nt - nt_p)

    return (y_p.reshape(bp, seq, d), y_s.reshape(bs, t_dec, d),
            c_p[None], nn_p[None], m_p[None], *kv_p,
            c_s[None], nn_s[None], m_s[None], *kv_s)
```

```python
import functools

import jax
import jax.numpy as jnp
import numpy as np
from jax import lax
from jax.experimental import pallas as pl
from jax.experimental.pallas import tpu as pltpu

F32 = jnp.float32
BF16 = jnp.bfloat16

D_MODEL = 1024
PAST_LEN = 8192
M_HEADS = 4
M_HEAD_DIM = 256
M_WIDTH = M_HEADS * M_HEAD_DIM
A_GROUPS = ((128, 1), (512, 4), (2048, 16))
A_HEADS_PER_GROUP = 4
A_HEAD_DIM = 128
A_GROUP_WIDTH = A_HEADS_PER_GROUP * A_HEAD_DIM
A_WIDTH = len(A_GROUPS) * A_GROUP_WIDTH
ROPE_THETA = 500000.0
ROPE_DIM = A_HEAD_DIM // 4
N_EXPERTS = 32
TOP_K = 4
D_FF = D_MODEL
SWIGLU_ALPHA = 1.702
SWIGLU_LIMIT = 7.0
PLE_DIM = 256
EPS = 1e-6

LANES = 128
VMEM_LIMIT = 56 * 1024 * 1024
FUSED_VMEM_LIMIT = 60 * 1024 * 1024
NEG_INF = float("-inf")
PROJ_TM = 512


def _cparams(sem, vmem_limit=VMEM_LIMIT):
    return pltpu.CompilerParams(dimension_semantics=sem, vmem_limit_bytes=vmem_limit)


def _rms(x, gain):
    return x * lax.rsqrt(jnp.mean(x * x, axis=-1, keepdims=True) + EPS) * gain


def _sigmoid(x):
    return 1.0 / (1.0 + jnp.exp(-x))


def _log_sigmoid(x):
    return jnp.minimum(x, 0.0) - jnp.log(1.0 + jnp.exp(-jnp.abs(x)))


def _two_source_specs(block, tiles_a, ix=lambda fn: fn):
    return [pl.BlockSpec(block, ix(lambda i, *_: (jnp.minimum(i, tiles_a - 1), 0))),
            pl.BlockSpec(block, ix(lambda i, *_: (jnp.maximum(i - tiles_a, 0), 0)))]


def _norm_proj_kernel(xa_ref, xb_ref, g_ref, w_ref, *rest, n_rope_blocks, heads_per_block, has_extra, col_axis,
                      tiles_a):
    rest = list(rest)
    if n_rope_blocks:
        cos_ref, sinm_ref, sinp_ref = rest[:3]
        rest = rest[3:]
    if has_extra:
        we_ref, o_ref, oe_ref = rest
    else:
        o_ref, = rest
    j = pl.program_id(col_axis)
    x = jnp.where(pl.program_id(1 - col_axis) < tiles_a, xa_ref[...], xb_ref[...])
    xn = _rms(x, g_ref[...]).astype(BF16)
    z = jnp.dot(xn, w_ref[...], preferred_element_type=F32)
    if has_extra:
        oe_ref[...] = jnp.dot(xn, we_ref[...], preferred_element_type=F32)
    if n_rope_blocks:
        @pl.when(j < n_rope_blocks)
        def _():
            cosf, sinm, sinp = cos_ref[...], sinm_ref[...], sinp_ref[...]
            for h in range(heads_per_block):
                zh = z[:, h * LANES:(h + 1) * LANES]
                rot = (zh * cosf + pltpu.roll(zh, LANES - ROPE_DIM // 2, 1) * sinm
                       + pltpu.roll(zh, ROPE_DIM // 2, 1) * sinp)
                o_ref[:, h * LANES:(h + 1) * LANES] = rot.astype(o_ref.dtype)

        @pl.when(j >= n_rope_blocks)
        def _():
            o_ref[...] = z.astype(o_ref.dtype)
    else:
        o_ref[...] = z.astype(o_ref.dtype)


def norm_proj(xa, xb, gain, w, out_dtype, tn, rope=None, n_rope_blocks=0, rope_block=None, w_extra=None):
    d = xa.shape[1]
    tm = PROJ_TM
    tiles_a = xa.shape[0] // tm
    n = xa.shape[0] + xb.shape[0]
    ncol = w.shape[1]
    rows_outer = w_extra is not None
    grid = (n // tm, ncol // tn) if rows_outer else (ncol // tn, n // tm)

    def ix(fn):
        return (lambda i, j: fn(i, j)) if rows_outer else (lambda j, i: fn(i, j))

    in_specs = _two_source_specs((tm, d), tiles_a, ix) + [
        pl.BlockSpec((1, d), ix(lambda i, j: (0, 0))),
        pl.BlockSpec((d, tn), ix(lambda i, j: (0, j)))]
    args = [xa, xb, gain.reshape(1, d), w]
    if n_rope_blocks:
        in_specs += [pl.BlockSpec((tm, LANES), ix(lambda i, j: (rope_block(i), 0)))] * 3
        args += list(rope)
    out_specs = pl.BlockSpec((tm, tn), ix(lambda i, j: (i, j)))
    out_shape = jax.ShapeDtypeStruct((n, ncol), out_dtype)
    if w_extra is not None:
        in_specs.append(pl.BlockSpec((d, LANES), ix(lambda i, j: (0, 0))))
        args.append(w_extra)
        out_specs = [out_specs, pl.BlockSpec((tm, LANES), ix(lambda i, j: (i, 0)))]
        out_shape = [out_shape, jax.ShapeDtypeStruct((n, LANES), F32)]
    return pl.pallas_call(
        functools.partial(_norm_proj_kernel, n_rope_blocks=n_rope_blocks, heads_per_block=tn // LANES,
                          has_extra=w_extra is not None, col_axis=1 if rows_outer else 0, tiles_a=tiles_a),
        grid=grid,
        in_specs=in_specs,
        out_specs=out_specs,
        out_shape=out_shape,
        compiler_params=_cparams(("arbitrary", "arbitrary")),
        name="norm_proj",
    )(*args)


def rope_tables(pos):
    half = ROPE_DIM // 2
    inv = (1.0 / (np.float32(ROPE_THETA) ** (np.arange(0, ROPE_DIM, 2, dtype=np.float32) / ROPE_DIM))).astype(np.float32)
    ang = (pos.astype(np.float32)[:, None] * inv[None, :]).astype(np.float64)
    cos, sin = np.cos(ang), np.sin(ang)
    n = pos.shape[0]
    ones = np.ones((n, LANES - ROPE_DIM))
    zeros = np.zeros((n, LANES - ROPE_DIM))
    zh = np.zeros((n, half))
    cosf = np.concatenate([cos, cos, ones], axis=1).astype(np.float32)
    sinm = np.concatenate([-sin, zh, zeros], axis=1).astype(np.float32)
    sinp = np.concatenate([zh, sin, zeros], axis=1).astype(np.float32)
    return jnp.asarray(cosf), jnp.asarray(sinm), jnp.asarray(sinp)


KV_ROWS = 512


def _kv_out_kernel(zk_ref, zv_ref, *outs, seq):
    s = pl.program_id(1)
    last = s == pl.num_programs(1) - 1
    for g, (win, _) in enumerate(A_GROUPS):
        keep = min(win, seq)
        for z_ref, o_ref in ((zk_ref, outs[2 * g]), (zv_ref, outs[2 * g + 1])):
            def write(z_ref=z_ref, o_ref=o_ref, rows=min(keep, KV_ROWS), g=g):
                for h in range(A_HEADS_PER_GROUP):
                    col = (g * A_HEADS_PER_GROUP + h) * LANES
                    o_ref[0, 0, :, h, :] = z_ref[KV_ROWS - rows:, col:col + LANES]

            if keep >= seq:
                write()
            else:
                assert keep <= KV_ROWS
                pl.when(last)(write)


def kv_out(zatt, batch, seq):
    steps = seq // KV_ROWS
    in_specs = [pl.BlockSpec((KV_ROWS, A_WIDTH), lambda b, s: (b * steps + s, 1)),
                pl.BlockSpec((KV_ROWS, A_WIDTH), lambda b, s: (b * steps + s, 2))]
    out_specs, out_shape = [], []
    for win, _ in A_GROUPS:
        keep = min(win, seq)
        rows = min(keep, KV_ROWS)
        idx = (lambda b, s: (0, b, s, 0, 0)) if keep >= seq else (lambda b, s: (0, b, 0, 0, 0))
        for _ in range(2):
            out_specs.append(pl.BlockSpec((1, 1, rows, A_HEADS_PER_GROUP, A_HEAD_DIM), idx))
            out_shape.append(jax.ShapeDtypeStruct((1, batch, keep, A_HEADS_PER_GROUP, A_HEAD_DIM), F32))
    return pl.pallas_call(
        functools.partial(_kv_out_kernel, seq=seq),
        grid=(batch, steps),
        in_specs=in_specs,
        out_specs=out_specs,
        out_shape=out_shape,
        compiler_params=_cparams(("arbitrary", "arbitrary")),
        name="kv_out",
    )(zatt, zatt)


def _mlstm_kernel(bias_ref, q_ref, k_ref, v_ref, o_ref, gi_ref, gfr_ref, gfc_ref, mn_ref, *rest,
                  chunk, n_chunks, valid_len, has_state):
    if has_state:
        c0_ref, n0_ref, m0_ref, h_ref, c_out, n_out, m_out, c_s, n_s, m_s = rest
    else:
        h_ref, c_out, n_out, m_out, c_s, n_s, m_s = rest
    c = pl.program_id(0) % n_chunks

    @pl.when(c == 0)
    def _():
        if has_state:
            c_s[...] = c0_ref[0]
            n_s[...] = n0_ref[0]
            m_s[...] = m0_ref[0]
        else:
            c_s[...] = jnp.zeros_like(c_s)
            n_s[...] = jnp.zeros_like(n_s)
            m_s[...] = jnp.zeros_like(m_s)

    last = c == n_chunks - 1
    for hd in range(M_HEADS):
        cols = slice(hd * M_HEAD_DIM, (hd + 1) * M_HEAD_DIM)
        _mlstm_head(hd, cols, last, bias_ref, q_ref, k_ref, v_ref, o_ref, gi_ref, gfr_ref, gfc_ref, mn_ref,
                    h_ref, c_out, n_out, m_out, c_s, n_s, m_s, chunk, valid_len)


def _mlstm_head(hd, cols, last, bias_ref, q_ref, k_ref, v_ref, o_ref, gi_ref, gfr_ref, gfc_ref, mn_ref,
                h_ref, c_out, n_out, m_out, c_s, n_s, m_s, chunk, valid_len):
    L = chunk
    q = q_ref[:, cols]
    k = k_ref[:, cols] * (M_HEAD_DIM ** -0.5)
    v = v_ref[:, cols]
    b_i = bias_ref[0, hd]
    b_f = bias_ref[1, hd]
    i_row = gi_ref[0, 0, hd:hd + 1, :] + b_i
    lf_row = _log_sigmoid(gfr_ref[0, 0, hd:hd + 1, :] + b_f)
    lf_col = _log_sigmoid(gfc_ref[0, 0, :, hd:hd + 1] + b_f)
    row_id = lax.broadcasted_iota(jnp.int32, (L, L), 0)
    col_id = lax.broadcasted_iota(jnp.int32, (L, L), 1)
    if valid_len < L:
        lane = lax.broadcasted_iota(jnp.int32, (1, L), 1)
        sub = lax.broadcasted_iota(jnp.int32, (L, 1), 0)
        i_row = jnp.where(lane < valid_len, i_row, NEG_INF)
        lf_row = jnp.where(lane < valid_len, lf_row, 0.0)
        lf_col = jnp.where(sub < valid_len, lf_col, 0.0)
    causal = col_id <= row_id
    b_col = jnp.sum(jnp.where(causal, lf_row, 0.0), axis=1, keepdims=True)
    b_row = jnp.sum(jnp.where(row_id <= col_id, lf_col, 0.0), axis=0, keepdims=True)
    m_prev = m_s[hd]
    dmat = jnp.where(causal, b_col - b_row + i_row, NEG_INF)
    inter = b_col + m_prev
    mj = jnp.maximum(inter, jnp.max(dmat, axis=1, keepdims=True))
    s = lax.dot_general(q, k, (((1,), (1,)), ((), ())), preferred_element_type=F32)
    sc = s * jnp.exp(dmat - mj)
    a_int = jnp.exp(inter - mj)
    c_prev = c_s[hd]
    n_prev = n_s[hd]
    qc = lax.dot_general(q, c_prev.astype(BF16), (((1,), (1,)), ((), ())), preferred_element_type=F32)
    num = jnp.dot(sc.astype(BF16), v, preferred_element_type=F32) + a_int * qc
    qn = jnp.sum(q.astype(F32) * n_prev, axis=1, keepdims=True)
    den = jnp.sum(sc, axis=1, keepdims=True) + a_int * qn
    h = num / jnp.maximum(jnp.abs(den), jnp.exp(-mj))
    h = h * _sigmoid(o_ref[:, cols].astype(F32))
    h = h * lax.rsqrt(jnp.mean(h * h, axis=-1, keepdims=True) + EPS) * mn_ref[:, cols]
    h_ref[:, cols] = h.astype(h_ref.dtype)

    bl = jnp.sum(lf_row, axis=1, keepdims=True)
    g_row = bl - b_row + i_row
    m_new = jnp.maximum(bl + m_prev, jnp.max(g_row, axis=1, keepdims=True))
    ws_row = jnp.exp(g_row - m_new)
    a_c = jnp.exp(bl + m_prev - m_new)
    vt = (v.astype(F32).T * ws_row).astype(BF16)
    c_new = a_c * c_prev + jnp.dot(vt, k, preferred_element_type=F32)
    ws8 = jnp.broadcast_to(ws_row, (8, L)).astype(BF16)
    n_new = a_c * n_prev + jnp.dot(ws8, k, preferred_element_type=F32)[0:1]
    c_s[hd] = c_new
    n_s[hd] = n_new
    m_s[hd] = m_new

    @pl.when(last)
    def _():
        c_out[0, hd] = c_new
        n_out[0, hd] = n_new
        m_out[0, hd] = m_new


def mlstm_part(zm, gates_row, gates_col, bias, m_norm, batch, seq, chunk, valid_len, row0, state=None):
    E = M_HEAD_DIM
    nc = seq // chunk
    blk0 = row0 // chunk

    def zspec(col):
        return pl.BlockSpec((chunk, M_WIDTH), lambda i: (blk0 + i, col))

    def per_chunk(shape):
        return pl.BlockSpec(shape, lambda i: (i // nc, i % nc, 0, 0))

    def per_row(shape):
        return pl.BlockSpec(shape, lambda i: (i // nc, 0, 0, 0))

    in_specs = [pl.BlockSpec(memory_space=pltpu.SMEM),
                zspec(0), zspec(1), zspec(2), zspec(3),
                per_chunk((1, 1, M_HEADS, chunk)), per_chunk((1, 1, M_HEADS, chunk)),
                per_chunk((1, 1, chunk, M_HEADS)),
                pl.BlockSpec((1, M_WIDTH), lambda i: (0, 0))]
    gi_row, gf_row = gates_row
    args = [bias, zm, zm, zm, zm, gi_row, gf_row, gates_col, m_norm.reshape(1, M_WIDTH)]
    state_shapes = [(1, M_HEADS, E, E), (1, M_HEADS, 1, E), (1, M_HEADS, 1, 1)]
    if state is not None:
        c0, n0, m0 = state
        in_specs += [per_row(s) for s in state_shapes]
        args += [c0, n0.reshape(batch, M_HEADS, 1, E), m0.reshape(batch, M_HEADS, 1, 1)]
    return dict(
        body=functools.partial(_mlstm_kernel, chunk=chunk, n_chunks=nc, valid_len=valid_len,
                               has_state=state is not None),
        steps=batch * nc,
        in_specs=in_specs,
        args=args,
        out_specs=[pl.BlockSpec((chunk, M_WIDTH), lambda i: (i, 0))] + [per_row(s) for s in state_shapes],
        out_shape=[jax.ShapeDtypeStruct((batch * seq, M_WIDTH), BF16)]
        + [jax.ShapeDtypeStruct((batch,) + s[1:], F32) for s in state_shapes],
        scratch=[pltpu.VMEM(s[1:], F32) for s in state_shapes],
    )


def fused_call(parts, name):
    steps = parts[0]["steps"]
    assert all(p["steps"] == steps for p in parts)
    n_in = [len(p["in_specs"]) for p in parts]
    n_out = [len(p["out_specs"]) for p in parts]
    n_scr = [len(p["scratch"]) for p in parts]

    def body(*refs):
        ins, outs, scr = refs[:sum(n_in)], refs[sum(n_in):sum(n_in) + sum(n_out)], refs[sum(n_in) + sum(n_out):]
        a = b = c = 0
        for p, na, nb, nc in zip(parts, n_in, n_out, n_scr):
            p["body"](*ins[a:a + na], *outs[b:b + nb], *scr[c:c + nc])
            a, b, c = a + na, b + nb, c + nc

    res = pl.pallas_call(
        body,
        grid=(steps,),
        in_specs=[s for p in parts for s in p["in_specs"]],
        out_specs=[s for p in parts for s in p["out_specs"]],
        out_shape=[s for p in parts for s in p["out_shape"]],
        scratch_shapes=[s for p in parts for s in p["scratch"]],
        compiler_params=_cparams(("arbitrary",), FUSED_VMEM_LIMIT),
        name=name,
    )(*[a for p in parts for a in p["args"]])
    grouped, b = [], 0
    for nb in n_out:
        grouped.append(list(res[b:b + nb]))
        b += nb
    return grouped


ATT_BLK = 128
ATT_SUBSTEPS = 4


def _band_block(qb, kcat, vcat, mask):
    s = lax.dot_general(qb, kcat, (((1,), (1,)), ((), ())), preferred_element_type=F32)
    s = jnp.where(mask, s, NEG_INF)
    mx = jnp.max(s, axis=1, keepdims=True)
    p = jnp.exp(s - mx)
    l = jnp.sum(p, axis=1, keepdims=True)
    o = jnp.dot(p.astype(BF16), vcat, preferred_element_type=F32) / l
    return o, mx + jnp.log(l)


def _att_substep_group(sub):
    last = len(A_GROUPS) - 1
    return jnp.minimum(sub, last) if not isinstance(sub, int) else min(sub, last)


def _attn_prompt_kernel(q_ref, k_ref, v_ref, y_ref, *scr, seq):
    o_scr, lse_scr = scr[:3], scr[3:6]
    scale = A_HEAD_DIM ** -0.5
    qi = lax.broadcasted_iota(jnp.int32, (ATT_BLK, ATT_BLK), 0)
    ki = lax.broadcasted_iota(jnp.int32, (ATT_BLK, ATT_BLK), 1)
    cur_mask = ki <= qi
    prev_mask = ki >= qi
    band_mask = jnp.concatenate([prev_mask, cur_mask], axis=1)

    def band_unit(g, r, n):
        dil = A_GROUPS[g][1]

        def rows(first_blk, n_blk):
            if dil == 1:
                return pl.ds(first_blk * ATT_BLK, n_blk * ATT_BLK)
            return pl.ds(r + first_blk * ATT_BLK * dil, n_blk * ATT_BLK, stride=dil)

        qb = (q_ref[rows(n, 1), :] * scale).astype(BF16)
        if n == 0:
            kk, vv, mask = k_ref[rows(0, 1), :], v_ref[rows(0, 1), :], cur_mask
        else:
            kk, vv, mask = k_ref[rows(n - 1, 2), :], v_ref[rows(n - 1, 2), :], band_mask
        o, lse = _band_block(qb, kk.astype(BF16), vv.astype(BF16), mask)
        o_scr[g][rows(n, 1), :] = o
        lse_scr[g][rows(n, 1), :] = lse

    sub = pl.program_id(0) % ATT_SUBSTEPS
    for s in range(ATT_SUBSTEPS):
        g = _att_substep_group(s)
        dil = A_GROUPS[g][1]
        units = [(g, r, n) for r in range(dil) for n in range(seq // dil // ATT_BLK)]
        shares = [t for t in range(ATT_SUBSTEPS) if _att_substep_group(t) == g]
        per_share = -(-len(units) // len(shares))
        k0 = shares.index(s) * per_share

        @pl.when(sub == s)
        def _(units=units[k0:k0 + per_share]):
            for unit in units:
                band_unit(*unit)

    @pl.when(sub == ATT_SUBSTEPS - 1)
    def _():
        l0, l1, l2 = lse_scr[0][...], lse_scr[1][...], lse_scr[2][...]
        mx = jnp.maximum(jnp.maximum(l0, l1), l2)
        w0, w1, w2 = jnp.exp(l0 - mx), jnp.exp(l1 - mx), jnp.exp(l2 - mx)
        y = (w0 * o_scr[0][...] + w1 * o_scr[1][...] + w2 * o_scr[2][...]) / (w0 + w1 + w2)
        y_ref[...] = y.astype(y_ref.dtype)


def attn_prompt_part(zatt, batch, seq):
    nh = A_WIDTH // LANES
    per_b = A_HEADS_PER_GROUP * ATT_SUBSTEPS

    assert len(A_GROUPS) == ATT_SUBSTEPS - 1

    def slot(i):
        return (i // ATT_SUBSTEPS) % A_HEADS_PER_GROUP

    def spec(part):
        return pl.BlockSpec((seq, LANES), lambda i: (
            i // per_b, part * nh + _att_substep_group(i % ATT_SUBSTEPS) * A_HEADS_PER_GROUP + slot(i)))

    return dict(
        body=functools.partial(_attn_prompt_kernel, seq=seq),
        steps=batch * per_b,
        in_specs=[spec(0), spec(1), spec(2)],
        args=[zatt, zatt, zatt],
        out_specs=[pl.BlockSpec((seq, LANES), lambda i: (i // per_b, slot(i)))],
        out_shape=[jax.ShapeDtypeStruct((batch * seq, A_GROUP_WIDTH), BF16)],
        scratch=[pltpu.VMEM((seq, LANES), F32)] * 3 + [pltpu.VMEM((seq, 1), F32)] * 3,
    )


def _window_kernel(new_ref, *refs, t_dec):
    n_buf = 2 * len(A_GROUPS)
    caches, news = refs[:n_buf], refs[n_buf:2 * n_buf]
    y_ref = refs[2 * n_buf]
    outs = refs[2 * n_buf + 1:3 * n_buf + 1]
    bufs = refs[3 * n_buf + 1:4 * n_buf + 1]
    sem_in, sem_out, sem_new = refs[4 * n_buf + 1:]
    b = pl.program_id(0)
    nb = pl.num_programs(0)
    slot = b % 2

    def copy_in(ci, row, sl):
        return pltpu.make_async_copy(caches[ci].at[0, row], bufs[ci].at[sl], sem_in.at[sl, ci])

    def copy_out(ci, row, sl):
        wb = bufs[ci].shape[1]
        return pltpu.make_async_copy(bufs[ci].at[sl, pl.ds(t_dec, wb - t_dec)],
                                     outs[ci].at[0, row, pl.ds(0, wb - t_dec)], sem_out.at[sl, ci])

    def copy_new(ci):
        wb = bufs[ci].shape[1]
        return pltpu.make_async_copy(news[ci], outs[ci].at[0, :, pl.ds(wb - t_dec, t_dec)], sem_new.at[ci])

    @pl.when(b == 0)
    def _():
        for ci in range(n_buf):
            copy_new(ci).start()
            copy_in(ci, 0, 0).start()

    for ci in range(n_buf):
        copy_in(ci, b, slot).wait()
        copy_out(ci, b, slot).start(priority=1)

    @pl.when(b >= 1)
    def _():
        for ci in range(n_buf):
            copy_out(ci, b - 1, 1 - slot).wait()

    @pl.when(b + 1 < nb)
    def _():
        for ci in range(n_buf):
            copy_in(ci, b + 1, 1 - slot).start()

    _attn_sample(new_ref, bufs, slot, y_ref, t_dec)

    @pl.when(b == nb - 1)
    def _():
        for ci in range(n_buf):
            copy_out(ci, b, slot).wait()
            copy_new(ci).wait()


def _attn_sample(new_ref, bufs, slot, y_ref, t_dec):
    nh = A_WIDTH // LANES
    scale = A_HEAD_DIM ** -0.5
    jj = lax.broadcasted_iota(jnp.int32, (ATT_BLK, 1, 1), 0)
    for t in range(t_dec):
        outs, lses = [], []
        for g, (_, dil) in enumerate(A_GROUPS):
            h0 = g * A_HEADS_PER_GROUP
            q = new_ref[0, t, h0:h0 + A_HEADS_PER_GROUP, :] * scale
            rows = pl.ds(t % dil, ATT_BLK, stride=dil) if dil > 1 else pl.ds(0, ATT_BLK)
            kc = bufs[2 * g][slot, rows]
            vc = bufs[2 * g + 1][slot, rows]
            s_c = jnp.sum(kc * q[None], axis=-1, keepdims=True)
            if t // dil > 0:
                s_c = jnp.where(jj >= t // dil, s_c, NEG_INF)
            mx = jnp.max(s_c, axis=0)
            new_u = [u for u in range(t + 1) if (t - u) % dil == 0]
            s_new = []
            for u in new_u:
                k_u = new_ref[0, u, nh + h0:nh + h0 + A_HEADS_PER_GROUP, :]
                s_u = jnp.sum(k_u * q, axis=-1, keepdims=True)
                s_new.append(s_u)
                mx = jnp.maximum(mx, s_u)
            p_c = jnp.exp(s_c - mx[None])
            l = jnp.sum(p_c, axis=0)
            acc = jnp.sum(p_c * vc, axis=0)
            for u, s_u in zip(new_u, s_new):
                p_u = jnp.exp(s_u - mx)
                v_u = new_ref[0, u, 2 * nh + h0:2 * nh + h0 + A_HEADS_PER_GROUP, :]
                l = l + p_u
                acc = acc + p_u * v_u
            outs.append(acc / l)
            lses.append(mx + jnp.log(l))
        mxg = jnp.maximum(jnp.maximum(lses[0], lses[1]), lses[2])
        ws = [jnp.exp(l_g - mxg) for l_g in lses]
        y = (ws[0] * outs[0] + ws[1] * outs[1] + ws[2] * outs[2]) / (ws[0] + ws[1] + ws[2])
        y_ref[0, t] = y.astype(y_ref.dtype)


def window_part(new_qkv, caches, news):
    batch, t_dec = new_qkv.shape[:2]
    for (win, dil), k_buf in zip(A_GROUPS, caches[::2]):
        assert win == ATT_BLK * dil and k_buf.shape[2] == win
    any_spec = pl.BlockSpec(memory_space=pl.ANY)
    n_buf = len(caches)
    return dict(
        body=functools.partial(_window_kernel, t_dec=t_dec),
        steps=batch,
        in_specs=[pl.BlockSpec((1, t_dec, new_qkv.shape[2], LANES), lambda b: (b, 0, 0, 0))]
        + [any_spec] * (2 * n_buf),
        args=[new_qkv, *caches, *news],
        out_specs=[pl.BlockSpec((1, t_dec, A_HEADS_PER_GROUP, LANES), lambda b: (b, 0, 0, 0))] + [any_spec] * n_buf,
        out_shape=[jax.ShapeDtypeStruct((batch, t_dec, A_HEADS_PER_GROUP, LANES), BF16)]
        + [jax.ShapeDtypeStruct(c.shape, c.dtype) for c in caches],
        scratch=[pltpu.VMEM((2,) + c.shape[2:], c.dtype) for c in caches]
        + [pltpu.SemaphoreType.DMA((2, n_buf)), pltpu.SemaphoreType.DMA((2, n_buf)),
           pltpu.SemaphoreType.DMA((n_buf,))],
    )


MOE_TM = 256
MOE_CHUNK = 8
MOE_BM = 512
MOE_TILE_ROWS = TOP_K * MOE_TM + N_EXPERTS * MOE_CHUNK
ROUTE_EXPERT, ROUTE_POS, ROUTE_GATE = 0, TOP_K, 2 * TOP_K


def _merge_route_kernel(h_a, h_b, ha_a, ha_b, hb_a, hb_b, ga_ref, gb_ref, wpa_ref, wpb_ref, wo_ref, g2_ref,
                        wrh_ref, wrl_ref, br_ref, h1_ref, xt_ref, route_ref, cnt_ref, *, tiles_a):
    tm = h_a.shape[0]
    rows = xt_ref.shape[0]
    first = pl.program_id(0) < tiles_a
    h = jnp.where(first, h_a[...], h_b[...])
    a = jnp.dot(jnp.where(first, ha_a[...], ha_b[...]), wpa_ref[...], preferred_element_type=F32)
    b = jnp.dot(jnp.where(first, hb_a[...], hb_b[...]), wpb_ref[...], preferred_element_type=F32)
    u = _sigmoid(ga_ref[...].astype(F32)) * a + _sigmoid(gb_ref[...].astype(F32)) * b
    h1 = h + jnp.dot(u.astype(BF16), wo_ref[...], preferred_element_type=F32)
    h1_ref[...] = h1
    xn = _rms(h1, g2_ref[...])
    x_hi = xn.astype(BF16)
    x_lo = (xn - x_hi.astype(F32)).astype(BF16)
    logits = (jnp.dot(x_hi, wrh_ref[...], preferred_element_type=F32)
              + jnp.dot(x_lo, wrh_ref[...], preferred_element_type=F32)
              + jnp.dot(x_hi, wrl_ref[...], preferred_element_type=F32)) + br_ref[...]
    lane = lax.broadcasted_iota(jnp.int32, (tm, LANES), 1)
    logits = jnp.where(lane < N_EXPERTS, logits, NEG_INF)
    vals, hots = [], []
    work = logits
    for _ in range(TOP_K):
        mx = jnp.max(work, axis=1, keepdims=True)
        idx = jnp.min(jnp.where(work == mx, lane, LANES), axis=1, keepdims=True)
        hot = lane == idx
        work = jnp.where(hot, NEG_INF, work)
        vals.append(mx)
        hots.append(hot)
    exps = [jnp.exp(v - vals[0]) for v in vals]
    denom = exps[0] + exps[1] + exps[2] + exps[3]
    hot_f = jnp.zeros((tm, LANES), F32)
    for hot in hots:
        hot_f = hot_f + jnp.where(hot, 1.0, 0.0)
    r_id = lax.broadcasted_iota(jnp.int32, (tm, tm), 0)
    c_id = lax.broadcasted_iota(jnp.int32, (tm, tm), 1)
    earlier = jnp.where(c_id < r_id, 1.0, 0.0).astype(BF16)
    rank = jnp.dot(earlier, hot_f.astype(BF16), preferred_element_type=F32)
    cnt = jnp.sum(hot_f, axis=0, keepdims=True)
    padded = jnp.floor((cnt + (MOE_CHUNK - 1)) * (1.0 / MOE_CHUNK)) * MOE_CHUNK
    la = lax.broadcasted_iota(jnp.int32, (LANES, LANES), 0)
    lb = lax.broadcasted_iota(jnp.int32, (LANES, LANES), 1)
    before = jnp.where(la < lb, 1.0, 0.0).astype(BF16)
    run_start = jnp.dot(jnp.broadcast_to(padded, (8, LANES)).astype(BF16), before,
                        preferred_element_type=F32)[0:1]
    pos_all = run_start + rank
    lane_f = lane.astype(F32)
    row_id = lax.broadcasted_iota(jnp.int32, (tm, rows), 1)
    place = jnp.zeros((tm, rows), F32)
    route = jnp.zeros((tm, LANES), F32)
    for k in range(TOP_K):
        e_k = jnp.sum(jnp.where(hots[k], lane_f, 0.0), axis=1, keepdims=True)
        p_k = jnp.sum(jnp.where(hots[k], pos_all, 0.0), axis=1, keepdims=True)
        place = jnp.where(row_id == p_k.astype(jnp.int32), 1.0, place)
        route = jnp.where(lane == ROUTE_EXPERT + k, e_k, route)
        route = jnp.where(lane == ROUTE_POS + k, p_k, route)
        route = jnp.where(lane == ROUTE_GATE + k, exps[k] / denom, route)
    route_ref[...] = route
    cnt_ref[0] = cnt
    xt_ref[...] = lax.dot_general(place.astype(BF16), x_hi, (((0,), (0,)), ((), ())),
                                  preferred_element_type=F32)


def merge_route(h, ha, hb, zmg, w_pa, w_pb, w_o, norm2, w_r_hi, w_r_lo, b_r):
    d = h[0].shape[1]
    tm = MOE_TM
    tiles_a = h[0].shape[0] // tm
    n = h[0].shape[0] + h[1].shape[0]
    nt = n // tm

    def full(arr):
        return pl.BlockSpec(arr.shape, lambda i: (0,) * arr.ndim)

    weights = [w_pa, w_pb, w_o, norm2.reshape(1, d), w_r_hi, w_r_lo, b_r]
    return pl.pallas_call(
        functools.partial(_merge_route_kernel, tiles_a=tiles_a),
        grid=(nt,),
        in_specs=_two_source_specs((tm, d), tiles_a) + _two_source_specs((tm, M_WIDTH), tiles_a)
        + _two_source_specs((tm, A_GROUP_WIDTH), tiles_a)
        + [pl.BlockSpec((tm, d), lambda i: (i, 0)),
           pl.BlockSpec((tm, d), lambda i: (i, 1))] + [full(wt) for wt in weights],
        out_specs=[pl.BlockSpec((tm, d), lambda i: (i, 0)),
                   pl.BlockSpec((MOE_TILE_ROWS, d), lambda i: (i, 0)),
                   pl.BlockSpec((tm, LANES), lambda i: (i, 0)),
                   pl.BlockSpec((1, 1, LANES), lambda i: (i, 0, 0))],
        out_shape=[jax.ShapeDtypeStruct((n, d), F32),
                   jax.ShapeDtypeStruct((nt * MOE_TILE_ROWS, d), F32),
                   jax.ShapeDtypeStruct((n, LANES), F32),
                   jax.ShapeDtypeStruct((nt, 1, LANES), F32)],
        compiler_params=_cparams(("parallel",)),
        name="merge_route",
    )(*h, *ha, *hb, zmg, zmg, *weights)


def moe_plan(cnt, n_blocks):
    nt = cnt.shape[0]
    cpb = MOE_BM // MOE_CHUNK
    cpt = MOE_TILE_ROWS // MOE_CHUNK
    i32 = jnp.int32
    cnt = cnt[:, 0, :N_EXPERTS].astype(i32)
    nch = (cnt + MOE_CHUNK - 1) // MOE_CHUNK
    off_incl = jnp.cumsum(nch, axis=1)
    off_ch = off_incl - nch
    seq_incl = jnp.cumsum(nch, axis=0)
    base_ch = seq_incl - nch
    tot = seq_incl[-1]
    nb = (tot + cpb - 1) // cpb
    bend = jnp.cumsum(nb)
    bstart = bend - nb
    n_used = bend[-1:]
    blk = jnp.arange(n_blocks, dtype=i32)
    blk_exp = jnp.minimum(jnp.sum(blk[:, None] >= bend[None, :], axis=1), N_EXPERTS - 1).astype(i32)
    experts = jnp.arange(N_EXPERTS, dtype=i32)
    tiles = jnp.arange(nt, dtype=i32)

    def pick(hot, table):
        return jnp.sum(jnp.where(hot, table, 0), axis=-1).astype(i32)

    hot_e = blk_exp[:, None] == experts[None, :]
    seq_b, off_b, base_b = (pick(hot_e[:, None, :], t[None]) for t in (seq_incl, off_ch, base_ch))
    q = (blk - pick(hot_e, bstart[None]))[:, None] * cpb + jnp.arange(cpb, dtype=i32)[None, :]
    tile = jnp.minimum(jnp.sum(q[:, :, None] >= seq_b[:, None, :], axis=2), nt - 1).astype(i32)
    hot_t = tile[:, :, None] == tiles[None, None, :]
    src = (tile * cpt + pick(hot_t, off_b[:, None, :]) + q - pick(hot_t, base_b[:, None, :])) * MOE_CHUNK
    src = jnp.where(q < pick(hot_e, tot[None])[:, None], src, 0).astype(i32)
    slot = jnp.arange(cpt, dtype=i32)
    e_s = jnp.sum(slot[None, :, None] >= off_incl[:, None, :], axis=2).astype(i32)
    hot_s = e_s[:, :, None] == experts[None, None, :]
    back = (pick(hot_s, bstart[None, None, :]) * cpb + pick(hot_s, base_ch[:, None, :])
            + slot[None, :] - pick(hot_s, off_ch[:, None, :])) * MOE_CHUNK
    back = jnp.where(e_s < N_EXPERTS, back, 0).astype(i32)
    used = nb > 0
    parity = (jnp.cumsum(used.astype(i32)) - 1) % 2
    later = used[None, :] & (experts[None, :] > experts[:, None])
    succ = jnp.min(jnp.where(later, experts[None, :], N_EXPERTS), axis=1)
    w_meta = jnp.stack([pick(hot_e, parity[None]), pick(hot_e, succ[None])]).astype(i32).reshape(-1)
    return blk_exp, n_used.astype(i32), src.reshape(-1), w_meta, back.reshape(-1)


def _swiglu(h):
    x_glu = jnp.minimum(h[:, :D_FF], SWIGLU_LIMIT)
    x_lin = jnp.clip(h[:, D_FF:], -SWIGLU_LIMIT, SWIGLU_LIMIT)
    return x_glu * _sigmoid(SWIGLU_ALPHA * x_glu) * (x_lin + 1.0)


def _chunk_gather_start(src_hbm, rows_ref, first, dst_ref, sem, both_queues=False):
    for c in range(dst_ref.shape[0] // MOE_CHUNK):
        r = pl.multiple_of(rows_ref[first + c], MOE_CHUNK)
        pltpu.make_async_copy(src_hbm.at[pl.ds(r, MOE_CHUNK)],
                              dst_ref.at[pl.ds(c * MOE_CHUNK, MOE_CHUNK)], sem).start(
                                  priority=c % 2 if both_queues else 0)


def _chunk_gather_wait(src_hbm, dst_ref, sem):
    pltpu.make_async_copy(src_hbm.at[pl.ds(0, dst_ref.shape[0])], dst_ref, sem).wait()


def _moe_kernel(blk_exp_ref, n_used_ref, rows_ref, wmeta_ref, xt_hbm, w1_hbm, b1_ref, w2_hbm, b2_ref, y_ref,
                xbuf, w1f, w2f, w1b, w2b, sems, wsems):
    i = pl.program_id(0)
    n_used = n_used_ref[0]
    n_blk = pl.num_programs(0)
    cpb = xbuf.shape[1] // MOE_CHUNK

    def fetch(blk, slot):
        _chunk_gather_start(xt_hbm, rows_ref, blk * cpb, xbuf.at[slot], sems.at[slot])

    def weight_copies(expert, slot):
        return (pltpu.make_async_copy(w1_hbm.at[expert], w1f.at[slot], wsems.at[0, slot]),
                pltpu.make_async_copy(w2_hbm.at[expert], w2f.at[slot], wsems.at[1, slot]))

    @pl.when(jnp.logical_and(i == 0, n_used > 0))
    def _():
        fetch(0, 0)
        for cp in weight_copies(blk_exp_ref[0], wmeta_ref[0]):
            cp.start()

    @pl.when(i < n_used)
    def _():
        slot = i % 2
        fetch(jnp.minimum(i + 1, n_used - 1), 1 - slot)
        _chunk_gather_wait(xt_hbm, xbuf.at[slot], sems.at[slot])
        expert = blk_exp_ref[i]
        changed = jnp.logical_or(i == 0, expert != blk_exp_ref[jnp.maximum(i - 1, 0)])

        @pl.when(changed)
        def _():
            w_slot = wmeta_ref[i]
            succ = wmeta_ref[n_blk + i]
            for cp in weight_copies(expert, w_slot):
                cp.wait()
            w1b[...] = w1f[w_slot].astype(BF16)
            w2b[...] = w2f[w_slot].astype(BF16)

            @pl.when(succ < N_EXPERTS)
            def _():
                for cp in weight_copies(succ, 1 - w_slot):
                    cp.start()

        h = jnp.dot(xbuf[slot].astype(BF16), w1b[...], preferred_element_type=F32) + b1_ref[0]
        act = _swiglu(h)
        y_ref[...] = jnp.dot(act.astype(BF16), w2b[...], preferred_element_type=F32) + b2_ref[0]

        @pl.when(i == n_used - 1)
        def _():
            _chunk_gather_wait(xt_hbm, xbuf.at[1 - slot], sems.at[1 - slot])

    @pl.when(i >= n_used)
    def _():
        y_ref[...] = jnp.zeros_like(y_ref)


def moe_experts(xt, blk_exp, n_used, chunk_rows, w_meta, w1, b1, w2, b2):
    d = xt.shape[1]
    nblk = blk_exp.shape[0]
    ne = w1.shape[0]
    bm = MOE_BM
    any_spec = pl.BlockSpec(memory_space=pl.ANY)
    grid_spec = pltpu.PrefetchScalarGridSpec(
        num_scalar_prefetch=4,
        grid=(nblk,),
        in_specs=[any_spec,
                  any_spec,
                  pl.BlockSpec((1, 1, 2 * D_FF), lambda i, be, nu, cr, wm: (be[i], 0, 0)),
                  any_spec,
                  pl.BlockSpec((1, 1, d), lambda i, be, nu, cr, wm: (be[i], 0, 0))],
        out_specs=pl.BlockSpec((bm, d), lambda i, be, nu, cr, wm: (i, 0)),
        scratch_shapes=[pltpu.VMEM((2, bm, d), F32),
                        pltpu.VMEM((2, d, 2 * D_FF), w1.dtype), pltpu.VMEM((2, D_FF, d), w2.dtype),
                        pltpu.VMEM((d, 2 * D_FF), BF16), pltpu.VMEM((D_FF, d), BF16),
                        pltpu.SemaphoreType.DMA((2,)), pltpu.SemaphoreType.DMA((2, 2))],
    )
    return pl.pallas_call(
        _moe_kernel,
        grid_spec=grid_spec,
        out_shape=jax.ShapeDtypeStruct((nblk * bm, d), F32),
        compiler_params=_cparams(("arbitrary",)),
        name="moe_experts",
    )(blk_exp, n_used, chunk_rows, w_meta, xt, w1, b1.reshape(ne, 1, -1), w2, b2.reshape(ne, 1, -1))


def _combine_kernel(rows_ref, ys_hbm, h1_ref, route_ref, p_ref, wple_ref, wpg_ref, g3_ref, gf_ref, y_ref,
                    ybuf, sems, *, tile0):
    i = pl.program_id(0)
    n = pl.num_programs(0)
    tm = h1_ref.shape[0]
    rows = ybuf.shape[1]
    cpt = rows // MOE_CHUNK

    def fetch(step, slot):
        _chunk_gather_start(ys_hbm, rows_ref, (tile0 + step) * cpt, ybuf.at[slot], sems.at[slot],
                            both_queues=True)

    @pl.when(i == 0)
    def _():
        fetch(0, 0)

    slot = i % 2
    fetch(jnp.minimum(i + 1, n - 1), 1 - slot)
    _chunk_gather_wait(ys_hbm, ybuf.at[slot], sems.at[slot])
    route = route_ref[...]
    row_id = lax.broadcasted_iota(jnp.int32, (tm, rows), 1)
    weight = jnp.zeros((tm, rows), F32)
    for k in range(TOP_K):
        p_k = route[:, ROUTE_POS + k:ROUTE_POS + k + 1].astype(jnp.int32)
        weight = jnp.where(row_id == p_k, route[:, ROUTE_GATE + k:ROUTE_GATE + k + 1], weight)
    h2 = h1_ref[...] + jnp.dot(weight.astype(BF16), ybuf[slot].astype(BF16), preferred_element_type=F32)
    ple = jnp.dot(p_ref[...].astype(BF16), wple_ref[...], preferred_element_type=F32)
    gate = _sigmoid(jnp.dot(_rms(h2, g3_ref[...]).astype(BF16), wpg_ref[...], preferred_element_type=F32))
    h3 = h2 + ple * gate
    y_ref[...] = _rms(h3, gf_ref[...])

    @pl.when(i == n - 1)
    def _():
        _chunk_gather_wait(ys_hbm, ybuf.at[1 - slot], sems.at[1 - slot])


def moe_combine(ys, back_rows, h1, route, p, w_ple, w_pg, norm3, norm_f, tile0, n_tiles):
    d = h1.shape[1]
    tm = MOE_TM

    def full(arr):
        return pl.BlockSpec(arr.shape, lambda i, br: (0,) * arr.ndim)

    weights = [w_ple, w_pg, norm3.reshape(1, d), norm_f.reshape(1, d)]
    grid_spec = pltpu.PrefetchScalarGridSpec(
        num_scalar_prefetch=1,
        grid=(n_tiles,),
        in_specs=[pl.BlockSpec(memory_space=pl.ANY),
                  pl.BlockSpec((tm, d), lambda i, br: (tile0 + i, 0)),
                  pl.BlockSpec((tm, LANES), lambda i, br: (tile0 + i, 0)),
                  pl.BlockSpec((tm, PLE_DIM), lambda i, br: (i, 0))] + [full(wt) for wt in weights],
        out_specs=pl.BlockSpec((tm, d), lambda i, br: (i, 0)),
        scratch_shapes=[pltpu.VMEM((2, MOE_TILE_ROWS, d), F32), pltpu.SemaphoreType.DMA((2,))],
    )
    return pl.pallas_call(
        functools.partial(_combine_kernel, tile0=tile0),
        grid_spec=grid_spec,
        out_shape=jax.ShapeDtypeStruct((n_tiles * tm, d), F32),
        compiler_params=_cparams(("arbitrary",)),
        name="moe_combine",
    )(back_rows, ys, h1, route, p, *weights)


def _gate_layouts(zg, batch, seq, chunk):
    nc = seq // chunk
    g = zg[:, :2 * M_HEADS].reshape(batch, nc, chunk, 2, M_HEADS)
    rows = jnp.transpose(g, (3, 0, 1, 4, 2))
    return (rows[0], rows[1]), g[:, :, :, 1, :]


def kernel(x_prompt, x_sample, state_C, state_n, state_m, cache_k0, cache_v0, cache_k1, cache_v1, cache_k2, cache_v2, p_prompt, p_sample, norm1, w_in, b_igate, b_fgate, m_norm, w_pa, w_pb, w_o, norm2, w_router, b_router, w1, b1, w2, b2, norm3, w_ple, w_ple_gate, norm_f):
    bp, seq, d = x_prompt.shape
    bs, t_dec, _ = x_sample.shape
    n_p, n_s = bp * seq, bs * t_dec
    n_all = n_p + n_s
    x_p, x_s = x_prompt.reshape(n_p, d), x_sample.reshape(n_s, d)

    w = w_in[0]
    c_gate = 4 * M_WIDTH
    c_att = c_gate + 2 * M_HEADS
    c_mg = c_att + 3 * A_WIDTH
    w_m = w[:, :c_gate].astype(BF16)
    w_gate = jnp.pad(w[:, c_gate:c_att], ((0, 0), (0, LANES - 2 * M_HEADS))).astype(BF16)
    w_att = w[:, c_att:c_mg].astype(BF16)
    w_mg = w[:, c_mg:].astype(BF16)
    rope = rope_tables(np.concatenate([np.arange(seq), np.tile(np.arange(t_dec) + PAST_LEN, bs)]))
    tiles_p, tiles_seq = n_p // PROJ_TM, seq // PROJ_TM

    def rope_block(i):
        return jnp.where(i < tiles_p, i % tiles_seq, tiles_seq + i - tiles_p)

    zm, zgate = norm_proj(x_p, x_s, norm1[0], w_m, BF16, 4 * M_WIDTH, w_extra=w_gate)
    zatt = norm_proj(x_p, x_s, norm1[0], w_att, F32, A_WIDTH, rope=rope, n_rope_blocks=2, rope_block=rope_block)
    zmg = norm_proj(x_p, x_s, norm1[0], w_mg, BF16, 2048)

    kv_p = kv_out(zatt, bp, seq)

    z_new = zatt[n_p:]
    caches = (cache_k0, cache_v0, cache_k1, cache_v1, cache_k2, cache_v2)
    news = []
    for g in range(len(A_GROUPS)):
        for part in range(2):
            col = (1 + part) * A_WIDTH + g * A_GROUP_WIDTH
            news.append(z_new[:, col:col + A_GROUP_WIDTH].reshape(bs, t_dec, A_HEADS_PER_GROUP, A_HEAD_DIM))
    bias = jnp.stack([b_igate[0], b_fgate[0]])
    chunk = 128
    grow_p, gcol_p = _gate_layouts(zgate[:n_p], bp, seq, chunk)
    t_pad = 8
    zm_s = jnp.pad(zm[n_p:].reshape(bs, t_dec, -1), ((0, 0), (0, t_pad - t_dec), (0, 0))).reshape(bs * t_pad, -1)
    zg_s = jnp.pad(zgate[n_p:].reshape(bs, t_dec, -1), ((0, 0), (0, t_pad - t_dec), (0, 0))).reshape(bs * t_pad, -1)
    grow_s, gcol_s = _gate_layouts(zg_s, bs, t_pad, t_pad)
    (hb_s, *kv_s), (ha_p, c_p, nn_p, m_p), (ha_s, c_s, nn_s, m_s), (hb_p,) = fused_call([
        window_part(z_new.reshape(bs, t_dec, 3 * A_WIDTH // LANES, LANES), caches, news),
        mlstm_part(zm, grow_p, gcol_p, bias, m_norm[0], bp, seq, chunk, chunk, 0),
        mlstm_part(zm_s, grow_s, gcol_s, bias, m_norm[0], bs, t_pad, t_pad, t_dec, 0,
                   state=(state_C[0], state_n[0], state_m[0])),
        attn_prompt_part(zatt, bp, seq),
    ], "mixers")
    nn_p, nn_s = nn_p.reshape(bp, M_HEADS, M_HEAD_DIM), nn_s.reshape(bs, M_HEADS, M_HEAD_DIM)
    m_p, m_s = m_p.reshape(bp, M_HEADS), m_s.reshape(bs, M_HEADS)
    ha_s = ha_s.reshape(bs, t_pad, M_WIDTH)[:, :t_dec].reshape(n_s, M_WIDTH)
    hb_s = hb_s.reshape(n_s, A_GROUP_WIDTH)

    w_r = jnp.pad(w_router[0], ((0, 0), (0, LANES - N_EXPERTS)))
    w_r_hi = w_r.astype(BF16)
    w_r_lo = (w_r - w_r_hi.astype(F32)).astype(BF16)
    b_r = jnp.pad(b_router[0], (0, LANES - N_EXPERTS)).reshape(1, LANES)
    h1, xt, route, cnt = merge_route((x_p, x_s), (ha_p, ha_s), (hb_p, hb_s), zmg, w_pa[0].astype(BF16),
                                     w_pb[0].astype(BF16), w_o[0].astype(BF16), norm2[0], w_r_hi, w_r_lo, b_r)
    nt = n_all // MOE_TM
    n_blocks = nt * MOE_TILE_ROWS // MOE_BM + N_EXPERTS + 1
    blk_exp, n_used, src_rows, w_meta, back_rows = moe_plan(cnt, n_blocks)
    ys = moe_experts(xt, blk_exp, n_used, src_rows, w_meta, w1[0], b1[0], w2[0], b2[0])
    tail_w = (w_ple[0].astype(BF16), w_ple_gate[0].astype(BF16), norm3[0], norm_f)
    nt_p = n_p // MOE_TM
    y_p = moe_combine(ys, back_rows, h1, route, p_prompt[0].reshape(n_p, PLE_DIM), *tail_w, 0, nt_p)
    y_s = moe_combine(ys, back_rows, h1, route, p_sample[0].reshape(n_s, PLE_DIM), *tail_w, nt_p, nt - nt_p)

    return (y_p.reshape(bp, seq, d), y_s.reshape(bs, t_dec, d),
            c_p[None], nn_p[None], m_p[None], *kv_p,
            c_s[None], nn_s[None], m_s[None], *kv_s)
```

```python
import functools

import jax
import jax.numpy as jnp
import numpy as np
from jax import lax
from jax.experimental import pallas as pl
from jax.experimental.pallas import tpu as pltpu

F32 = jnp.float32
BF16 = jnp.bfloat16

D_MODEL = 1024
PAST_LEN = 8192
M_HEADS = 4
M_HEAD_DIM = 256
M_WIDTH = M_HEADS * M_HEAD_DIM
A_GROUPS = ((128, 1), (512, 4), (2048, 16))
A_HEADS_PER_GROUP = 4
A_HEAD_DIM = 128
A_GROUP_WIDTH = A_HEADS_PER_GROUP * A_HEAD_DIM
A_WIDTH = len(A_GROUPS) * A_GROUP_WIDTH
ROPE_THETA = 500000.0
ROPE_DIM = A_HEAD_DIM // 4
N_EXPERTS = 32
TOP_K = 4
D_FF = D_MODEL
SWIGLU_ALPHA = 1.702
SWIGLU_LIMIT = 7.0
PLE_DIM = 256
EPS = 1e-6

LANES = 128
VMEM_LIMIT = 56 * 1024 * 1024
FUSED_VMEM_LIMIT = 60 * 1024 * 1024
NEG_INF = float("-inf")
PROJ_TM = 512
ROPE_SUB_COLS = 512


def _cparams(sem, vmem_limit=VMEM_LIMIT):
    return pltpu.CompilerParams(dimension_semantics=sem, vmem_limit_bytes=vmem_limit)


def _rms(x, gain):
    return x * lax.rsqrt(jnp.mean(x * x, axis=-1, keepdims=True) + EPS) * gain


def _sigmoid(x):
    return 1.0 / (1.0 + jnp.exp(-x))


def _log_sigmoid(x):
    return jnp.minimum(x, 0.0) - jnp.log(1.0 + jnp.exp(-jnp.abs(x)))


def _two_source_specs(block, tiles_a, ix=lambda fn: fn):
    return [pl.BlockSpec(block, ix(lambda i, *_: (jnp.minimum(i, tiles_a - 1), 0))),
            pl.BlockSpec(block, ix(lambda i, *_: (jnp.maximum(i - tiles_a, 0), 0)))]


def _norm_proj_kernel(xa_ref, xb_ref, g_ref, w_ref, *rest, rope_heads, has_extra, row_axis, tiles_a):
    rest = list(rest)
    if rope_heads:
        cos_ref, sinm_ref, sinp_ref = rest[:3]
        rest = rest[3:]
    if has_extra:
        we_ref, o_ref, oe_ref = rest
    else:
        o_ref, = rest
    x = jnp.where(pl.program_id(row_axis) < tiles_a, xa_ref[...], xb_ref[...])
    xn = _rms(x, g_ref[...]).astype(BF16)
    if has_extra:
        oe_ref[...] = jnp.dot(xn, we_ref[...], preferred_element_type=F32)
    if not rope_heads:
        o_ref[...] = jnp.dot(xn, w_ref[...], preferred_element_type=F32).astype(o_ref.dtype)
        return
    cosf, sinm, sinp = cos_ref[...], sinm_ref[...], sinp_ref[...]
    for c0 in range(0, o_ref.shape[1], ROPE_SUB_COLS):
        z = jnp.dot(xn, w_ref[:, c0:c0 + ROPE_SUB_COLS], preferred_element_type=F32)
        for h in range(ROPE_SUB_COLS // LANES):
            zh = z[:, h * LANES:(h + 1) * LANES]
            if c0 // LANES + h < rope_heads:
                zh = (zh * cosf + pltpu.roll(zh, LANES - ROPE_DIM // 2, 1) * sinm
                      + pltpu.roll(zh, ROPE_DIM // 2, 1) * sinp)
            o_ref[:, c0 + h * LANES:c0 + (h + 1) * LANES] = zh.astype(o_ref.dtype)


def norm_proj(xa, xb, gain, w, out_dtype, tn, rope=None, rope_heads=0, rope_block=None, w_extra=None):
    d = xa.shape[1]
    tm = PROJ_TM
    tiles_a = xa.shape[0] // tm
    n = xa.shape[0] + xb.shape[0]
    ncol = w.shape[1]
    assert not rope_heads or (tn == ncol and ncol % ROPE_SUB_COLS == 0)
    rows_outer = w_extra is not None
    grid = (n // tm, ncol // tn) if rows_outer else (ncol // tn, n // tm)

    def ix(fn):
        return (lambda i, j: fn(i, j)) if rows_outer else (lambda j, i: fn(i, j))

    in_specs = _two_source_specs((tm, d), tiles_a, ix) + [
        pl.BlockSpec((1, d), ix(lambda i, j: (0, 0))),
        pl.BlockSpec((d, tn), ix(lambda i, j: (0, j)))]
    args = [xa, xb, gain.reshape(1, d), w]
    if rope_heads:
        in_specs += [pl.BlockSpec((tm, LANES), ix(lambda i, j: (rope_block(i), 0)))] * 3
        args += list(rope)
    out_specs = pl.BlockSpec((tm, tn), ix(lambda i, j: (i, j)))
    out_shape = jax.ShapeDtypeStruct((n, ncol), out_dtype)
    if w_extra is not None:
        in_specs.append(pl.BlockSpec((d, LANES), ix(lambda i, j: (0, 0))))
        args.append(w_extra)
        out_specs = [out_specs, pl.BlockSpec((tm, LANES), ix(lambda i, j: (i, 0)))]
        out_shape = [out_shape, jax.ShapeDtypeStruct((n, LANES), F32)]
    return pl.pallas_call(
        functools.partial(_norm_proj_kernel, rope_heads=rope_heads, has_extra=w_extra is not None,
                          row_axis=0 if rows_outer else 1, tiles_a=tiles_a),
        grid=grid,
        in_specs=in_specs,
        out_specs=out_specs,
        out_shape=out_shape,
        compiler_params=_cparams(("arbitrary", "arbitrary")),
        name="norm_proj",
    )(*args)


def rope_tables(pos):
    half = ROPE_DIM // 2
    inv = (1.0 / (np.float32(ROPE_THETA) ** (np.arange(0, ROPE_DIM, 2, dtype=np.float32) / ROPE_DIM))).astype(np.float32)
    ang = (pos.astype(np.float32)[:, None] * inv[None, :]).astype(np.float64)
    cos, sin = np.cos(ang), np.sin(ang)
    n = pos.shape[0]
    ones = np.ones((n, LANES - ROPE_DIM))
    zeros = np.zeros((n, LANES - ROPE_DIM))
    zh = np.zeros((n, half))
    cosf = np.concatenate([cos, cos, ones], axis=1).astype(np.float32)
    sinm = np.concatenate([-sin, zh, zeros], axis=1).astype(np.float32)
    sinp = np.concatenate([zh, sin, zeros], axis=1).astype(np.float32)
    return jnp.asarray(cosf), jnp.asarray(sinm), jnp.asarray(sinp)


KV_ROWS = 512


def _kv_out_kernel(zk_ref, zv_ref, *outs, seq):
    s = pl.program_id(1)
    last = s == pl.num_programs(1) - 1
    for g, (win, _) in enumerate(A_GROUPS):
        keep = min(win, seq)
        for z_ref, o_ref in ((zk_ref, outs[2 * g]), (zv_ref, outs[2 * g + 1])):
            def write(z_ref=z_ref, o_ref=o_ref, rows=min(keep, KV_ROWS), g=g):
                for h in range(A_HEADS_PER_GROUP):
                    col = (g * A_HEADS_PER_GROUP + h) * LANES
                    o_ref[0, 0, :, h, :] = z_ref[KV_ROWS - rows:, col:col + LANES]

            if keep >= seq:
                write()
            else:
                assert keep <= KV_ROWS
                pl.when(last)(write)


def kv_out(zatt, batch, seq):
    steps = seq // KV_ROWS
    in_specs = [pl.BlockSpec((KV_ROWS, A_WIDTH), lambda b, s: (b * steps + s, 1)),
                pl.BlockSpec((KV_ROWS, A_WIDTH), lambda b, s: (b * steps + s, 2))]
    out_specs, out_shape = [], []
    for win, _ in A_GROUPS:
        keep = min(win, seq)
        rows = min(keep, KV_ROWS)
        idx = (lambda b, s: (0, b, s, 0, 0)) if keep >= seq else (lambda b, s: (0, b, 0, 0, 0))
        for _ in range(2):
            out_specs.append(pl.BlockSpec((1, 1, rows, A_HEADS_PER_GROUP, A_HEAD_DIM), idx))
            out_shape.append(jax.ShapeDtypeStruct((1, batch, keep, A_HEADS_PER_GROUP, A_HEAD_DIM), F32))
    return pl.pallas_call(
        functools.partial(_kv_out_kernel, seq=seq),
        grid=(batch, steps),
        in_specs=in_specs,
        out_specs=out_specs,
        out_shape=out_shape,
        compiler_params=_cparams(("arbitrary", "arbitrary")),
        name="kv_out",
    )(zatt, zatt)


def _mlstm_kernel(bias_ref, q_ref, k_ref, v_ref, o_ref, gi_ref, gfr_ref, gfc_ref, mn_ref, *rest,
                  chunk, n_chunks, valid_len, has_state):
    if has_state:
        c0_ref, n0_ref, m0_ref, h_ref, c_out, n_out, m_out, c_s, n_s, m_s = rest
    else:
        h_ref, c_out, n_out, m_out, c_s, n_s, m_s = rest
    c = pl.program_id(0) % n_chunks

    @pl.when(c == 0)
    def _():
        if has_state:
            c_s[...] = c0_ref[0]
            n_s[...] = n0_ref[0]
            m_s[...] = m0_ref[0]
        else:
            c_s[...] = jnp.zeros_like(c_s)
            n_s[...] = jnp.zeros_like(n_s)
            m_s[...] = jnp.zeros_like(m_s)

    last = c == n_chunks - 1
    for hd in range(M_HEADS):
        cols = slice(hd * M_HEAD_DIM, (hd + 1) * M_HEAD_DIM)
        _mlstm_head(hd, cols, last, bias_ref, q_ref, k_ref, v_ref, o_ref, gi_ref, gfr_ref, gfc_ref, mn_ref,
                    h_ref, c_out, n_out, m_out, c_s, n_s, m_s, chunk, valid_len)


def _mlstm_head(hd, cols, last, bias_ref, q_ref, k_ref, v_ref, o_ref, gi_ref, gfr_ref, gfc_ref, mn_ref,
                h_ref, c_out, n_out, m_out, c_s, n_s, m_s, chunk, valid_len):
    L = chunk
    q = q_ref[:, cols]
    k = k_ref[:, cols] * (M_HEAD_DIM ** -0.5)
    v = v_ref[:, cols]
    b_i = bias_ref[0, hd]
    b_f = bias_ref[1, hd]
    i_row = gi_ref[0, 0, hd:hd + 1, :] + b_i
    lf_row = _log_sigmoid(gfr_ref[0, 0, hd:hd + 1, :] + b_f)
    lf_col = _log_sigmoid(gfc_ref[0, 0, :, hd:hd + 1] + b_f)
    row_id = lax.broadcasted_iota(jnp.int32, (L, L), 0)
    col_id = lax.broadcasted_iota(jnp.int32, (L, L), 1)
    if valid_len < L:
        lane = lax.broadcasted_iota(jnp.int32, (1, L), 1)
        sub = lax.broadcasted_iota(jnp.int32, (L, 1), 0)
        i_row = jnp.where(lane < valid_len, i_row, NEG_INF)
        lf_row = jnp.where(lane < valid_len, lf_row, 0.0)
        lf_col = jnp.where(sub < valid_len, lf_col, 0.0)
    causal = col_id <= row_id
    b_col = jnp.sum(jnp.where(causal, lf_row, 0.0), axis=1, keepdims=True)
    b_row = jnp.sum(jnp.where(row_id <= col_id, lf_col, 0.0), axis=0, keepdims=True)
    m_prev = m_s[hd]
    dmat = jnp.where(causal, b_col - b_row + i_row, NEG_INF)
    inter = b_col + m_prev
    mj = jnp.maximum(inter, jnp.max(dmat, axis=1, keepdims=True))
    s = lax.dot_general(q, k, (((1,), (1,)), ((), ())), preferred_element_type=F32)
    sc = s * jnp.exp(dmat - mj)
    a_int = jnp.exp(inter - mj)
    c_prev = c_s[hd]
    n_prev = n_s[hd]
    qc = lax.dot_general(q, c_prev.astype(BF16), (((1,), (1,)), ((), ())), preferred_element_type=F32)
    num = jnp.dot(sc.astype(BF16), v, preferred_element_type=F32) + a_int * qc
    qn = jnp.sum(q.astype(F32) * n_prev, axis=1, keepdims=True)
    den = jnp.sum(sc, axis=1, keepdims=True) + a_int * qn
    h = num / jnp.maximum(jnp.abs(den), jnp.exp(-mj))
    h = h * _sigmoid(o_ref[:, cols].astype(F32))
    h = h * lax.rsqrt(jnp.mean(h * h, axis=-1, keepdims=True) + EPS) * mn_ref[:, cols]
    h_ref[:, cols] = h.astype(h_ref.dtype)

    bl = jnp.sum(lf_row, axis=1, keepdims=True)
    g_row = bl - b_row + i_row
    m_new = jnp.maximum(bl + m_prev, jnp.max(g_row, axis=1, keepdims=True))
    ws_row = jnp.exp(g_row - m_new)
    a_c = jnp.exp(bl + m_prev - m_new)
    vt = (v.astype(F32).T * ws_row).astype(BF16)
    c_new = a_c * c_prev + jnp.dot(vt, k, preferred_element_type=F32)
    ws8 = jnp.broadcast_to(ws_row, (8, L)).astype(BF16)
    n_new = a_c * n_prev + jnp.dot(ws8, k, preferred_element_type=F32)[0:1]
    c_s[hd] = c_new
    n_s[hd] = n_new
    m_s[hd] = m_new

    @pl.when(last)
    def _():
        c_out[0, hd] = c_new
        n_out[0, hd] = n_new
        m_out[0, hd] = m_new


def mlstm_part(zm, gates_row, gates_col, bias, m_norm, batch, seq, chunk, valid_len, row0, state=None):
    E = M_HEAD_DIM
    nc = seq // chunk
    blk0 = row0 // chunk

    def zspec(col):
        return pl.BlockSpec((chunk, M_WIDTH), lambda i: (blk0 + i, col))

    def per_chunk(shape):
        return pl.BlockSpec(shape, lambda i: (i // nc, i % nc, 0, 0))

    def per_row(shape):
        return pl.BlockSpec(shape, lambda i: (i // nc, 0, 0, 0))

    in_specs = [pl.BlockSpec(memory_space=pltpu.SMEM),
                zspec(0), zspec(1), zspec(2), zspec(3),
                per_chunk((1, 1, M_HEADS, chunk)), per_chunk((1, 1, M_HEADS, chunk)),
                per_chunk((1, 1, chunk, M_HEADS)),
                pl.BlockSpec((1, M_WIDTH), lambda i: (0, 0))]
    gi_row, gf_row = gates_row
    args = [bias, zm, zm, zm, zm, gi_row, gf_row, gates_col, m_norm.reshape(1, M_WIDTH)]
    state_shapes = [(1, M_HEADS, E, E), (1, M_HEADS, 1, E), (1, M_HEADS, 1, 1)]
    if state is not None:
        c0, n0, m0 = state
        in_specs += [per_row(s) for s in state_shapes]
        args += [c0, n0.reshape(batch, M_HEADS, 1, E), m0.reshape(batch, M_HEADS, 1, 1)]
    return dict(
        body=functools.partial(_mlstm_kernel, chunk=chunk, n_chunks=nc, valid_len=valid_len,
                               has_state=state is not None),
        steps=batch * nc,
        in_specs=in_specs,
        args=args,
        out_specs=[pl.BlockSpec((chunk, M_WIDTH), lambda i: (i, 0))] + [per_row(s) for s in state_shapes],
        out_shape=[jax.ShapeDtypeStruct((batch * seq, M_WIDTH), BF16)]
        + [jax.ShapeDtypeStruct((batch,) + s[1:], F32) for s in state_shapes],
        scratch=[pltpu.VMEM(s[1:], F32) for s in state_shapes],
    )


def fused_call(parts, name):
    steps = parts[0]["steps"]
    assert all(p["steps"] == steps for p in parts)
    n_in = [len(p["in_specs"]) for p in parts]
    n_out = [len(p["out_specs"]) for p in parts]
    n_scr = [len(p["scratch"]) for p in parts]

    def body(*refs):
        ins, outs, scr = refs[:sum(n_in)], refs[sum(n_in):sum(n_in) + sum(n_out)], refs[sum(n_in) + sum(n_out):]
        a = b = c = 0
        for p, na, nb, nc in zip(parts, n_in, n_out, n_scr):
            p["body"](*ins[a:a + na], *outs[b:b + nb], *scr[c:c + nc])
            a, b, c = a + na, b + nb, c + nc

    res = pl.pallas_call(
        body,
        grid=(steps,),
        in_specs=[s for p in parts for s in p["in_specs"]],
        out_specs=[s for p in parts for s in p["out_specs"]],
        out_shape=[s for p in parts for s in p["out_shape"]],
        scratch_shapes=[s for p in parts for s in p["scratch"]],
        compiler_params=_cparams(("arbitrary",), FUSED_VMEM_LIMIT),
        name=name,
    )(*[a for p in parts for a in p["args"]])
    grouped, b = [], 0
    for nb in n_out:
        grouped.append(list(res[b:b + nb]))
        b += nb
    return grouped


ATT_BLK = 128
ATT_SUBSTEPS = 4


def _band_block(qb, kcat, vcat, mask):
    s = lax.dot_general(qb, kcat, (((1,), (1,)), ((), ())), preferred_element_type=F32)
    s = jnp.where(mask, s, NEG_INF)
    mx = jnp.max(s, axis=1, keepdims=True)
    p = jnp.exp(s - mx)
    l = jnp.sum(p, axis=1, keepdims=True)
    o = jnp.dot(p.astype(BF16), vcat, preferred_element_type=F32) / l
    return o, mx + jnp.log(l)


def _att_substep_group(sub):
    last = len(A_GROUPS) - 1
    return jnp.minimum(sub, last) if not isinstance(sub, int) else min(sub, last)


def _attn_prompt_kernel(q_ref, k_ref, v_ref, y_ref, *scr, seq):
    o_scr, lse_scr = scr[:3], scr[3:6]
    scale = A_HEAD_DIM ** -0.5
    qi = lax.broadcasted_iota(jnp.int32, (ATT_BLK, ATT_BLK), 0)
    ki = lax.broadcasted_iota(jnp.int32, (ATT_BLK, ATT_BLK), 1)
    cur_mask = ki <= qi
    prev_mask = ki >= qi
    band_mask = jnp.concatenate([prev_mask, cur_mask], axis=1)

    def band_unit(g, r, n):
        dil = A_GROUPS[g][1]

        def rows(first_blk, n_blk):
            if dil == 1:
                return pl.ds(first_blk * ATT_BLK, n_blk * ATT_BLK)
            return pl.ds(r + first_blk * ATT_BLK * dil, n_blk * ATT_BLK, stride=dil)

        qb = (q_ref[rows(n, 1), :] * scale).astype(BF16)
        if n == 0:
            kk, vv, mask = k_ref[rows(0, 1), :], v_ref[rows(0, 1), :], cur_mask
        else:
            kk, vv, mask = k_ref[rows(n - 1, 2), :], v_ref[rows(n - 1, 2), :], band_mask
        o, lse = _band_block(qb, kk.astype(BF16), vv.astype(BF16), mask)
        o_scr[g][rows(n, 1), :] = o
        lse_scr[g][rows(n, 1), :] = lse

    sub = pl.program_id(0) % ATT_SUBSTEPS
    for s in range(ATT_SUBSTEPS):
        g = _att_substep_group(s)
        dil = A_GROUPS[g][1]
        units = [(g, r, n) for r in range(dil) for n in range(seq // dil // ATT_BLK)]
        shares = [t for t in range(ATT_SUBSTEPS) if _att_substep_group(t) == g]
        per_share = -(-len(units) // len(shares))
        k0 = shares.index(s) * per_share

        @pl.when(sub == s)
        def _(units=units[k0:k0 + per_share]):
            for unit in units:
                band_unit(*unit)

    @pl.when(sub == ATT_SUBSTEPS - 1)
    def _():
        l0, l1, l2 = lse_scr[0][...], lse_scr[1][...], lse_scr[2][...]
        mx = jnp.maximum(jnp.maximum(l0, l1), l2)
        w0, w1, w2 = jnp.exp(l0 - mx), jnp.exp(l1 - mx), jnp.exp(l2 - mx)
        y = (w0 * o_scr[0][...] + w1 * o_scr[1][...] + w2 * o_scr[2][...]) / (w0 + w1 + w2)
        y_ref[...] = y.astype(y_ref.dtype)


def attn_prompt_part(zatt, batch, seq):
    nh = A_WIDTH // LANES
    per_b = A_HEADS_PER_GROUP * ATT_SUBSTEPS

    assert len(A_GROUPS) == ATT_SUBSTEPS - 1

    def slot(i):
        return (i // ATT_SUBSTEPS) % A_HEADS_PER_GROUP

    def spec(part):
        return pl.BlockSpec((seq, LANES), lambda i: (
            i // per_b, part * nh + _att_substep_group(i % ATT_SUBSTEPS) * A_HEADS_PER_GROUP + slot(i)))

    return dict(
        body=functools.partial(_attn_prompt_kernel, seq=seq),
        steps=batch * per_b,
        in_specs=[spec(0), spec(1), spec(2)],
        args=[zatt, zatt, zatt],
        out_specs=[pl.BlockSpec((seq, LANES), lambda i: (i // per_b, slot(i)))],
        out_shape=[jax.ShapeDtypeStruct((batch * seq, A_GROUP_WIDTH), BF16)],
        scratch=[pltpu.VMEM((seq, LANES), F32)] * 3 + [pltpu.VMEM((seq, 1), F32)] * 3,
    )


def _window_kernel(new_ref, *refs, t_dec):
    n_buf = 2 * len(A_GROUPS)
    caches, news = refs[:n_buf], refs[n_buf:2 * n_buf]
    y_ref = refs[2 * n_buf]
    outs = refs[2 * n_buf + 1:3 * n_buf + 1]
    bufs = refs[3 * n_buf + 1:4 * n_buf + 1]
    sem_in, sem_out, sem_new = refs[4 * n_buf + 1:]
    b = pl.program_id(0)
    nb = pl.num_programs(0)
    slot = b % 2

    def copy_in(ci, row, sl):
        return pltpu.make_async_copy(caches[ci].at[0, row], bufs[ci].at[sl], sem_in.at[sl, ci])

    def copy_out(ci, row, sl):
        wb = bufs[ci].shape[1]
        return pltpu.make_async_copy(bufs[ci].at[sl, pl.ds(t_dec, wb - t_dec)],
                                     outs[ci].at[0, row, pl.ds(0, wb - t_dec)], sem_out.at[sl, ci])

    def copy_new(ci):
        wb = bufs[ci].shape[1]
        return pltpu.make_async_copy(news[ci], outs[ci].at[0, :, pl.ds(wb - t_dec, t_dec)], sem_new.at[ci])

    @pl.when(b == 0)
    def _():
        for ci in range(n_buf):
            copy_new(ci).start()
            copy_in(ci, 0, 0).start()

    for ci in range(n_buf):
        copy_in(ci, b, slot).wait()
        copy_out(ci, b, slot).start(priority=1)

    @pl.when(b >= 1)
    def _():
        for ci in range(n_buf):
            copy_out(ci, b - 1, 1 - slot).wait()

    @pl.when(b + 1 < nb)
    def _():
        for ci in range(n_buf):
            copy_in(ci, b + 1, 1 - slot).start()

    _attn_sample(new_ref, bufs, slot, y_ref, t_dec)

    @pl.when(b == nb - 1)
    def _():
        for ci in range(n_buf):
            copy_out(ci, b, slot).wait()
            copy_new(ci).wait()


def _attn_sample(new_ref, bufs, slot, y_ref, t_dec):
    nh = A_WIDTH // LANES
    scale = A_HEAD_DIM ** -0.5
    jj = lax.broadcasted_iota(jnp.int32, (ATT_BLK, 1, 1), 0)
    for t in range(t_dec):
        outs, lses = [], []
        for g, (_, dil) in enumerate(A_GROUPS):
            h0 = g * A_HEADS_PER_GROUP
            q = new_ref[0, t, h0:h0 + A_HEADS_PER_GROUP, :] * scale
            rows = pl.ds(t % dil, ATT_BLK, stride=dil) if dil > 1 else pl.ds(0, ATT_BLK)
            kc = bufs[2 * g][slot, rows]
            vc = bufs[2 * g + 1][slot, rows]
            s_c = jnp.sum(kc * q[None], axis=-1, keepdims=True)
            if t // dil > 0:
                s_c = jnp.where(jj >= t // dil, s_c, NEG_INF)
            mx = jnp.max(s_c, axis=0)
            new_u = [u for u in range(t + 1) if (t - u) % dil == 0]
            s_new = []
            for u in new_u:
                k_u = new_ref[0, u, nh + h0:nh + h0 + A_HEADS_PER_GROUP, :]
                s_u = jnp.sum(k_u * q, axis=-1, keepdims=True)
                s_new.append(s_u)
                mx = jnp.maximum(mx, s_u)
            p_c = jnp.exp(s_c - mx[None])
            l = jnp.sum(p_c, axis=0)
            acc = jnp.sum(p_c * vc, axis=0)
            for u, s_u in zip(new_u, s_new):
                p_u = jnp.exp(s_u - mx)
                v_u = new_ref[0, u, 2 * nh + h0:2 * nh + h0 + A_HEADS_PER_GROUP, :]
                l = l + p_u
                acc = acc + p_u * v_u
            outs.append(acc / l)
            lses.append(mx + jnp.log(l))
        mxg = jnp.maximum(jnp.maximum(lses[0], lses[1]), lses[2])
        ws = [jnp.exp(l_g - mxg) for l_g in lses]
        y = (ws[0] * outs[0] + ws[1] * outs[1] + ws[2] * outs[2]) / (ws[0] + ws[1] + ws[2])
        y_ref[0, t] = y.astype(y_ref.dtype)


def window_part(new_qkv, caches, news):
    batch, t_dec = new_qkv.shape[:2]
    for (win, dil), k_buf in zip(A_GROUPS, caches[::2]):
        assert win == ATT_BLK * dil and k_buf.shape[2] == win
    any_spec = pl.BlockSpec(memory_space=pl.ANY)
    n_buf = len(caches)
    return dict(
        body=functools.partial(_window_kernel, t_dec=t_dec),
        steps=batch,
        in_specs=[pl.BlockSpec((1, t_dec, new_qkv.shape[2], LANES), lambda b: (b, 0, 0, 0))]
        + [any_spec] * (2 * n_buf),
        args=[new_qkv, *caches, *news],
        out_specs=[pl.BlockSpec((1, t_dec, A_HEADS_PER_GROUP, LANES), lambda b: (b, 0, 0, 0))] + [any_spec] * n_buf,
        out_shape=[jax.ShapeDtypeStruct((batch, t_dec, A_HEADS_PER_GROUP, LANES), BF16)]
        + [jax.ShapeDtypeStruct(c.shape, c.dtype) for c in caches],
        scratch=[pltpu.VMEM((2,) + c.shape[2:], c.dtype) for c in caches]
        + [pltpu.SemaphoreType.DMA((2, n_buf)), pltpu.SemaphoreType.DMA((2, n_buf)),
           pltpu.SemaphoreType.DMA((n_buf,))],
    )


MOE_TM = 256
MOE_CHUNK = 8
MOE_BM = 512
MOE_TILE_ROWS = TOP_K * MOE_TM + N_EXPERTS * MOE_CHUNK
ROUTE_EXPERT, ROUTE_POS, ROUTE_GATE = 0, TOP_K, 2 * TOP_K


def _merge_route_kernel(h_a, h_b, ha_a, ha_b, hb_a, hb_b, ga_ref, gb_ref, wpa_ref, wpb_ref, wo_ref, g2_ref,
                        wrh_ref, wrl_ref, br_ref, h1_ref, xt_ref, route_ref, cnt_ref, *, tiles_a):
    tm = h_a.shape[0]
    rows = xt_ref.shape[0]
    first = pl.program_id(0) < tiles_a
    h = jnp.where(first, h_a[...], h_b[...])
    a = jnp.dot(jnp.where(first, ha_a[...], ha_b[...]), wpa_ref[...], preferred_element_type=F32)
    b = jnp.dot(jnp.where(first, hb_a[...], hb_b[...]), wpb_ref[...], preferred_element_type=F32)
    u = _sigmoid(ga_ref[...].astype(F32)) * a + _sigmoid(gb_ref[...].astype(F32)) * b
    h1 = h + jnp.dot(u.astype(BF16), wo_ref[...], preferred_element_type=F32)
    h1_ref[...] = h1
    xn = _rms(h1, g2_ref[...])
    x_hi = xn.astype(BF16)
    x_lo = (xn - x_hi.astype(F32)).astype(BF16)
    logits = (jnp.dot(x_hi, wrh_ref[...], preferred_element_type=F32)
              + jnp.dot(x_lo, wrh_ref[...], preferred_element_type=F32)
              + jnp.dot(x_hi, wrl_ref[...], preferred_element_type=F32)) + br_ref[...]
    lane = lax.broadcasted_iota(jnp.int32, (tm, LANES), 1)
    logits = jnp.where(lane < N_EXPERTS, logits, NEG_INF)
    vals, hots = [], []
    work = logits
    for _ in range(TOP_K):
        mx = jnp.max(work, axis=1, keepdims=True)
        idx = jnp.min(jnp.where(work == mx, lane, LANES), axis=1, keepdims=True)
        hot = lane == idx
        work = jnp.where(hot, NEG_INF, work)
        vals.append(mx)
        hots.append(hot)
    exps = [jnp.exp(v - vals[0]) for v in vals]
    denom = exps[0] + exps[1] + exps[2] + exps[3]
    hot_f = jnp.zeros((tm, LANES), F32)
    for hot in hots:
        hot_f = hot_f + jnp.where(hot, 1.0, 0.0)
    r_id = lax.broadcasted_iota(jnp.int32, (tm, tm), 0)
    c_id = lax.broadcasted_iota(jnp.int32, (tm, tm), 1)
    earlier = jnp.where(c_id < r_id, 1.0, 0.0).astype(BF16)
    rank = jnp.dot(earlier, hot_f.astype(BF16), preferred_element_type=F32)
    cnt = jnp.sum(hot_f, axis=0, keepdims=True)
    padded = jnp.floor((cnt + (MOE_CHUNK - 1)) * (1.0 / MOE_CHUNK)) * MOE_CHUNK
    la = lax.broadcasted_iota(jnp.int32, (LANES, LANES), 0)
    lb = lax.broadcasted_iota(jnp.int32, (LANES, LANES), 1)
    before = jnp.where(la < lb, 1.0, 0.0).astype(BF16)
    run_start = jnp.dot(jnp.broadcast_to(padded, (8, LANES)).astype(BF16), before,
                        preferred_element_type=F32)[0:1]
    pos_all = run_start + rank
    lane_f = lane.astype(F32)
    row_id = lax.broadcasted_iota(jnp.int32, (tm, rows), 1)
    place = jnp.zeros((tm, rows), F32)
    route = jnp.zeros((tm, LANES), F32)
    for k in range(TOP_K):
        e_k = jnp.sum(jnp.where(hots[k], lane_f, 0.0), axis=1, keepdims=True)
        p_k = jnp.sum(jnp.where(hots[k], pos_all, 0.0), axis=1, keepdims=True)
        place = jnp.where(row_id == p_k.astype(jnp.int32), 1.0, place)
        route = jnp.where(lane == ROUTE_EXPERT + k, e_k, route)
        route = jnp.where(lane == ROUTE_POS + k, p_k, route)
        route = jnp.where(lane == ROUTE_GATE + k, exps[k] / denom, route)
    route_ref[...] = route
    cnt_ref[0] = cnt
    xt_ref[...] = lax.dot_general(place.astype(BF16), x_hi, (((0,), (0,)), ((), ())),
                                  preferred_element_type=F32)


def merge_route(h, ha, hb, zmg, w_pa, w_pb, w_o, norm2, w_r_hi, w_r_lo, b_r):
    d = h[0].shape[1]
    tm = MOE_TM
    tiles_a = h[0].shape[0] // tm
    n = h[0].shape[0] + h[1].shape[0]
    nt = n // tm

    def full(arr):
        return pl.BlockSpec(arr.shape, lambda i: (0,) * arr.ndim)

    weights = [w_pa, w_pb, w_o, norm2.reshape(1, d), w_r_hi, w_r_lo, b_r]
    return pl.pallas_call(
        functools.partial(_merge_route_kernel, tiles_a=tiles_a),
        grid=(nt,),
        in_specs=_two_source_specs((tm, d), tiles_a) + _two_source_specs((tm, M_WIDTH), tiles_a)
        + _two_source_specs((tm, A_GROUP_WIDTH), tiles_a)
        + [pl.BlockSpec((tm, d), lambda i: (i, 0)),
           pl.BlockSpec((tm, d), lambda i: (i, 1))] + [full(wt) for wt in weights],
        out_specs=[pl.BlockSpec((tm, d), lambda i: (i, 0)),
                   pl.BlockSpec((MOE_TILE_ROWS, d), lambda i: (i, 0)),
                   pl.BlockSpec((tm, LANES), lambda i: (i, 0)),
                   pl.BlockSpec((1, 1, LANES), lambda i: (i, 0, 0))],
        out_shape=[jax.ShapeDtypeStruct((n, d), F32),
                   jax.ShapeDtypeStruct((nt * MOE_TILE_ROWS, d), F32),
                   jax.ShapeDtypeStruct((n, LANES), F32),
                   jax.ShapeDtypeStruct((nt, 1, LANES), F32)],
        compiler_params=_cparams(("parallel",)),
        name="merge_route",
    )(*h, *ha, *hb, zmg, zmg, *weights)


def moe_plan(cnt, n_blocks):
    nt = cnt.shape[0]
    cpb = MOE_BM // MOE_CHUNK
    cpt = MOE_TILE_ROWS // MOE_CHUNK
    i32 = jnp.int32
    cnt = cnt[:, 0, :N_EXPERTS].astype(i32)
    nch = (cnt + MOE_CHUNK - 1) // MOE_CHUNK
    off_incl = jnp.cumsum(nch, axis=1)
    off_ch = off_incl - nch
    seq_incl = jnp.cumsum(nch, axis=0)
    base_ch = seq_incl - nch
    tot = seq_incl[-1]
    nb = (tot + cpb - 1) // cpb
    bend = jnp.cumsum(nb)
    bstart = bend - nb
    n_used = bend[-1:]
    blk = jnp.arange(n_blocks, dtype=i32)
    blk_exp = jnp.minimum(jnp.sum(blk[:, None] >= bend[None, :], axis=1), N_EXPERTS - 1).astype(i32)
    experts = jnp.arange(N_EXPERTS, dtype=i32)
    tiles = jnp.arange(nt, dtype=i32)

    def pick(hot, table):
        return jnp.sum(jnp.where(hot, table, 0), axis=-1).astype(i32)

    hot_e = blk_exp[:, None] == experts[None, :]
    seq_b, off_b, base_b = (pick(hot_e[:, None, :], t[None]) for t in (seq_incl, off_ch, base_ch))
    q = (blk - pick(hot_e, bstart[None]))[:, None] * cpb + jnp.arange(cpb, dtype=i32)[None, :]
    tile = jnp.minimum(jnp.sum(q[:, :, None] >= seq_b[:, None, :], axis=2), nt - 1).astype(i32)
    hot_t = tile[:, :, None] == tiles[None, None, :]
    src = (tile * cpt + pick(hot_t, off_b[:, None, :]) + q - pick(hot_t, base_b[:, None, :])) * MOE_CHUNK
    src = jnp.where(q < pick(hot_e, tot[None])[:, None], src, 0).astype(i32)
    slot = jnp.arange(cpt, dtype=i32)
    e_s = jnp.sum(slot[None, :, None] >= off_incl[:, None, :], axis=2).astype(i32)
    hot_s = e_s[:, :, None] == experts[None, None, :]
    back = (pick(hot_s, bstart[None, None, :]) * cpb + pick(hot_s, base_ch[:, None, :])
            + slot[None, :] - pick(hot_s, off_ch[:, None, :])) * MOE_CHUNK
    back = jnp.where(e_s < N_EXPERTS, back, 0).astype(i32)
    used = nb > 0
    parity = (jnp.cumsum(used.astype(i32)) - 1) % 2
    later = used[None, :] & (experts[None, :] > experts[:, None])
    succ = jnp.min(jnp.where(later, experts[None, :], N_EXPERTS), axis=1)
    w_meta = jnp.stack([pick(hot_e, parity[None]), pick(hot_e, succ[None])]).astype(i32).reshape(-1)
    return blk_exp, n_used.astype(i32), src.reshape(-1), w_meta, back.reshape(-1)


def _swiglu(h):
    x_glu = jnp.minimum(h[:, :D_FF], SWIGLU_LIMIT)
    x_lin = jnp.clip(h[:, D_FF:], -SWIGLU_LIMIT, SWIGLU_LIMIT)
    return x_glu * _sigmoid(SWIGLU_ALPHA * x_glu) * (x_lin + 1.0)


def _chunk_gather_start(src_hbm, rows_ref, first, dst_ref, sem, both_queues=False):
    for c in range(dst_ref.shape[0] // MOE_CHUNK):
        r = pl.multiple_of(rows_ref[first + c], MOE_CHUNK)
        pltpu.make_async_copy(src_hbm.at[pl.ds(r, MOE_CHUNK)],
                              dst_ref.at[pl.ds(c * MOE_CHUNK, MOE_CHUNK)], sem).start(
                                  priority=c % 2 if both_queues else 0)


def _chunk_gather_wait(src_hbm, dst_ref, sem):
    pltpu.make_async_copy(src_hbm.at[pl.ds(0, dst_ref.shape[0])], dst_ref, sem).wait()


def _moe_kernel(blk_exp_ref, n_used_ref, rows_ref, wmeta_ref, xt_hbm, w1_hbm, b1_ref, w2_hbm, b2_ref, y_ref,
                xbuf, w1f, w2f, w1b, w2b, sems, wsems):
    i = pl.program_id(0)
    n_used = n_used_ref[0]
    n_blk = pl.num_programs(0)
    cpb = xbuf.shape[1] // MOE_CHUNK

    def fetch(blk, slot):
        _chunk_gather_start(xt_hbm, rows_ref, blk * cpb, xbuf.at[slot], sems.at[slot])

    def weight_copies(expert, slot):
        return (pltpu.make_async_copy(w1_hbm.at[expert], w1f.at[slot], wsems.at[0, slot]),
                pltpu.make_async_copy(w2_hbm.at[expert], w2f.at[slot], wsems.at[1, slot]))

    @pl.when(jnp.logical_and(i == 0, n_used > 0))
    def _():
        fetch(0, 0)
        for cp in weight_copies(blk_exp_ref[0], wmeta_ref[0]):
            cp.start()

    @pl.when(i < n_used)
    def _():
        slot = i % 2
        fetch(jnp.minimum(i + 1, n_used - 1), 1 - slot)
        _chunk_gather_wait(xt_hbm, xbuf.at[slot], sems.at[slot])
        expert = blk_exp_ref[i]
        changed = jnp.logical_or(i == 0, expert != blk_exp_ref[jnp.maximum(i - 1, 0)])

        @pl.when(changed)
        def _():
            w_slot = wmeta_ref[i]
            succ = wmeta_ref[n_blk + i]
            for cp in weight_copies(expert, w_slot):
                cp.wait()
            w1b[...] = w1f[w_slot].astype(BF16)
            w2b[...] = w2f[w_slot].astype(BF16)

            @pl.when(succ < N_EXPERTS)
            def _():
                for cp in weight_copies(succ, 1 - w_slot):
                    cp.start()

        h = jnp.dot(xbuf[slot].astype(BF16), w1b[...], preferred_element_type=F32) + b1_ref[0]
        act = _swiglu(h)
        y_ref[...] = jnp.dot(act.astype(BF16), w2b[...], preferred_element_type=F32) + b2_ref[0]

        @pl.when(i == n_used - 1)
        def _():
            _chunk_gather_wait(xt_hbm, xbuf.at[1 - slot], sems.at[1 - slot])

    @pl.when(i >= n_used)
    def _():
        y_ref[...] = jnp.zeros_like(y_ref)


def moe_experts(xt, blk_exp, n_used, chunk_rows, w_meta, w1, b1, w2, b2):
    d = xt.shape[1]
    nblk = blk_exp.shape[0]
    ne = w1.shape[0]
    bm = MOE_BM
    any_spec = pl.BlockSpec(memory_space=pl.ANY)
    grid_spec = pltpu.PrefetchScalarGridSpec(
        num_scalar_prefetch=4,
        grid=(nblk,),
        in_specs=[any_spec,
                  any_spec,
                  pl.BlockSpec((1, 1, 2 * D_FF), lambda i, be, nu, cr, wm: (be[i], 0, 0)),
                  any_spec,
                  pl.BlockSpec((1, 1, d), lambda i, be, nu, cr, wm: (be[i], 0, 0))],
        out_specs=pl.BlockSpec((bm, d), lambda i, be, nu, cr, wm: (i, 0)),
        scratch_shapes=[pltpu.VMEM((2, bm, d), F32),
                        pltpu.VMEM((2, d, 2 * D_FF), w1.dtype), pltpu.VMEM((2, D_FF, d), w2.dtype),
                        pltpu.VMEM((d, 2 * D_FF), BF16), pltpu.VMEM((D_FF, d), BF16),
                        pltpu.SemaphoreType.DMA((2,)), pltpu.SemaphoreType.DMA((2, 2))],
    )
    return pl.pallas_call(
        _moe_kernel,
        grid_spec=grid_spec,
        out_shape=jax.ShapeDtypeStruct((nblk * bm, d), F32),
        compiler_params=_cparams(("arbitrary",)),
        name="moe_experts",
    )(blk_exp, n_used, chunk_rows, w_meta, xt, w1, b1.reshape(ne, 1, -1), w2, b2.reshape(ne, 1, -1))


def _combine_kernel(rows_ref, ys_hbm, h1_ref, route_ref, p_ref, wple_ref, wpg_ref, g3_ref, gf_ref, y_ref,
                    ybuf, sems, *, tile0):
    i = pl.program_id(0)
    n = pl.num_programs(0)
    tm = h1_ref.shape[0]
    rows = ybuf.shape[1]
    cpt = rows // MOE_CHUNK

    def fetch(step, slot):
        _chunk_gather_start(ys_hbm, rows_ref, (tile0 + step) * cpt, ybuf.at[slot], sems.at[slot],
                            both_queues=True)

    @pl.when(i == 0)
    def _():
        fetch(0, 0)

    slot = i % 2
    fetch(jnp.minimum(i + 1, n - 1), 1 - slot)
    _chunk_gather_wait(ys_hbm, ybuf.at[slot], sems.at[slot])
    route = route_ref[...]
    row_id = lax.broadcasted_iota(jnp.int32, (tm, rows), 1)
    weight = jnp.zeros((tm, rows), F32)
    for k in range(TOP_K):
        p_k = route[:, ROUTE_POS + k:ROUTE_POS + k + 1].astype(jnp.int32)
        weight = jnp.where(row_id == p_k, route[:, ROUTE_GATE + k:ROUTE_GATE + k + 1], weight)
    h2 = h1_ref[...] + jnp.dot(weight.astype(BF16), ybuf[slot].astype(BF16), preferred_element_type=F32)
    ple = jnp.dot(p_ref[...].astype(BF16), wple_ref[...], preferred_element_type=F32)
    gate = _sigmoid(jnp.dot(_rms(h2, g3_ref[...]).astype(BF16), wpg_ref[...], preferred_element_type=F32))
    h3 = h2 + ple * gate
    y_ref[...] = _rms(h3, gf_ref[...])

    @pl.when(i == n - 1)
    def _():
        _chunk_gather_wait(ys_hbm, ybuf.at[1 - slot], sems.at[1 - slot])


def moe_combine(ys, back_rows, h1, route, p, w_ple, w_pg, norm3, norm_f, tile0, n_tiles):
    d = h1.shape[1]
    tm = MOE_TM

    def full(arr):
        return pl.BlockSpec(arr.shape, lambda i, br: (0,) * arr.ndim)

    weights = [w_ple, w_pg, norm3.reshape(1, d), norm_f.reshape(1, d)]
    grid_spec = pltpu.PrefetchScalarGridSpec(
        num_scalar_prefetch=1,
        grid=(n_tiles,),
        in_specs=[pl.BlockSpec(memory_space=pl.ANY),
                  pl.BlockSpec((tm, d), lambda i, br: (tile0 + i, 0)),
                  pl.BlockSpec((tm, LANES), lambda i, br: (tile0 + i, 0)),
                  pl.BlockSpec((tm, PLE_DIM), lambda i, br: (i, 0))] + [full(wt) for wt in weights],
        out_specs=pl.BlockSpec((tm, d), lambda i, br: (i, 0)),
        scratch_shapes=[pltpu.VMEM((2, MOE_TILE_ROWS, d), F32), pltpu.SemaphoreType.DMA((2,))],
    )
    return pl.pallas_call(
        functools.partial(_combine_kernel, tile0=tile0),
        grid_spec=grid_spec,
        out_shape=jax.ShapeDtypeStruct((n_tiles * tm, d), F32),
        compiler_params=_cparams(("arbitrary",)),
        name="moe_combine",
    )(back_rows, ys, h1, route, p, *weights)


def _gate_layouts(zg, batch, seq, chunk):
    nc = seq // chunk
    g = zg[:, :2 * M_HEADS].reshape(batch, nc, chunk, 2, M_HEADS)
    rows = jnp.transpose(g, (3, 0, 1, 4, 2))
    return (rows[0], rows[1]), g[:, :, :, 1, :]


def kernel(x_prompt, x_sample, state_C, state_n, state_m, cache_k0, cache_v0, cache_k1, cache_v1, cache_k2, cache_v2, p_prompt, p_sample, norm1, w_in, b_igate, b_fgate, m_norm, w_pa, w_pb, w_o, norm2, w_router, b_router, w1, b1, w2, b2, norm3, w_ple, w_ple_gate, norm_f):
    bp, seq, d = x_prompt.shape
    bs, t_dec, _ = x_sample.shape
    n_p, n_s = bp * seq, bs * t_dec
    n_all = n_p + n_s
    x_p, x_s = x_prompt.reshape(n_p, d), x_sample.reshape(n_s, d)

    w = w_in[0]
    c_gate = 4 * M_WIDTH
    c_att = c_gate + 2 * M_HEADS
    c_mg = c_att + 3 * A_WIDTH
    w_m = w[:, :c_gate].astype(BF16)
    w_gate = jnp.pad(w[:, c_gate:c_att], ((0, 0), (0, LANES - 2 * M_HEADS))).astype(BF16)
    w_att = w[:, c_att:c_mg].astype(BF16)
    w_mg = w[:, c_mg:].astype(BF16)
    rope = rope_tables(np.concatenate([np.arange(seq), np.tile(np.arange(t_dec) + PAST_LEN, bs)]))
    tiles_p, tiles_seq = n_p // PROJ_TM, seq // PROJ_TM

    def rope_block(i):
        return jnp.where(i < tiles_p, i % tiles_seq, tiles_seq + i - tiles_p)

    zm, zgate = norm_proj(x_p, x_s, norm1[0], w_m, BF16, 4 * M_WIDTH, w_extra=w_gate)
    zatt = norm_proj(x_p, x_s, norm1[0], w_att, F32, 3 * A_WIDTH, rope=rope, rope_heads=2 * A_WIDTH // LANES,
                     rope_block=rope_block)
    zmg = norm_proj(x_p, x_s, norm1[0], w_mg, BF16, 2048)

    kv_p = kv_out(zatt, bp, seq)

    z_new = zatt[n_p:]
    caches = (cache_k0, cache_v0, cache_k1, cache_v1, cache_k2, cache_v2)
    news = []
    for g in range(len(A_GROUPS)):
        for part in range(2):
            col = (1 + part) * A_WIDTH + g * A_GROUP_WIDTH
            news.append(z_new[:, col:col + A_GROUP_WIDTH].reshape(bs, t_dec, A_HEADS_PER_GROUP, A_HEAD_DIM))
    bias = jnp.stack([b_igate[0], b_fgate[0]])
    chunk = 128
    grow_p, gcol_p = _gate_layouts(zgate[:n_p], bp, seq, chunk)
    t_pad = 8
    zm_s = jnp.pad(zm[n_p:].reshape(bs, t_dec, -1), ((0, 0), (0, t_pad - t_dec), (0, 0))).reshape(bs * t_pad, -1)
    zg_s = jnp.pad(zgate[n_p:].reshape(bs, t_dec, -1), ((0, 0), (0, t_pad - t_dec), (0, 0))).reshape(bs * t_pad, -1)
    grow_s, gcol_s = _gate_layouts(zg_s, bs, t_pad, t_pad)
    (hb_s, *kv_s), (ha_p, c_p, nn_p, m_p), (ha_s, c_s, nn_s, m_s), (hb_p,) = fused_call([
        window_part(z_new.reshape(bs, t_dec, 3 * A_WIDTH // LANES, LANES), caches, news),
        mlstm_part(zm, grow_p, gcol_p, bias, m_norm[0], bp, seq, chunk, chunk, 0),
        mlstm_part(zm_s, grow_s, gcol_s, bias, m_norm[0], bs, t_pad, t_pad, t_dec, 0,
                   state=(state_C[0], state_n[0], state_m[0])),
        attn_prompt_part(zatt, bp, seq),
    ], "mixers")
    nn_p, nn_s = nn_p.reshape(bp, M_HEADS, M_HEAD_DIM), nn_s.reshape(bs, M_HEADS, M_HEAD_DIM)
    m_p, m_s = m_p.reshape(bp, M_HEADS), m_s.reshape(bs, M_HEADS)
    ha_s = ha_s.reshape(bs, t_pad, M_WIDTH)[:, :t_dec].reshape(n_s, M_WIDTH)
    hb_s = hb_s.reshape(n_s, A_GROUP_WIDTH)

    w_r = jnp.pad(w_router[0], ((0, 0), (0, LANES - N_EXPERTS)))
    w_r_hi = w_r.astype(BF16)
    w_r_lo = (w_r - w_r_hi.astype(F32)).astype(BF16)
    b_r = jnp.pad(b_router[0], (0, LANES - N_EXPERTS)).reshape(1, LANES)
    h1, xt, route, cnt = merge_route((x_p, x_s), (ha_p, ha_s), (hb_p, hb_s), zmg, w_pa[0].astype(BF16),
                                     w_pb[0].astype(BF16), w_o[0].astype(BF16), norm2[0], w_r_hi, w_r_lo, b_r)
    nt = n_all // MOE_TM
    n_blocks = nt * MOE_TILE_ROWS // MOE_BM + N_EXPERTS + 1
    blk_exp, n_used, src_rows, w_meta, back_rows = moe_plan(cnt, n_blocks)
    ys = moe_experts(xt, blk_exp, n_used, src_rows, w_meta, w1[0], b1[0], w2[0], b2[0])
    tail_w = (w_ple[0].astype(BF16), w_ple_gate[0].astype(BF16), norm3[0], norm_f)
    nt_p = n_p // MOE_TM
    y_p = moe_combine(ys, back_rows, h1, route, p_prompt[0].reshape(n_p, PLE_DIM), *tail_w, 0, nt_p)
    y_s = moe_combine(ys, back_rows, h1, route, p_sample[0].reshape(n_s, PLE_DIM), *tail_w, nt_p, nt - nt_p)

    return (y_p.reshape(bp, seq, d), y_s.reshape(bs, t_dec, d),
            c_p[None], nn_p[None], m_p[None], *kv_p,
            c_s[None], nn_s[None], m_s[None], *kv_s)
```

```python
import functools

import jax
import jax.numpy as jnp
import numpy as np
from jax import lax
from jax.experimental import pallas as pl
from jax.experimental.pallas import tpu as pltpu

F32 = jnp.float32
BF16 = jnp.bfloat16

D_MODEL = 1024
PAST_LEN = 8192
M_HEADS = 4
M_HEAD_DIM = 256
M_WIDTH = M_HEADS * M_HEAD_DIM
A_GROUPS = ((128, 1), (512, 4), (2048, 16))
A_HEADS_PER_GROUP = 4
A_HEAD_DIM = 128
A_GROUP_WIDTH = A_HEADS_PER_GROUP * A_HEAD_DIM
A_WIDTH = len(A_GROUPS) * A_GROUP_WIDTH
ROPE_THETA = 500000.0
ROPE_DIM = A_HEAD_DIM // 4
N_EXPERTS = 32
TOP_K = 4
D_FF = D_MODEL
SWIGLU_ALPHA = 1.702
SWIGLU_LIMIT = 7.0
PLE_DIM = 256
EPS = 1e-6

LANES = 128
VMEM_LIMIT = 56 * 1024 * 1024
FUSED_VMEM_LIMIT = 60 * 1024 * 1024
NEG_INF = float("-inf")
PROJ_TM = 512
ROPE_SUB_COLS = 512


def _cparams(sem, vmem_limit=VMEM_LIMIT):
    return pltpu.CompilerParams(dimension_semantics=sem, vmem_limit_bytes=vmem_limit)


def _rms(x, gain):
    return x * lax.rsqrt(jnp.mean(x * x, axis=-1, keepdims=True) + EPS) * gain


def _sigmoid(x):
    return 1.0 / (1.0 + jnp.exp(-x))


def _log_sigmoid(x):
    return jnp.minimum(x, 0.0) - jnp.log(1.0 + jnp.exp(-jnp.abs(x)))


def _two_source_specs(block, tiles_a, ix=lambda fn: fn):
    return [pl.BlockSpec(block, ix(lambda i, *_: (jnp.minimum(i, tiles_a - 1), 0))),
            pl.BlockSpec(block, ix(lambda i, *_: (jnp.maximum(i - tiles_a, 0), 0)))]


def _norm_proj_kernel(xa_ref, xb_ref, g_ref, w_ref, *rest, rope_heads, has_extra, row_axis, tiles_a, kv_tiles):
    rest = list(rest)
    if rope_heads:
        cos_ref, sinm_ref, sinp_ref = rest[:3]
        rest = rest[3:]
    kv_refs = ()
    if has_extra:
        we_ref, o_ref, oe_ref = rest
    elif kv_tiles:
        o_ref, *kv_refs = rest
    else:
        o_ref, = rest
    x = jnp.where(pl.program_id(row_axis) < tiles_a, xa_ref[...], xb_ref[...])
    xn = _rms(x, g_ref[...]).astype(BF16)
    if has_extra:
        oe_ref[...] = jnp.dot(xn, we_ref[...], preferred_element_type=F32)
    if not rope_heads:
        o_ref[...] = jnp.dot(xn, w_ref[...], preferred_element_type=F32).astype(o_ref.dtype)
        return
    cosf, sinm, sinp = cos_ref[...], sinm_ref[...], sinp_ref[...]
    tm = o_ref.shape[0]
    nh = A_WIDTH // LANES
    last_of_seq = (pl.program_id(row_axis) % kv_tiles == kv_tiles - 1) if kv_tiles else None
    for c0 in range(0, o_ref.shape[1], ROPE_SUB_COLS):
        z = jnp.dot(xn, w_ref[:, c0:c0 + ROPE_SUB_COLS], preferred_element_type=F32)
        for h in range(ROPE_SUB_COLS // LANES):
            head = c0 // LANES + h
            zh = z[:, h * LANES:(h + 1) * LANES]
            if head < rope_heads:
                zh = (zh * cosf + pltpu.roll(zh, LANES - ROPE_DIM // 2, 1) * sinm
                      + pltpu.roll(zh, ROPE_DIM // 2, 1) * sinp)
            o_ref[:, c0 + h * LANES:c0 + (h + 1) * LANES] = zh.astype(o_ref.dtype)
            if kv_tiles and head >= nh:
                part, g, hh = head // nh, (head % nh) // A_HEADS_PER_GROUP, head % A_HEADS_PER_GROUP
                kv_ref = kv_refs[2 * g + part - 1]
                rows = kv_ref.shape[2]
                if A_GROUPS[g][0] >= kv_tiles * tm:
                    kv_ref[0, 0, :, hh, :] = zh
                else:
                    kv_ref[0, 0, :, hh, :] = jnp.where(last_of_seq, zh[tm - rows:], 0.0)


def norm_proj(xa, xb, gain, w, out_dtype, tn, rope=None, rope_heads=0, rope_block=None, w_extra=None,
              kv_seq=None):
    d = xa.shape[1]
    tm = PROJ_TM
    tiles_a = xa.shape[0] // tm
    n = xa.shape[0] + (0 if xb is None else xb.shape[0])
    xb = xa if xb is None else xb
    ncol = w.shape[1]
    assert not rope_heads or (tn == ncol and ncol % ROPE_SUB_COLS == 0)
    assert kv_seq is None or (rope_heads and w_extra is None and n == kv_seq[0] * kv_seq[1])
    rows_outer = w_extra is not None
    grid = (n // tm, ncol // tn) if rows_outer else (ncol // tn, n // tm)

    def ix(fn):
        return (lambda i, j: fn(i, j)) if rows_outer else (lambda j, i: fn(i, j))

    in_specs = _two_source_specs((tm, d), tiles_a, ix) + [
        pl.BlockSpec((1, d), ix(lambda i, j: (0, 0))),
        pl.BlockSpec((d, tn), ix(lambda i, j: (0, j)))]
    args = [xa, xb, gain.reshape(1, d), w]
    if rope_heads:
        in_specs += [pl.BlockSpec((tm, LANES), ix(lambda i, j: (rope_block(i), 0)))] * 3
        args += list(rope)
    out_specs = pl.BlockSpec((tm, tn), ix(lambda i, j: (i, j)))
    out_shape = jax.ShapeDtypeStruct((n, ncol), out_dtype)
    if w_extra is not None:
        in_specs.append(pl.BlockSpec((d, LANES), ix(lambda i, j: (0, 0))))
        args.append(w_extra)
        out_specs = [out_specs, pl.BlockSpec((tm, LANES), ix(lambda i, j: (i, 0)))]
        out_shape = [out_shape, jax.ShapeDtypeStruct((n, LANES), F32)]
    kv_tiles = 0
    if kv_seq is not None:
        batch, seq = kv_seq
        kv_tiles = seq // tm
        out_specs, out_shape = [out_specs], [out_shape]
        for win, _ in A_GROUPS:
            keep = min(win, seq)
            assert keep >= seq or keep <= tm
            blk = (1, 1, min(keep, tm), A_HEADS_PER_GROUP, A_HEAD_DIM)
            if keep >= seq:
                idx = ix(lambda i, j: (0, i // kv_tiles, i % kv_tiles, 0, 0))
            else:
                idx = ix(lambda i, j: (0, i // kv_tiles, 0, 0, 0))
            out_specs += [pl.BlockSpec(blk, idx)] * 2
            out_shape += [jax.ShapeDtypeStruct((1, batch, keep, A_HEADS_PER_GROUP, A_HEAD_DIM), F32)] * 2
    return pl.pallas_call(
        functools.partial(_norm_proj_kernel, rope_heads=rope_heads, has_extra=w_extra is not None,
                          row_axis=0 if rows_outer else 1, tiles_a=tiles_a, kv_tiles=kv_tiles),
        grid=grid,
        in_specs=in_specs,
        out_specs=out_specs,
        out_shape=out_shape,
        compiler_params=_cparams(("arbitrary", "arbitrary")),
        name="norm_proj",
    )(*args)


def rope_tables(pos):
    half = ROPE_DIM // 2
    inv = (1.0 / (np.float32(ROPE_THETA) ** (np.arange(0, ROPE_DIM, 2, dtype=np.float32) / ROPE_DIM))).astype(np.float32)
    ang = (pos.astype(np.float32)[:, None] * inv[None, :]).astype(np.float64)
    cos, sin = np.cos(ang), np.sin(ang)
    n = pos.shape[0]
    ones = np.ones((n, LANES - ROPE_DIM))
    zeros = np.zeros((n, LANES - ROPE_DIM))
    zh = np.zeros((n, half))
    cosf = np.concatenate([cos, cos, ones], axis=1).astype(np.float32)
    sinm = np.concatenate([-sin, zh, zeros], axis=1).astype(np.float32)
    sinp = np.concatenate([zh, sin, zeros], axis=1).astype(np.float32)
    return jnp.asarray(cosf), jnp.asarray(sinm), jnp.asarray(sinp)


def _mlstm_kernel(bias_ref, q_ref, k_ref, v_ref, o_ref, gi_ref, gfr_ref, gfc_ref, mn_ref, *rest,
                  chunk, n_chunks, valid_len, has_state):
    if has_state:
        c0_ref, n0_ref, m0_ref, h_ref, c_out, n_out, m_out, c_s, n_s, m_s = rest
    else:
        h_ref, c_out, n_out, m_out, c_s, n_s, m_s = rest
    c = pl.program_id(0) % n_chunks

    @pl.when(c == 0)
    def _():
        if has_state:
            c_s[...] = c0_ref[0]
            n_s[...] = n0_ref[0]
            m_s[...] = m0_ref[0]
        else:
            c_s[...] = jnp.zeros_like(c_s)
            n_s[...] = jnp.zeros_like(n_s)
            m_s[...] = jnp.zeros_like(m_s)

    last = c == n_chunks - 1
    for hd in range(M_HEADS):
        cols = slice(hd * M_HEAD_DIM, (hd + 1) * M_HEAD_DIM)
        _mlstm_head(hd, cols, last, bias_ref, q_ref, k_ref, v_ref, o_ref, gi_ref, gfr_ref, gfc_ref, mn_ref,
                    h_ref, c_out, n_out, m_out, c_s, n_s, m_s, chunk, valid_len)


def _mlstm_head(hd, cols, last, bias_ref, q_ref, k_ref, v_ref, o_ref, gi_ref, gfr_ref, gfc_ref, mn_ref,
                h_ref, c_out, n_out, m_out, c_s, n_s, m_s, chunk, valid_len):
    L = chunk
    q = q_ref[:, cols]
    k = k_ref[:, cols] * (M_HEAD_DIM ** -0.5)
    v = v_ref[:, cols]
    b_i = bias_ref[0, hd]
    b_f = bias_ref[1, hd]
    i_row = gi_ref[0, 0, hd:hd + 1, :] + b_i
    lf_row = _log_sigmoid(gfr_ref[0, 0, hd:hd + 1, :] + b_f)
    lf_col = _log_sigmoid(gfc_ref[0, 0, :, hd:hd + 1] + b_f)
    row_id = lax.broadcasted_iota(jnp.int32, (L, L), 0)
    col_id = lax.broadcasted_iota(jnp.int32, (L, L), 1)
    if valid_len < L:
        lane = lax.broadcasted_iota(jnp.int32, (1, L), 1)
        sub = lax.broadcasted_iota(jnp.int32, (L, 1), 0)
        i_row = jnp.where(lane < valid_len, i_row, NEG_INF)
        lf_row = jnp.where(lane < valid_len, lf_row, 0.0)
        lf_col = jnp.where(sub < valid_len, lf_col, 0.0)
    causal = col_id <= row_id
    b_col = jnp.sum(jnp.where(causal, lf_row, 0.0), axis=1, keepdims=True)
    b_row = jnp.sum(jnp.where(row_id <= col_id, lf_col, 0.0), axis=0, keepdims=True)
    m_prev = m_s[hd]
    dmat = jnp.where(causal, b_col - b_row + i_row, NEG_INF)
    inter = b_col + m_prev
    mj = jnp.maximum(inter, jnp.max(dmat, axis=1, keepdims=True))
    s = lax.dot_general(q, k, (((1,), (1,)), ((), ())), preferred_element_type=F32)
    sc = s * jnp.exp(dmat - mj)
    a_int = jnp.exp(inter - mj)
    c_prev = c_s[hd]
    n_prev = n_s[hd]
    qc = lax.dot_general(q, c_prev.astype(BF16), (((1,), (1,)), ((), ())), preferred_element_type=F32)
    num = jnp.dot(sc.astype(BF16), v, preferred_element_type=F32) + a_int * qc
    qn = jnp.sum(q.astype(F32) * n_prev, axis=1, keepdims=True)
    den = jnp.sum(sc, axis=1, keepdims=True) + a_int * qn
    h = num / jnp.maximum(jnp.abs(den), jnp.exp(-mj))
    h = h * _sigmoid(o_ref[:, cols].astype(F32))
    h = h * lax.rsqrt(jnp.mean(h * h, axis=-1, keepdims=True) + EPS) * mn_ref[:, cols]
    h_ref[:, cols] = h.astype(h_ref.dtype)

    bl = jnp.sum(lf_row, axis=1, keepdims=True)
    g_row = bl - b_row + i_row
    m_new = jnp.maximum(bl + m_prev, jnp.max(g_row, axis=1, keepdims=True))
    ws_row = jnp.exp(g_row - m_new)
    a_c = jnp.exp(bl + m_prev - m_new)
    vt = (v.astype(F32).T * ws_row).astype(BF16)
    c_new = a_c * c_prev + jnp.dot(vt, k, preferred_element_type=F32)
    ws8 = jnp.broadcast_to(ws_row, (8, L)).astype(BF16)
    n_new = a_c * n_prev + jnp.dot(ws8, k, preferred_element_type=F32)[0:1]
    c_s[hd] = c_new
    n_s[hd] = n_new
    m_s[hd] = m_new

    @pl.when(last)
    def _():
        c_out[0, hd] = c_new
        n_out[0, hd] = n_new
        m_out[0, hd] = m_new


def mlstm_part(zm, gates_row, gates_col, bias, m_norm, batch, seq, chunk, valid_len, row0, state=None):
    E = M_HEAD_DIM
    nc = seq // chunk
    blk0 = row0 // chunk

    def zspec(col):
        return pl.BlockSpec((chunk, M_WIDTH), lambda i: (blk0 + i, col))

    def per_chunk(shape):
        return pl.BlockSpec(shape, lambda i: (i // nc, i % nc, 0, 0))

    def per_row(shape):
        return pl.BlockSpec(shape, lambda i: (i // nc, 0, 0, 0))

    in_specs = [pl.BlockSpec(memory_space=pltpu.SMEM),
                zspec(0), zspec(1), zspec(2), zspec(3),
                per_chunk((1, 1, M_HEADS, chunk)), per_chunk((1, 1, M_HEADS, chunk)),
                per_chunk((1, 1, chunk, M_HEADS)),
                pl.BlockSpec((1, M_WIDTH), lambda i: (0, 0))]
    gi_row, gf_row = gates_row
    args = [bias, zm, zm, zm, zm, gi_row, gf_row, gates_col, m_norm.reshape(1, M_WIDTH)]
    state_shapes = [(1, M_HEADS, E, E), (1, M_HEADS, 1, E), (1, M_HEADS, 1, 1)]
    if state is not None:
        c0, n0, m0 = state
        in_specs += [per_row(s) for s in state_shapes]
        args += [c0, n0.reshape(batch, M_HEADS, 1, E), m0.reshape(batch, M_HEADS, 1, 1)]
    return dict(
        body=functools.partial(_mlstm_kernel, chunk=chunk, n_chunks=nc, valid_len=valid_len,
                               has_state=state is not None),
        steps=batch * nc,
        in_specs=in_specs,
        args=args,
        out_specs=[pl.BlockSpec((chunk, M_WIDTH), lambda i: (i, 0))] + [per_row(s) for s in state_shapes],
        out_shape=[jax.ShapeDtypeStruct((batch * seq, M_WIDTH), BF16)]
        + [jax.ShapeDtypeStruct((batch,) + s[1:], F32) for s in state_shapes],
        scratch=[pltpu.VMEM(s[1:], F32) for s in state_shapes],
    )


def fused_call(parts, name):
    steps = parts[0]["steps"]
    assert all(p["steps"] == steps for p in parts)
    n_in = [len(p["in_specs"]) for p in parts]
    n_out = [len(p["out_specs"]) for p in parts]
    n_scr = [len(p["scratch"]) for p in parts]

    def body(*refs):
        ins, outs, scr = refs[:sum(n_in)], refs[sum(n_in):sum(n_in) + sum(n_out)], refs[sum(n_in) + sum(n_out):]
        a = b = c = 0
        for p, na, nb, nc in zip(parts, n_in, n_out, n_scr):
            p["body"](*ins[a:a + na], *outs[b:b + nb], *scr[c:c + nc])
            a, b, c = a + na, b + nb, c + nc

    res = pl.pallas_call(
        body,
        grid=(steps,),
        in_specs=[s for p in parts for s in p["in_specs"]],
        out_specs=[s for p in parts for s in p["out_specs"]],
        out_shape=[s for p in parts for s in p["out_shape"]],
        scratch_shapes=[s for p in parts for s in p["scratch"]],
        compiler_params=_cparams(("arbitrary",), FUSED_VMEM_LIMIT),
        name=name,
    )(*[a for p in parts for a in p["args"]])
    grouped, b = [], 0
    for nb in n_out:
        grouped.append(list(res[b:b + nb]))
        b += nb
    return grouped


ATT_BLK = 128
ATT_SUBSTEPS = 4


def _band_block(qb, kcat, vcat, mask):
    s = lax.dot_general(qb, kcat, (((1,), (1,)), ((), ())), preferred_element_type=F32)
    s = jnp.where(mask, s, NEG_INF)
    mx = jnp.max(s, axis=1, keepdims=True)
    p = jnp.exp(s - mx)
    l = jnp.sum(p, axis=1, keepdims=True)
    o = jnp.dot(p.astype(BF16), vcat, preferred_element_type=F32) / l
    return o, mx + jnp.log(l)


def _att_substep_group(sub):
    last = len(A_GROUPS) - 1
    return jnp.minimum(sub, last) if not isinstance(sub, int) else min(sub, last)


def _attn_prompt_kernel(q_ref, k_ref, v_ref, y_ref, *scr, seq):
    o_scr, lse_scr = scr[:3], scr[3:6]
    scale = A_HEAD_DIM ** -0.5
    qi = lax.broadcasted_iota(jnp.int32, (ATT_BLK, ATT_BLK), 0)
    ki = lax.broadcasted_iota(jnp.int32, (ATT_BLK, ATT_BLK), 1)
    cur_mask = ki <= qi
    prev_mask = ki >= qi
    band_mask = jnp.concatenate([prev_mask, cur_mask], axis=1)

    def band_unit(g, r, n):
        dil = A_GROUPS[g][1]

        def rows(first_blk, n_blk):
            if dil == 1:
                return pl.ds(first_blk * ATT_BLK, n_blk * ATT_BLK)
            return pl.ds(r + first_blk * ATT_BLK * dil, n_blk * ATT_BLK, stride=dil)

        qb = (q_ref[rows(n, 1), :] * scale).astype(BF16)
        if n == 0:
            kk, vv, mask = k_ref[rows(0, 1), :], v_ref[rows(0, 1), :], cur_mask
        else:
            kk, vv, mask = k_ref[rows(n - 1, 2), :], v_ref[rows(n - 1, 2), :], band_mask
        o, lse = _band_block(qb, kk.astype(BF16), vv.astype(BF16), mask)
        o_scr[g][rows(n, 1), :] = o
        lse_scr[g][rows(n, 1), :] = lse

    sub = pl.program_id(0) % ATT_SUBSTEPS
    for s in range(ATT_SUBSTEPS):
        g = _att_substep_group(s)
        dil = A_GROUPS[g][1]
        units = [(g, r, n) for r in range(dil) for n in range(seq // dil // ATT_BLK)]
        shares = [t for t in range(ATT_SUBSTEPS) if _att_substep_group(t) == g]
        per_share = -(-len(units) // len(shares))
        k0 = shares.index(s) * per_share

        @pl.when(sub == s)
        def _(units=units[k0:k0 + per_share]):
            for unit in units:
                band_unit(*unit)

    @pl.when(sub == ATT_SUBSTEPS - 1)
    def _():
        l0, l1, l2 = lse_scr[0][...], lse_scr[1][...], lse_scr[2][...]
        mx = jnp.maximum(jnp.maximum(l0, l1), l2)
        w0, w1, w2 = jnp.exp(l0 - mx), jnp.exp(l1 - mx), jnp.exp(l2 - mx)
        y = (w0 * o_scr[0][...] + w1 * o_scr[1][...] + w2 * o_scr[2][...]) / (w0 + w1 + w2)
        y_ref[...] = y.astype(y_ref.dtype)


def attn_prompt_part(zatt, batch, seq):
    nh = A_WIDTH // LANES
    per_b = A_HEADS_PER_GROUP * ATT_SUBSTEPS

    assert len(A_GROUPS) == ATT_SUBSTEPS - 1

    def slot(i):
        return (i // ATT_SUBSTEPS) % A_HEADS_PER_GROUP

    def spec(part):
        return pl.BlockSpec((seq, LANES), lambda i: (
            i // per_b, part * nh + _att_substep_group(i % ATT_SUBSTEPS) * A_HEADS_PER_GROUP + slot(i)))

    return dict(
        body=functools.partial(_attn_prompt_kernel, seq=seq),
        steps=batch * per_b,
        in_specs=[spec(0), spec(1), spec(2)],
        args=[zatt, zatt, zatt],
        out_specs=[pl.BlockSpec((seq, LANES), lambda i: (i // per_b, slot(i)))],
        out_shape=[jax.ShapeDtypeStruct((batch * seq, A_GROUP_WIDTH), BF16)],
        scratch=[pltpu.VMEM((seq, LANES), F32)] * 3 + [pltpu.VMEM((seq, 1), F32)] * 3,
    )


def _window_kernel(new_ref, *refs, t_dec):
    n_buf = 2 * len(A_GROUPS)
    caches, news = refs[:n_buf], refs[n_buf:2 * n_buf]
    y_ref = refs[2 * n_buf]
    outs = refs[2 * n_buf + 1:3 * n_buf + 1]
    bufs = refs[3 * n_buf + 1:4 * n_buf + 1]
    sem_in, sem_out, sem_new = refs[4 * n_buf + 1:]
    b = pl.program_id(0)
    nb = pl.num_programs(0)
    slot = b % 2

    def copy_in(ci, row, sl):
        return pltpu.make_async_copy(caches[ci].at[0, row], bufs[ci].at[sl], sem_in.at[sl, ci])

    def copy_out(ci, row, sl):
        wb = bufs[ci].shape[1]
        return pltpu.make_async_copy(bufs[ci].at[sl, pl.ds(t_dec, wb - t_dec)],
                                     outs[ci].at[0, row, pl.ds(0, wb - t_dec)], sem_out.at[sl, ci])

    def copy_new(ci):
        wb = bufs[ci].shape[1]
        return pltpu.make_async_copy(news[ci], outs[ci].at[0, :, pl.ds(wb - t_dec, t_dec)], sem_new.at[ci])

    @pl.when(b == 0)
    def _():
        for ci in range(n_buf):
            copy_new(ci).start()
            copy_in(ci, 0, 0).start()

    for ci in range(n_buf):
        copy_in(ci, b, slot).wait()
        copy_out(ci, b, slot).start(priority=1)

    @pl.when(b >= 1)
    def _():
        for ci in range(n_buf):
            copy_out(ci, b - 1, 1 - slot).wait()

    @pl.when(b + 1 < nb)
    def _():
        for ci in range(n_buf):
            copy_in(ci, b + 1, 1 - slot).start()

    _attn_sample(new_ref, bufs, slot, y_ref, t_dec)

    @pl.when(b == nb - 1)
    def _():
        for ci in range(n_buf):
            copy_out(ci, b, slot).wait()
            copy_new(ci).wait()


def _attn_sample(new_ref, bufs, slot, y_ref, t_dec):
    nh = A_WIDTH // LANES
    scale = A_HEAD_DIM ** -0.5
    jj = lax.broadcasted_iota(jnp.int32, (ATT_BLK, 1, 1), 0)
    for t in range(t_dec):
        outs, lses = [], []
        for g, (_, dil) in enumerate(A_GROUPS):
            h0 = g * A_HEADS_PER_GROUP
            q = new_ref[0, t, h0:h0 + A_HEADS_PER_GROUP, :] * scale
            rows = pl.ds(t % dil, ATT_BLK, stride=dil) if dil > 1 else pl.ds(0, ATT_BLK)
            kc = bufs[2 * g][slot, rows]
            vc = bufs[2 * g + 1][slot, rows]
            s_c = jnp.sum(kc * q[None], axis=-1, keepdims=True)
            if t // dil > 0:
                s_c = jnp.where(jj >= t // dil, s_c, NEG_INF)
            mx = jnp.max(s_c, axis=0)
            new_u = [u for u in range(t + 1) if (t - u) % dil == 0]
            s_new = []
            for u in new_u:
                k_u = new_ref[0, u, nh + h0:nh + h0 + A_HEADS_PER_GROUP, :]
                s_u = jnp.sum(k_u * q, axis=-1, keepdims=True)
                s_new.append(s_u)
                mx = jnp.maximum(mx, s_u)
            p_c = jnp.exp(s_c - mx[None])
            l = jnp.sum(p_c, axis=0)
            acc = jnp.sum(p_c * vc, axis=0)
            for u, s_u in zip(new_u, s_new):
                p_u = jnp.exp(s_u - mx)
                v_u = new_ref[0, u, 2 * nh + h0:2 * nh + h0 + A_HEADS_PER_GROUP, :]
                l = l + p_u
                acc = acc + p_u * v_u
            outs.append(acc / l)
            lses.append(mx + jnp.log(l))
        mxg = jnp.maximum(jnp.maximum(lses[0], lses[1]), lses[2])
        ws = [jnp.exp(l_g - mxg) for l_g in lses]
        y = (ws[0] * outs[0] + ws[1] * outs[1] + ws[2] * outs[2]) / (ws[0] + ws[1] + ws[2])
        y_ref[0, t] = y.astype(y_ref.dtype)


def window_part(new_qkv, caches, news):
    batch, t_dec = new_qkv.shape[:2]
    for (win, dil), k_buf in zip(A_GROUPS, caches[::2]):
        assert win == ATT_BLK * dil and k_buf.shape[2] == win
    any_spec = pl.BlockSpec(memory_space=pl.ANY)
    n_buf = len(caches)
    return dict(
        body=functools.partial(_window_kernel, t_dec=t_dec),
        steps=batch,
        in_specs=[pl.BlockSpec((1, t_dec, new_qkv.shape[2], LANES), lambda b: (b, 0, 0, 0))]
        + [any_spec] * (2 * n_buf),
        args=[new_qkv, *caches, *news],
        out_specs=[pl.BlockSpec((1, t_dec, A_HEADS_PER_GROUP, LANES), lambda b: (b, 0, 0, 0))] + [any_spec] * n_buf,
        out_shape=[jax.ShapeDtypeStruct((batch, t_dec, A_HEADS_PER_GROUP, LANES), BF16)]
        + [jax.ShapeDtypeStruct(c.shape, c.dtype) for c in caches],
        scratch=[pltpu.VMEM((2,) + c.shape[2:], c.dtype) for c in caches]
        + [pltpu.SemaphoreType.DMA((2, n_buf)), pltpu.SemaphoreType.DMA((2, n_buf)),
           pltpu.SemaphoreType.DMA((n_buf,))],
    )


MOE_TM = 256
MOE_CHUNK = 8
MOE_BM = 512
MOE_TILE_ROWS = TOP_K * MOE_TM + N_EXPERTS * MOE_CHUNK
ROUTE_EXPERT, ROUTE_POS, ROUTE_GATE = 0, TOP_K, 2 * TOP_K


def _merge_route_kernel(h_a, h_b, ha_a, ha_b, hb_a, hb_b, ga_ref, gb_ref, wpa_ref, wpb_ref, wo_ref, g2_ref,
                        wrh_ref, wrl_ref, br_ref, h1_ref, xt_ref, route_ref, cnt_ref, *, tiles_a):
    tm = h_a.shape[0]
    rows = xt_ref.shape[0]
    first = pl.program_id(0) < tiles_a
    h = jnp.where(first, h_a[...], h_b[...])
    a = jnp.dot(jnp.where(first, ha_a[...], ha_b[...]), wpa_ref[...], preferred_element_type=F32)
    b = jnp.dot(jnp.where(first, hb_a[...], hb_b[...]), wpb_ref[...], preferred_element_type=F32)
    u = _sigmoid(ga_ref[...].astype(F32)) * a + _sigmoid(gb_ref[...].astype(F32)) * b
    h1 = h + jnp.dot(u.astype(BF16), wo_ref[...], preferred_element_type=F32)
    h1_ref[...] = h1
    xn = _rms(h1, g2_ref[...])
    x_hi = xn.astype(BF16)
    x_lo = (xn - x_hi.astype(F32)).astype(BF16)
    logits = (jnp.dot(x_hi, wrh_ref[...], preferred_element_type=F32)
              + jnp.dot(x_lo, wrh_ref[...], preferred_element_type=F32)
              + jnp.dot(x_hi, wrl_ref[...], preferred_element_type=F32)) + br_ref[...]
    lane = lax.broadcasted_iota(jnp.int32, (tm, LANES), 1)
    logits = jnp.where(lane < N_EXPERTS, logits, NEG_INF)
    vals, hots = [], []
    work = logits
    for _ in range(TOP_K):
        mx = jnp.max(work, axis=1, keepdims=True)
        idx = jnp.min(jnp.where(work == mx, lane, LANES), axis=1, keepdims=True)
        hot = lane == idx
        work = jnp.where(hot, NEG_INF, work)
        vals.append(mx)
        hots.append(hot)
    exps = [jnp.exp(v - vals[0]) for v in vals]
    denom = exps[0] + exps[1] + exps[2] + exps[3]
    hot_f = jnp.zeros((tm, LANES), F32)
    for hot in hots:
        hot_f = hot_f + jnp.where(hot, 1.0, 0.0)
    r_id = lax.broadcasted_iota(jnp.int32, (tm, tm), 0)
    c_id = lax.broadcasted_iota(jnp.int32, (tm, tm), 1)
    earlier = jnp.where(c_id < r_id, 1.0, 0.0).astype(BF16)
    rank = jnp.dot(earlier, hot_f.astype(BF16), preferred_element_type=F32)
    cnt = jnp.sum(hot_f, axis=0, keepdims=True)
    padded = jnp.floor((cnt + (MOE_CHUNK - 1)) * (1.0 / MOE_CHUNK)) * MOE_CHUNK
    la = lax.broadcasted_iota(jnp.int32, (LANES, LANES), 0)
    lb = lax.broadcasted_iota(jnp.int32, (LANES, LANES), 1)
    before = jnp.where(la < lb, 1.0, 0.0).astype(BF16)
    run_start = jnp.dot(jnp.broadcast_to(padded, (8, LANES)).astype(BF16), before,
                        preferred_element_type=F32)[0:1]
    pos_all = run_start + rank
    lane_f = lane.astype(F32)
    row_id = lax.broadcasted_iota(jnp.int32, (tm, rows), 1)
    place = jnp.zeros((tm, rows), F32)
    route = jnp.zeros((tm, LANES), F32)
    for k in range(TOP_K):
        e_k = jnp.sum(jnp.where(hots[k], lane_f, 0.0), axis=1, keepdims=True)
        p_k = jnp.sum(jnp.where(hots[k], pos_all, 0.0), axis=1, keepdims=True)
        place = jnp.where(row_id == p_k.astype(jnp.int32), 1.0, place)
        route = jnp.where(lane == ROUTE_EXPERT + k, e_k, route)
        route = jnp.where(lane == ROUTE_POS + k, p_k, route)
        route = jnp.where(lane == ROUTE_GATE + k, exps[k] / denom, route)
    route_ref[...] = route
    cnt_ref[0] = cnt
    xt_ref[...] = lax.dot_general(place.astype(BF16), x_hi, (((0,), (0,)), ((), ())),
                                  preferred_element_type=F32)


def merge_route(h, ha, hb, zmg, w_pa, w_pb, w_o, norm2, w_r_hi, w_r_lo, b_r):
    d = h[0].shape[1]
    tm = MOE_TM
    tiles_a = h[0].shape[0] // tm
    n = h[0].shape[0] + h[1].shape[0]
    nt = n // tm

    def full(arr):
        return pl.BlockSpec(arr.shape, lambda i: (0,) * arr.ndim)

    weights = [w_pa, w_pb, w_o, norm2.reshape(1, d), w_r_hi, w_r_lo, b_r]
    return pl.pallas_call(
        functools.partial(_merge_route_kernel, tiles_a=tiles_a),
        grid=(nt,),
        in_specs=_two_source_specs((tm, d), tiles_a) + _two_source_specs((tm, M_WIDTH), tiles_a)
        + _two_source_specs((tm, A_GROUP_WIDTH), tiles_a)
        + [pl.BlockSpec((tm, d), lambda i: (i, 0)),
           pl.BlockSpec((tm, d), lambda i: (i, 1))] + [full(wt) for wt in weights],
        out_specs=[pl.BlockSpec((tm, d), lambda i: (i, 0)),
                   pl.BlockSpec((MOE_TILE_ROWS, d), lambda i: (i, 0)),
                   pl.BlockSpec((tm, LANES), lambda i: (i, 0)),
                   pl.BlockSpec((1, 1, LANES), lambda i: (i, 0, 0))],
        out_shape=[jax.ShapeDtypeStruct((n, d), F32),
                   jax.ShapeDtypeStruct((nt * MOE_TILE_ROWS, d), F32),
                   jax.ShapeDtypeStruct((n, LANES), F32),
                   jax.ShapeDtypeStruct((nt, 1, LANES), F32)],
        compiler_params=_cparams(("parallel",)),
        name="merge_route",
    )(*h, *ha, *hb, zmg, zmg, *weights)


def moe_plan(cnt, n_blocks):
    nt = cnt.shape[0]
    cpb = MOE_BM // MOE_CHUNK
    cpt = MOE_TILE_ROWS // MOE_CHUNK
    i32 = jnp.int32
    cnt = cnt[:, 0, :N_EXPERTS].astype(i32)
    nch = (cnt + MOE_CHUNK - 1) // MOE_CHUNK
    off_incl = jnp.cumsum(nch, axis=1)
    off_ch = off_incl - nch
    seq_incl = jnp.cumsum(nch, axis=0)
    base_ch = seq_incl - nch
    tot = seq_incl[-1]
    nb = (tot + cpb - 1) // cpb
    bend = jnp.cumsum(nb)
    bstart = bend - nb
    n_used = bend[-1:]
    blk = jnp.arange(n_blocks, dtype=i32)
    blk_exp = jnp.minimum(jnp.sum(blk[:, None] >= bend[None, :], axis=1), N_EXPERTS - 1).astype(i32)
    experts = jnp.arange(N_EXPERTS, dtype=i32)
    tiles = jnp.arange(nt, dtype=i32)

    def pick(hot, table):
        return jnp.sum(jnp.where(hot, table, 0), axis=-1).astype(i32)

    hot_e = blk_exp[:, None] == experts[None, :]
    seq_b, off_b, base_b = (pick(hot_e[:, None, :], t[None]) for t in (seq_incl, off_ch, base_ch))
    q = (blk - pick(hot_e, bstart[None]))[:, None] * cpb + jnp.arange(cpb, dtype=i32)[None, :]
    tile = jnp.minimum(jnp.sum(q[:, :, None] >= seq_b[:, None, :], axis=2), nt - 1).astype(i32)
    hot_t = tile[:, :, None] == tiles[None, None, :]
    src = (tile * cpt + pick(hot_t, off_b[:, None, :]) + q - pick(hot_t, base_b[:, None, :])) * MOE_CHUNK
    src = jnp.where(q < pick(hot_e, tot[None])[:, None], src, 0).astype(i32)
    slot = jnp.arange(cpt, dtype=i32)
    e_s = jnp.sum(slot[None, :, None] >= off_incl[:, None, :], axis=2).astype(i32)
    hot_s = e_s[:, :, None] == experts[None, None, :]
    back = (pick(hot_s, bstart[None, None, :]) * cpb + pick(hot_s, base_ch[:, None, :])
            + slot[None, :] - pick(hot_s, off_ch[:, None, :])) * MOE_CHUNK
    back = jnp.where(e_s < N_EXPERTS, back, 0).astype(i32)
    used = nb > 0
    parity = (jnp.cumsum(used.astype(i32)) - 1) % 2
    later = used[None, :] & (experts[None, :] > experts[:, None])
    succ = jnp.min(jnp.where(later, experts[None, :], N_EXPERTS), axis=1)
    w_meta = jnp.stack([pick(hot_e, parity[None]), pick(hot_e, succ[None])]).astype(i32).reshape(-1)
    return blk_exp, n_used.astype(i32), src.reshape(-1), w_meta, back.reshape(-1)


def _swiglu(h):
    x_glu = jnp.minimum(h[:, :D_FF], SWIGLU_LIMIT)
    x_lin = jnp.clip(h[:, D_FF:], -SWIGLU_LIMIT, SWIGLU_LIMIT)
    return x_glu * _sigmoid(SWIGLU_ALPHA * x_glu) * (x_lin + 1.0)


def _chunk_gather_start(src_hbm, rows_ref, first, dst_ref, sem, both_queues=False):
    for c in range(dst_ref.shape[0] // MOE_CHUNK):
        r = pl.multiple_of(rows_ref[first + c], MOE_CHUNK)
        pltpu.make_async_copy(src_hbm.at[pl.ds(r, MOE_CHUNK)],
                              dst_ref.at[pl.ds(c * MOE_CHUNK, MOE_CHUNK)], sem).start(
                                  priority=c % 2 if both_queues else 0)


def _chunk_gather_wait(src_hbm, dst_ref, sem):
    pltpu.make_async_copy(src_hbm.at[pl.ds(0, dst_ref.shape[0])], dst_ref, sem).wait()


def _moe_kernel(blk_exp_ref, n_used_ref, rows_ref, wmeta_ref, xt_hbm, w1_hbm, b1_ref, w2_hbm, b2_ref, y_ref,
                xbuf, w1f, w2f, w1b, w2b, sems, wsems):
    i = pl.program_id(0)
    n_used = n_used_ref[0]
    n_blk = pl.num_programs(0)
    cpb = xbuf.shape[1] // MOE_CHUNK

    def fetch(blk, slot):
        _chunk_gather_start(xt_hbm, rows_ref, blk * cpb, xbuf.at[slot], sems.at[slot])

    def weight_copies(expert, slot):
        return (pltpu.make_async_copy(w1_hbm.at[expert], w1f.at[slot], wsems.at[0, slot]),
                pltpu.make_async_copy(w2_hbm.at[expert], w2f.at[slot], wsems.at[1, slot]))

    @pl.when(jnp.logical_and(i == 0, n_used > 0))
    def _():
        fetch(0, 0)
        for cp in weight_copies(blk_exp_ref[0], wmeta_ref[0]):
            cp.start()

    @pl.when(i < n_used)
    def _():
        slot = i % 2
        fetch(jnp.minimum(i + 1, n_used - 1), 1 - slot)
        _chunk_gather_wait(xt_hbm, xbuf.at[slot], sems.at[slot])
        expert = blk_exp_ref[i]
        changed = jnp.logical_or(i == 0, expert != blk_exp_ref[jnp.maximum(i - 1, 0)])

        @pl.when(changed)
        def _():
            w_slot = wmeta_ref[i]
            succ = wmeta_ref[n_blk + i]
            for cp in weight_copies(expert, w_slot):
                cp.wait()
            w1b[...] = w1f[w_slot].astype(BF16)
            w2b[...] = w2f[w_slot].astype(BF16)

            @pl.when(succ < N_EXPERTS)
            def _():
                for cp in weight_copies(succ, 1 - w_slot):
                    cp.start()

        h = jnp.dot(xbuf[slot].astype(BF16), w1b[...], preferred_element_type=F32) + b1_ref[0]
        act = _swiglu(h)
        y_ref[...] = jnp.dot(act.astype(BF16), w2b[...], preferred_element_type=F32) + b2_ref[0]

        @pl.when(i == n_used - 1)
        def _():
            _chunk_gather_wait(xt_hbm, xbuf.at[1 - slot], sems.at[1 - slot])

    @pl.when(i >= n_used)
    def _():
        y_ref[...] = jnp.zeros_like(y_ref)


def moe_experts(xt, blk_exp, n_used, chunk_rows, w_meta, w1, b1, w2, b2):
    d = xt.shape[1]
    nblk = blk_exp.shape[0]
    ne = w1.shape[0]
    bm = MOE_BM
    any_spec = pl.BlockSpec(memory_space=pl.ANY)
    grid_spec = pltpu.PrefetchScalarGridSpec(
        num_scalar_prefetch=4,
        grid=(nblk,),
        in_specs=[any_spec,
                  any_spec,
                  pl.BlockSpec((1, 1, 2 * D_FF), lambda i, be, nu, cr, wm: (be[i], 0, 0)),
                  any_spec,
                  pl.BlockSpec((1, 1, d), lambda i, be, nu, cr, wm: (be[i], 0, 0))],
        out_specs=pl.BlockSpec((bm, d), lambda i, be, nu, cr, wm: (i, 0)),
        scratch_shapes=[pltpu.VMEM((2, bm, d), F32),
                        pltpu.VMEM((2, d, 2 * D_FF), w1.dtype), pltpu.VMEM((2, D_FF, d), w2.dtype),
                        pltpu.VMEM((d, 2 * D_FF), BF16), pltpu.VMEM((D_FF, d), BF16),
                        pltpu.SemaphoreType.DMA((2,)), pltpu.SemaphoreType.DMA((2, 2))],
    )
    return pl.pallas_call(
        _moe_kernel,
        grid_spec=grid_spec,
        out_shape=jax.ShapeDtypeStruct((nblk * bm, d), F32),
        compiler_params=_cparams(("arbitrary",)),
        name="moe_experts",
    )(blk_exp, n_used, chunk_rows, w_meta, xt, w1, b1.reshape(ne, 1, -1), w2, b2.reshape(ne, 1, -1))


def _combine_kernel(rows_ref, ys_hbm, h1_ref, route_ref, p_ref, wple_ref, wpg_ref, g3_ref, gf_ref, y_ref,
                    ybuf, sems, *, tile0):
    i = pl.program_id(0)
    n = pl.num_programs(0)
    tm = h1_ref.shape[0]
    rows = ybuf.shape[1]
    cpt = rows // MOE_CHUNK

    def fetch(step, slot):
        _chunk_gather_start(ys_hbm, rows_ref, (tile0 + step) * cpt, ybuf.at[slot], sems.at[slot],
                            both_queues=True)

    @pl.when(i == 0)
    def _():
        fetch(0, 0)

    slot = i % 2
    fetch(jnp.minimum(i + 1, n - 1), 1 - slot)
    _chunk_gather_wait(ys_hbm, ybuf.at[slot], sems.at[slot])
    route = route_ref[...]
    row_id = lax.broadcasted_iota(jnp.int32, (tm, rows), 1)
    weight = jnp.zeros((tm, rows), F32)
    for k in range(TOP_K):
        p_k = route[:, ROUTE_POS + k:ROUTE_POS + k + 1].astype(jnp.int32)
        weight = jnp.where(row_id == p_k, route[:, ROUTE_GATE + k:ROUTE_GATE + k + 1], weight)
    h2 = h1_ref[...] + jnp.dot(weight.astype(BF16), ybuf[slot].astype(BF16), preferred_element_type=F32)
    ple = jnp.dot(p_ref[...].astype(BF16), wple_ref[...], preferred_element_type=F32)
    gate = _sigmoid(jnp.dot(_rms(h2, g3_ref[...]).astype(BF16), wpg_ref[...], preferred_element_type=F32))
    h3 = h2 + ple * gate
    y_ref[...] = _rms(h3, gf_ref[...])

    @pl.when(i == n - 1)
    def _():
        _chunk_gather_wait(ys_hbm, ybuf.at[1 - slot], sems.at[1 - slot])


def moe_combine(ys, back_rows, h1, route, p, w_ple, w_pg, norm3, norm_f, tile0, n_tiles):
    d = h1.shape[1]
    tm = MOE_TM

    def full(arr):
        return pl.BlockSpec(arr.shape, lambda i, br: (0,) * arr.ndim)

    weights = [w_ple, w_pg, norm3.reshape(1, d), norm_f.reshape(1, d)]
    grid_spec = pltpu.PrefetchScalarGridSpec(
        num_scalar_prefetch=1,
        grid=(n_tiles,),
        in_specs=[pl.BlockSpec(memory_space=pl.ANY),
                  pl.BlockSpec((tm, d), lambda i, br: (tile0 + i, 0)),
                  pl.BlockSpec((tm, LANES), lambda i, br: (tile0 + i, 0)),
                  pl.BlockSpec((tm, PLE_DIM), lambda i, br: (i, 0))] + [full(wt) for wt in weights],
        out_specs=pl.BlockSpec((tm, d), lambda i, br: (i, 0)),
        scratch_shapes=[pltpu.VMEM((2, MOE_TILE_ROWS, d), F32), pltpu.SemaphoreType.DMA((2,))],
    )
    return pl.pallas_call(
        functools.partial(_combine_kernel, tile0=tile0),
        grid_spec=grid_spec,
        out_shape=jax.ShapeDtypeStruct((n_tiles * tm, d), F32),
        compiler_params=_cparams(("arbitrary",)),
        name="moe_combine",
    )(back_rows, ys, h1, route, p, *weights)


def _gate_layouts(zg, batch, seq, chunk):
    nc = seq // chunk
    g = zg[:, :2 * M_HEADS].reshape(batch, nc, chunk, 2, M_HEADS)
    rows = jnp.transpose(g, (3, 0, 1, 4, 2))
    return (rows[0], rows[1]), g[:, :, :, 1, :]


def kernel(x_prompt, x_sample, state_C, state_n, state_m, cache_k0, cache_v0, cache_k1, cache_v1, cache_k2, cache_v2, p_prompt, p_sample, norm1, w_in, b_igate, b_fgate, m_norm, w_pa, w_pb, w_o, norm2, w_router, b_router, w1, b1, w2, b2, norm3, w_ple, w_ple_gate, norm_f):
    bp, seq, d = x_prompt.shape
    bs, t_dec, _ = x_sample.shape
    n_p, n_s = bp * seq, bs * t_dec
    n_all = n_p + n_s
    x_p, x_s = x_prompt.reshape(n_p, d), x_sample.reshape(n_s, d)

    w = w_in[0]
    c_gate = 4 * M_WIDTH
    c_att = c_gate + 2 * M_HEADS
    c_mg = c_att + 3 * A_WIDTH
    w_m = w[:, :c_gate].astype(BF16)
    w_gate = jnp.pad(w[:, c_gate:c_att], ((0, 0), (0, LANES - 2 * M_HEADS))).astype(BF16)
    w_att = w[:, c_att:c_mg].astype(BF16)
    w_mg = w[:, c_mg:].astype(BF16)
    rope = rope_tables(np.concatenate([np.arange(seq), np.tile(np.arange(t_dec) + PAST_LEN, bs)]))
    tiles_seq = seq // PROJ_TM
    qk_heads = 2 * A_WIDTH // LANES

    zm, zgate = norm_proj(x_p, x_s, norm1[0], w_m, BF16, 4 * M_WIDTH, w_extra=w_gate)
    zatt, *kv_p = norm_proj(x_p, None, norm1[0], w_att, F32, 3 * A_WIDTH, rope=rope, rope_heads=qk_heads,
                            rope_block=lambda i: i % tiles_seq, kv_seq=(bp, seq))
    z_new = norm_proj(x_s, None, norm1[0], w_att, F32, 3 * A_WIDTH, rope=rope, rope_heads=qk_heads,
                      rope_block=lambda i: tiles_seq + i)
    zmg = norm_proj(x_p, x_s, norm1[0], w_mg, BF16, 2048)

    caches = (cache_k0, cache_v0, cache_k1, cache_v1, cache_k2, cache_v2)
    news = []
    for g in range(len(A_GROUPS)):
        for part in range(2):
            col = (1 + part) * A_WIDTH + g * A_GROUP_WIDTH
            news.append(z_new[:, col:col + A_GROUP_WIDTH].reshape(bs, t_dec, A_HEADS_PER_GROUP, A_HEAD_DIM))
    bias = jnp.stack([b_igate[0], b_fgate[0]])
    chunk = 128
    grow_p, gcol_p = _gate_layouts(zgate[:n_p], bp, seq, chunk)
    t_pad = 8
    zm_s = jnp.pad(zm[n_p:].reshape(bs, t_dec, -1), ((0, 0), (0, t_pad - t_dec), (0, 0))).reshape(bs * t_pad, -1)
    zg_s = jnp.pad(zgate[n_p:].reshape(bs, t_dec, -1), ((0, 0), (0, t_pad - t_dec), (0, 0))).reshape(bs * t_pad, -1)
    grow_s, gcol_s = _gate_layouts(zg_s, bs, t_pad, t_pad)
    (hb_s, *kv_s), (ha_p, c_p, nn_p, m_p), (ha_s, c_s, nn_s, m_s), (hb_p,) = fused_call([
        window_part(z_new.reshape(bs, t_dec, 3 * A_WIDTH // LANES, LANES), caches, news),
        mlstm_part(zm, grow_p, gcol_p, bias, m_norm[0], bp, seq, chunk, chunk, 0),
        mlstm_part(zm_s, grow_s, gcol_s, bias, m_norm[0], bs, t_pad, t_pad, t_dec, 0,
                   state=(state_C[0], state_n[0], state_m[0])),
        attn_prompt_part(zatt, bp, seq),
    ], "mixers")
    nn_p, nn_s = nn_p.reshape(bp, M_HEADS, M_HEAD_DIM), nn_s.reshape(bs, M_HEADS, M_HEAD_DIM)
    m_p, m_s = m_p.reshape(bp, M_HEADS), m_s.reshape(bs, M_HEADS)
    ha_s = ha_s.reshape(bs, t_pad, M_WIDTH)[:, :t_dec].reshape(n_s, M_WIDTH)
    hb_s = hb_s.reshape(n_s, A_GROUP_WIDTH)

    w_r = jnp.pad(w_router[0], ((0, 0), (0, LANES - N_EXPERTS)))
    w_r_hi = w_r.astype(BF16)
    w_r_lo = (w_r - w_r_hi.astype(F32)).astype(BF16)
    b_r = jnp.pad(b_router[0], (0, LANES - N_EXPERTS)).reshape(1, LANES)
    h1, xt, route, cnt = merge_route((x_p, x_s), (ha_p, ha_s), (hb_p, hb_s), zmg, w_pa[0].astype(BF16),
                                     w_pb[0].astype(BF16), w_o[0].astype(BF16), norm2[0], w_r_hi, w_r_lo, b_r)
    nt = n_all // MOE_TM
    n_blocks = nt * MOE_TILE_ROWS // MOE_BM + N_EXPERTS + 1
    blk_exp, n_used, src_rows, w_meta, back_rows = moe_plan(cnt, n_blocks)
    ys = moe_experts(xt, blk_exp, n_used, src_rows, w_meta, w1[0], b1[0], w2[0], b2[0])
    tail_w = (w_ple[0].astype(BF16), w_ple_gate[0].astype(BF16), norm3[0], norm_f)
    nt_p = n_p // MOE_TM
    y_p = moe_combine(ys, back_rows, h1, route, p_prompt[0].reshape(n_p, PLE_DIM), *tail_w, 0, nt_p)
    y_s = moe_combine(ys, back_rows, h1, route, p_sample[0].reshape(n_s, PLE_DIM), *tail_w, nt_p, nt - nt_p)

    return (y_p.reshape(bp, seq, d), y_s.reshape(bs, t_dec, d),
            c_p[None], nn_p[None], m_p[None], *kv_p,
            c_s[None], nn_s[None], m_s[None], *kv_s)
```
